```python
import jax
import jax.numpy as jnp
from jax import lax
import numpy as np

D_MODEL = 1024
BATCH = 2
SEQ = 8192
DEPTH = 1

MEM_LEN = 256
ROPE_THETA = 500000.0
LN_EPS = 1e-5
ALPHA = (2.0 * DEPTH) ** 0.25
BETA = (8.0 * DEPTH) ** -0.25
GMLP_WIDTH = D_MODEL
GMLP_GROUPS = 8
GMLP_CHUNK = 128
NSA_HEADS = 16
NSA_HEAD_DIM = D_MODEL // NSA_HEADS
NSA_KV_HEADS = 4
NSA_GROUP = NSA_HEADS // NSA_KV_HEADS
NSA_KV_DIM = NSA_KV_HEADS * NSA_HEAD_DIM
ROT_DIM = NSA_HEAD_DIM // 4
CMP_BLOCK = 32
CMP_STRIDE = 16
CMP_HIDDEN = 4 * NSA_HEAD_DIM
SLC_BLOCK = 64
N_SELECT = 16
WINDOW = 512
Q_BLOCK = 128
XATTN_HEADS = 4
XATTN_HEAD_DIM = D_MODEL // XATTN_HEADS
D_FF = 4 * D_MODEL
IN_SIZES = (GMLP_WIDTH, GMLP_WIDTH, NSA_HEADS * NSA_HEAD_DIM, 6 * NSA_KV_DIM, 3 * NSA_HEADS, D_MODEL, D_MODEL)
IN_COLS = sum(IN_SIZES)

kernel_name = 'hybrid_gmlp_nsa_deepnorm_block'


def layer_norm(x, g, b):
    xf = x.astype(jnp.float32)
    mu = jnp.mean(xf, axis=-1, keepdims=True)
    var = jnp.mean(jnp.square(xf - mu), axis=-1, keepdims=True)
    y = (xf - mu) * lax.rsqrt(var + LN_EPS) * g.astype(jnp.float32) + b.astype(jnp.float32)
    return y.astype(x.dtype)


def masked_softmax(s, mask):
    s = jnp.where(mask, s.astype(jnp.float32), -1e30)
    p = jax.nn.softmax(s, axis=-1)
    return jnp.where(mask, p, 0.0)


def partial_rope(x, pos):
    half = ROT_DIM // 2
    inv = ROPE_THETA ** (-jnp.arange(half, dtype=jnp.float32) / half)
    ang = pos.astype(jnp.float32)[..., None] * inv
    cos = jnp.cos(ang)[:, :, None, :]
    sin = jnp.sin(ang)[:, :, None, :]
    xf = x.astype(jnp.float32)
    x1, x2, rest = xf[..., :half], xf[..., half:ROT_DIM], xf[..., ROT_DIM:]
    out = jnp.concatenate([x1 * cos - x2 * sin, x2 * cos + x1 * sin, rest], axis=-1)
    return out.astype(x.dtype)


def gmlp_spatial_gating(u, v, ln_g, ln_b, w_s, b_s):
    B, S, C = v.shape
    v = layer_norm(v, ln_g, ln_b)
    vg = v.reshape(B, S // GMLP_CHUNK, GMLP_CHUNK, GMLP_GROUPS, C // GMLP_GROUPS)
    w = jnp.tril(w_s)
    mixed = jnp.einsum('gts,bnsgc->bntgc', w, vg) + b_s.T[None, None, :, :, None]
    return u * mixed.reshape(B, S, C)


def compress_blocks(k, pe, w1, b1, w2, b2):
    B, S = k.shape[:2]
    n_cmp = (S - CMP_BLOCK) // CMP_STRIDE + 1
    idx = CMP_STRIDE * np.arange(n_cmp)[:, None] + np.arange(CMP_BLOCK)[None, :]
    blocks = k[:, idx] + pe[None, None, :, None, :]
    flat = blocks.transpose(0, 1, 3, 2, 4).reshape(B, n_cmp, NSA_KV_HEADS, CMP_BLOCK * NSA_HEAD_DIM)
    h = jax.nn.gelu(flat @ w1 + b1)
    return h @ w2 + b2


def nsa_attention(q, k_cmp, v_cmp, k_slc, v_slc, k_win, v_win, g_nsa, positions,
                  cmp_k_pe, cmp_k_w1, cmp_k_b1, cmp_k_w2, cmp_k_b2,
                  cmp_v_pe, cmp_v_w1, cmp_v_b1, cmp_v_w2, cmp_v_b2):
    B, S = q.shape[:2]
    dt = q.dtype
    n_cmp = (S - CMP_BLOCK) // CMP_STRIDE + 1
    n_slc = S // SLC_BLOCK
    n_sel = min(N_SELECT, n_slc)
    n_qb = S // Q_BLOCK
    scale = NSA_HEAD_DIM ** -0.5

    q = partial_rope(q, positions).reshape(B, S, NSA_KV_HEADS, NSA_GROUP, NSA_HEAD_DIM)
    k_slc = partial_rope(k_slc, positions)
    k_win = partial_rope(k_win, positions)
    cmp_start = CMP_STRIDE * np.arange(n_cmp)
    cmp_end = cmp_start + CMP_BLOCK - 1
    kc = partial_rope(compress_blocks(k_cmp, cmp_k_pe, cmp_k_w1, cmp_k_b1, cmp_k_w2, cmp_k_b2), positions[:, cmp_end])
    vc = compress_blocks(v_cmp, cmp_v_pe, cmp_v_w1, cmp_v_b1, cmp_v_w2, cmp_v_b2)

    slc_start = SLC_BLOCK * np.arange(n_slc)
    overlap = jnp.asarray(((cmp_start[:, None] <= slc_start[None, :] + SLC_BLOCK - 1)
                           & (cmp_end[:, None] >= slc_start[None, :])).astype(np.float32))

    ks_blocks = k_slc.reshape(B, n_slc, SLC_BLOCK, NSA_KV_HEADS, NSA_HEAD_DIM).transpose(0, 3, 1, 2, 4)
    vs_blocks = v_slc.reshape(B, n_slc, SLC_BLOCK, NSA_KV_HEADS, NSA_HEAD_DIM).transpose(0, 3, 1, 2, 4)
    pad = ((0, 0), (WINDOW, 0), (0, 0), (0, 0))
    kw_pad = jnp.pad(k_win, pad)
    vw_pad = jnp.pad(v_win, pad)
    gates = jax.nn.sigmoid(g_nsa.astype(jnp.float32)).astype(dt).reshape(B, S, NSA_KV_HEADS, NSA_GROUP, 3)
    b_ix = jnp.arange(B)[:, None, None, None]
    h_ix = jnp.arange(NSA_KV_HEADS)[None, :, None, None]
    cmp_end_j = jnp.asarray(cmp_end)
    blk = jnp.arange(n_slc)
    m_sel = n_sel * SLC_BLOCK

    def query_block(qb):
        q0 = qb * Q_BLOCK
        t = q0 + jnp.arange(Q_BLOCK)
        qblk = lax.dynamic_slice_in_dim(q, q0, Q_BLOCK, axis=1)
        gblk = lax.dynamic_slice_in_dim(gates, q0, Q_BLOCK, axis=1)
        s_c = jnp.einsum('bqhgd,bnhd->bhgqn', qblk, kc) * scale
        p_c = masked_softmax(s_c, cmp_end_j[None, :] <= t[:, None])
        o_c = jnp.einsum('bhgqn,bnhd->bqhgd', p_c.astype(dt), vc)
        p_slc = jnp.einsum('bhgqn,nj->bhqj', p_c, overlap)
        cur = t // SLC_BLOCK
        valid = blk[None, :] <= cur[:, None]
        forced = (blk[None, :] == 0) | (blk[None, :] == cur[:, None]) | (blk[None, :] == cur[:, None] - 1)
        score = jnp.where(forced, jnp.inf, jnp.where(valid, p_slc, -jnp.inf))
        _, sel = lax.top_k(score, n_sel)
        k_sel = ks_blocks[b_ix, h_ix, sel].reshape(B, NSA_KV_HEADS, Q_BLOCK, m_sel, NSA_HEAD_DIM)
        v_sel = vs_blocks[b_ix, h_ix, sel].reshape(B, NSA_KV_HEADS, Q_BLOCK, m_sel, NSA_HEAD_DIM)
        kpos = (sel[..., None] * SLC_BLOCK + jnp.arange(SLC_BLOCK)).reshape(B, NSA_KV_HEADS, 1, Q_BLOCK, m_sel)
        s_s = jnp.einsum('bqhgd,bhqmd->bhgqm', qblk, k_sel) * scale
        p_s = masked_softmax(s_s, kpos <= t[:, None])
        o_s = jnp.einsum('bhgqm,bhqmd->bqhgd', p_s.astype(dt), v_sel)
        kw = lax.dynamic_slice_in_dim(kw_pad, q0, Q_BLOCK + WINDOW, axis=1)
        vw = lax.dynamic_slice_in_dim(vw_pad, q0, Q_BLOCK + WINDOW, axis=1)
        kpos_w = q0 - WINDOW + jnp.arange(Q_BLOCK + WINDOW)
        diff = t[:, None] - kpos_w[None, :]
        mask_w = (diff >= 0) & (diff < WINDOW) & (kpos_w[None, :] >= 0)
        s_w = jnp.einsum('bqhgd,bkhd->bhgqk', qblk, kw) * scale
        p_w = masked_softmax(s_w, mask_w)
        o_w = jnp.einsum('bhgqk,bkhd->bqhgd', p_w.astype(dt), vw)
        return gblk[..., 0:1] * o_c + gblk[..., 1:2] * o_s + gblk[..., 2:3] * o_w

    o = lax.map(query_block, jnp.arange(n_qb))
    return o.transpose(1, 0, 2, 3, 4, 5).reshape(B, S, NSA_HEADS * NSA_HEAD_DIM)


def memory_cross_attention(h, mem, w_q, w_kv, w_o):
    B, S, _ = h.shape
    M = mem.shape[1]
    q = (h @ w_q).reshape(B, S, XATTN_HEADS, XATTN_HEAD_DIM)
    k, v = jnp.split(mem @ w_kv, 2, axis=-1)
    k = k.reshape(B, M, XATTN_HEADS, XATTN_HEAD_DIM)
    v = v.reshape(B, M, XATTN_HEADS, XATTN_HEAD_DIM)
    s = jnp.einsum('bshd,bmhd->bhsm', q, k).astype(jnp.float32) * (XATTN_HEAD_DIM ** -0.5)
    p = jax.nn.softmax(s, axis=-1).astype(h.dtype)
    o = jnp.einsum('bhsm,bmhd->bshd', p, v).reshape(B, S, D_MODEL)
    return o @ w_o


def setup_inputs(seed: int = 0) -> dict:
    key = jax.random.key(seed)
    ks = iter(jax.random.split(key, 48))
    L = DEPTH

    def nrm(shape, scale):
        return scale * jax.random.normal(next(ks), shape, jnp.float32)

    def gain(shape):
        return 1.0 + nrm(shape, 0.02)

    fl = CMP_BLOCK * NSA_HEAD_DIM
    return {
        'x': nrm((BATCH, SEQ, D_MODEL), 1.0),
        'mem': nrm((BATCH, MEM_LEN, D_MODEL), 1.0),
        'positions': jnp.broadcast_to(jnp.arange(SEQ, dtype=jnp.int32), (BATCH, SEQ)),
        'ln_in_g': gain((D_MODEL,)),
        'ln_in_b': nrm((D_MODEL,), 0.02),
        'w_in': nrm((L, D_MODEL, IN_COLS), D_MODEL ** -0.5),
        'gmlp_ln_g': gain((L, GMLP_WIDTH)),
        'gmlp_ln_b': nrm((L, GMLP_WIDTH), 0.02),
        'gmlp_ws': nrm((L, GMLP_GROUPS, GMLP_CHUNK, GMLP_CHUNK), GMLP_CHUNK ** -0.5),
        'gmlp_bs': gain((L, GMLP_GROUPS, GMLP_CHUNK)),
        'cmp_k_pe': nrm((L, CMP_BLOCK, NSA_HEAD_DIM), 0.02),
        'cmp_k_w1': nrm((L, fl, CMP_HIDDEN), fl ** -0.5),
        'cmp_k_b1': nrm((L, CMP_HIDDEN), 0.02),
        'cmp_k_w2': nrm((L, CMP_HIDDEN, NSA_HEAD_DIM), CMP_HIDDEN ** -0.5),
        'cmp_k_b2': nrm((L, NSA_HEAD_DIM), 0.02),
        'cmp_v_pe': nrm((L, CMP_BLOCK, NSA_HEAD_DIM), 0.02),
        'cmp_v_w1': nrm((L, fl, CMP_HIDDEN), fl ** -0.5),
        'cmp_v_b1': nrm((L, CMP_HIDDEN), 0.02),
        'cmp_v_w2': nrm((L, CMP_HIDDEN, NSA_HEAD_DIM), CMP_HIDDEN ** -0.5),
        'cmp_v_b2': nrm((L, NSA_HEAD_DIM), 0.02),
        'w_out': nrm((L, D_MODEL, D_MODEL), BETA * D_MODEL ** -0.5),
        'ln1_g': gain((L, D_MODEL)),
        'ln1_b': nrm((L, D_MODEL), 0.02),
        'w_xq': nrm((L, D_MODEL, D_MODEL), D_MODEL ** -0.5),
        'w_xkv': nrm((L, D_MODEL, 2 * D_MODEL), D_MODEL ** -0.5),
        'w_xo': nrm((L, D_MODEL, D_MODEL), BETA * D_MODEL ** -0.5),
        'ln2_g': gain((L, D_MODEL)),
        'ln2_b': nrm((L, D_MODEL), 0.02),
        'w_ff1': nrm((L, D_MODEL, D_FF), D_MODEL ** -0.5),
        'w_ff2': nrm((L, D_FF, D_MODEL), BETA * D_FF ** -0.5),
        'ln3_g': gain((L, D_MODEL)),
        'ln3_b': nrm((L, D_MODEL), 0.02),
    }


def reference(x, mem, positions, ln_in_g, ln_in_b, w_in, gmlp_ln_g, gmlp_ln_b, gmlp_ws, gmlp_bs,
              cmp_k_pe, cmp_k_w1, cmp_k_b1, cmp_k_w2, cmp_k_b2,
              cmp_v_pe, cmp_v_w1, cmp_v_b1, cmp_v_w2, cmp_v_b2,
              w_out, ln1_g, ln1_b, w_xq, w_xkv, w_xo, ln2_g, ln2_b, w_ff1, w_ff2, ln3_g, ln3_b):
    B, S, _ = x.shape
    split_idx = [int(c) for c in np.cumsum(IN_SIZES)[:-1]]
    h = layer_norm(x, ln_in_g, ln_in_b)
    for l in range(DEPTH):
        z = h @ w_in[l]
        u, v, q, kv, g_nsa, g_a, g_b = jnp.split(z, split_idx, axis=-1)
        y_a = gmlp_spatial_gating(jax.nn.gelu(u), jax.nn.gelu(v), gmlp_ln_g[l], gmlp_ln_b[l], gmlp_ws[l], gmlp_bs[l])
        k_cmp, v_cmp, k_slc, v_slc, k_win, v_win = [
            t.reshape(B, S, NSA_KV_HEADS, NSA_HEAD_DIM) for t in jnp.split(kv, 6, axis=-1)]
        y_b = nsa_attention(q.reshape(B, S, NSA_HEADS, NSA_HEAD_DIM), k_cmp, v_cmp, k_slc, v_slc, k_win, v_win,
                            g_nsa, positions,
                            cmp_k_pe[l], cmp_k_w1[l], cmp_k_b1[l], cmp_k_w2[l], cmp_k_b2[l],
                            cmp_v_pe[l], cmp_v_w1[l], cmp_v_b1[l], cmp_v_w2[l], cmp_v_b2[l])
        mix = jax.nn.sigmoid(g_a) * y_a + jax.nn.sigmoid(g_b) * y_b
        h = layer_norm(ALPHA * h + mix @ w_out[l], ln1_g[l], ln1_b[l])
        h = layer_norm(ALPHA * h + memory_cross_attention(h, mem, w_xq[l], w_xkv[l], w_xo[l]), ln2_g[l], ln2_b[l])
        ff = jnp.square(jax.nn.relu(h @ w_ff1[l])) @ w_ff2[l]
        h = layer_norm(ALPHA * h + ff, ln3_g[l], ln3_b[l])
    return h
```

```python
import functools

import jax
import jax.numpy as jnp
from jax import lax
from jax.experimental import pallas as pl
from jax.experimental.pallas import tpu as pltpu

D_MODEL = 1024
LN_EPS = 1e-5
ALPHA = 2.0 ** 0.25
ROPE_THETA = 500000.0
GMLP_GROUPS = 8
GMLP_CHUNK = 128
NSA_HEADS = 16
HEAD_DIM = 64
KV_HEADS = 4
GROUP = NSA_HEADS // KV_HEADS
KV_DIM = KV_HEADS * HEAD_DIM
ROT_DIM = HEAD_DIM // 4
CMP_BLOCK = 32
CMP_STRIDE = 16
CMP_HIDDEN = 4 * HEAD_DIM
SLC_BLOCK = 64
N_SELECT = 16
WINDOW = 512
Q_BLOCK = 128
XATTN_HEADS = 4
XATTN_HEAD_DIM = D_MODEL // XATTN_HEADS
D_FF = 4 * D_MODEL

LANES = 128
ROW_TILE = 256
SEL_CHUNK = 512
WIN_SPAN = WINDOW + Q_BLOCK
MASK_VALUE = -1e30
VMEM_LIMIT_BYTES = 56 * 1024 * 1024

_F32 = jnp.float32
_BF16 = jnp.bfloat16


def _layer_norm(x, g, b):
    mu = jnp.mean(x, axis=-1, keepdims=True)
    xc = x - mu
    var = jnp.mean(xc * xc, axis=-1, keepdims=True)
    return xc * lax.rsqrt(var + LN_EPS) * g + b


def _dot(a, b):
    return jnp.dot(a, b, preferred_element_type=_F32)


def _dot_nt(a, b):
    return lax.dot_general(a, b, (((1,), (1,)), ((), ())), preferred_element_type=_F32)


def _rope_tables(pos_f32, inv_lane, neg_first, pos_second):
    ang = pos_f32 * inv_lane
    c = jnp.cos(ang)
    s = jnp.sin(ang)
    return c, s * neg_first, s * pos_second


def _rope_block(x, tables):
    c, s_first, s_second = tables
    half = ROT_DIM // 2
    up = pltpu.roll(x, LANES - half, axis=1)
    down = pltpu.roll(x, half, axis=1)
    return x * c + up * s_first + down * s_second


def _const_spec(shape):
    nd = len(shape)
    return pl.BlockSpec(shape, lambda *_: (0,) * nd, pipeline_mode=pl.Buffered(1))


def _proj_kernel(x_ref, pos_ref, lng_ref, lnb_ref, wu_ref, wv_ref, wga_ref, wq_ref, wgb_ref, wkv_ref, wgn_ref,
                 glng_ref, glnb_ref, ws_ref, bst_ref, inv_ref, m1_ref, m2_ref,
                 h_ref, a_ref, gb_ref, q_ref, kvc_ref, ks_ref, vs_ref, kw_ref, vw_ref, gn_ref):
    tm = x_ref.shape[0]
    h = _layer_norm(x_ref[...], lng_ref[...], lnb_ref[...])
    h_ref[...] = h
    hb = h.astype(_BF16)

    u_act = jax.nn.gelu(_dot(hb, wu_ref[...]))
    vn = _layer_norm(jax.nn.gelu(_dot(hb, wv_ref[...])), glng_ref[...], glnb_ref[...]).astype(_BF16)
    gate_a = jax.nn.sigmoid(_dot(hb, wga_ref[...]))
    ua = u_act * gate_a
    ti = lax.broadcasted_iota(jnp.int32, (GMLP_CHUNK, GMLP_CHUNK), 0)
    si = lax.broadcasted_iota(jnp.int32, (GMLP_CHUNK, GMLP_CHUNK), 1)
    lower = si <= ti
    bst = bst_ref[...]
    for g in range(GMLP_GROUPS):
        w = jnp.where(lower, ws_ref[g], 0.0).astype(_BF16)
        bias = bst[:, g:g + 1]
        cs = slice(g * LANES, (g + 1) * LANES)
        for c in range(tm // GMLP_CHUNK):
            rs = slice(c * GMLP_CHUNK, (c + 1) * GMLP_CHUNK)
            mixed = _dot(w, vn[rs, cs]) + bias
            a_ref[rs, cs] = ua[rs, cs] * mixed

    gb_ref[...] = jax.nn.sigmoid(_dot(hb, wgb_ref[...]))
    gn_ref[...] = jax.nn.sigmoid(_dot(hb, wgn_ref[...]))

    tables = _rope_tables(pos_ref[...].astype(_F32), inv_ref[...], m1_ref[...], m2_ref[...])
    scale = HEAD_DIM ** -0.5
    zq = _dot(hb, wq_ref[...])
    for i in range(D_MODEL // LANES):
        cs = slice(i * LANES, (i + 1) * LANES)
        q_ref[:, cs] = (_rope_block(zq[:, cs], tables) * scale).astype(_BF16)

    zkv = _dot(hb, wkv_ref[...])
    ones_col = (lax.broadcasted_iota(jnp.int32, (tm, HEAD_DIM), 1) == 0).astype(_BF16)
    for j in range(2 * KV_HEADS):
        kvc_ref[j] = zkv[:, j * HEAD_DIM:(j + 1) * HEAD_DIM]
    for base, k_out, v_out in ((2 * KV_DIM, ks_ref, vs_ref), (4 * KV_DIM, kw_ref, vw_ref)):
        for i in range(KV_DIM // LANES):
            kr = _rope_block(zkv[:, base + i * LANES: base + (i + 1) * LANES], tables).astype(_BF16)
            for hh in range(LANES // HEAD_DIM):
                k_out[2 * i + hh] = kr[:, hh * HEAD_DIM:(hh + 1) * HEAD_DIM]
        for hh in range(KV_HEADS):
            v = zkv[:, base + KV_DIM + hh * HEAD_DIM: base + KV_DIM + (hh + 1) * HEAD_DIM].astype(_BF16)
            v_out[hh] = jnp.concatenate([v, ones_col], axis=1)


def _project(x2, pos2, ln_g, ln_b, w_parts, gln_g, gln_b, ws, bst, rope_consts):
    n = x2.shape[0]
    tm = ROW_TILE
    row = lambda shape: pl.BlockSpec(shape, lambda i: (i, 0))
    head_rows = lambda nh, w: pl.BlockSpec((nh, tm, w), lambda i: (0, i, 0))
    w_specs = [_const_spec(w.shape) for w in w_parts]
    in_specs = ([row((tm, D_MODEL)), row((tm, 1)), _const_spec(ln_g.shape), _const_spec(ln_b.shape)] + w_specs
                + [_const_spec(gln_g.shape), _const_spec(gln_b.shape), _const_spec(ws.shape), _const_spec(bst.shape)]
                + [_const_spec(c.shape) for c in rope_consts])
    out_shape = (
        jax.ShapeDtypeStruct((n, D_MODEL), _F32),
        jax.ShapeDtypeStruct((n, D_MODEL), _F32),
        jax.ShapeDtypeStruct((n, D_MODEL), _F32),
        jax.ShapeDtypeStruct((n, D_MODEL), _BF16),
        jax.ShapeDtypeStruct((2 * KV_HEADS, n, HEAD_DIM), _F32),
        jax.ShapeDtypeStruct((KV_HEADS, n, HEAD_DIM), _BF16),
        jax.ShapeDtypeStruct((KV_HEADS, n, LANES), _BF16),
        jax.ShapeDtypeStruct((KV_HEADS, n, HEAD_DIM), _BF16),
        jax.ShapeDtypeStruct((KV_HEADS, n, LANES), _BF16),
        jax.ShapeDtypeStruct((n, KV_HEADS * LANES), _F32),
    )
    out_specs = (row((tm, D_MODEL)), row((tm, D_MODEL)), row((tm, D_MODEL)), row((tm, D_MODEL)),
                 head_rows(2 * KV_HEADS, HEAD_DIM), head_rows(KV_HEADS, HEAD_DIM), head_rows(KV_HEADS, LANES),
                 head_rows(KV_HEADS, HEAD_DIM), head_rows(KV_HEADS, LANES), row((tm, KV_HEADS * LANES)))
    return pl.pallas_call(
        _proj_kernel, grid=(n // tm,), in_specs=in_specs, out_specs=out_specs, out_shape=out_shape,
        compiler_params=pltpu.CompilerParams(dimension_semantics=("arbitrary",), vmem_limit_bytes=VMEM_LIMIT_BYTES),
        name="nsa_proj",
    )(x2, pos2, ln_g, ln_b, *w_parts, gln_g, gln_b, ws, bst, *rope_consts)


def _compress_kernel(x_ref, pos_ref, pe_ref, w1_ref, b1_ref, w2_ref, b2_ref, inv_ref, m1_ref, m2_ref,
                     o_ref, perm_ref):
    rows = x_ref.shape[0]
    is_key = (pl.program_id(1) == 0).astype(_F32)
    x = x_ref[...]
    half = CMP_STRIDE * HEAD_DIM
    ya = _dot((x + pe_ref[0:1, :]).astype(_BF16), w1_ref[0:half, :])
    yb = _dot((x + pe_ref[1:2, :]).astype(_BF16), w1_ref[half:2 * half, :])
    hid = jax.nn.gelu(ya + pltpu.roll(yb, rows - 1, axis=0) + b1_ref[...])
    out = _dot(hid.astype(_BF16), w2_ref[...]) + b2_ref[...]
    tables = _rope_tables(pos_ref[...].astype(_F32) * is_key, inv_ref[...], m1_ref[...], m2_ref[...])
    perm_ref[...] = _rope_block(out, tables)
    groups = rows // 4
    for r in range(4):
        o_ref[r * groups:(r + 1) * groups, :] = perm_ref[pl.ds(r, groups, stride=4), :][:, :HEAD_DIM].astype(_BF16)


def _compress(kvc, pos_end, pe, w1, b1, w2, b2, rope_consts, batch):
    rows = kvc.shape[2]
    flat = CMP_STRIDE * HEAD_DIM
    in_specs = [
        pl.BlockSpec((None, None, rows, flat), lambda b, kv, hh: (kv * KV_HEADS + hh, b, 0, 0)),
        pl.BlockSpec((None, rows, 1), lambda b, kv, hh: (b, 0, 0)),
        pl.BlockSpec((None, 2, flat), lambda b, kv, hh: (kv, 0, 0)),
        pl.BlockSpec((None, 2 * flat, CMP_HIDDEN), lambda b, kv, hh: (kv, 0, 0)),
        pl.BlockSpec((None, 1, CMP_HIDDEN), lambda b, kv, hh: (kv, 0, 0)),
        pl.BlockSpec((None, CMP_HIDDEN, LANES), lambda b, kv, hh: (kv, 0, 0)),
        pl.BlockSpec((None, 1, LANES), lambda b, kv, hh: (kv, 0, 0)),
    ] + [pl.BlockSpec(c.shape, lambda b, kv, hh: (0, 0)) for c in rope_consts]
    return pl.pallas_call(
        _compress_kernel, grid=(batch, 2, KV_HEADS), in_specs=in_specs,
        out_specs=pl.BlockSpec((None, None, None, rows, HEAD_DIM), lambda b, kv, hh: (kv, b, hh, 0, 0)),
        out_shape=jax.ShapeDtypeStruct((2, batch, KV_HEADS, rows, HEAD_DIM), _BF16),
        scratch_shapes=[pltpu.VMEM((rows, LANES), _F32)],
        compiler_params=pltpu.CompilerParams(dimension_semantics=("arbitrary",) * 3, vmem_limit_bytes=VMEM_LIMIT_BYTES),
        name="nsa_compress",
    )(kvc, pos_end, pe, w1, b1, w2, b2, *rope_consts)


def _nsa_kernel(q_ref, gn_ref, kc_ref, vc_ref, ks_ref, vs_ref, kw_ref, vw_ref, e_ref, o_ref, *, n_sel):
    rows = kc_ref.shape[0]
    n_slc = rows // 4
    seq = ks_ref.shape[0]
    gq = GROUP * Q_BLOCK
    q0 = pl.program_id(2) * Q_BLOCK
    t = q0 + lax.broadcasted_iota(jnp.int32, (Q_BLOCK, 1), 0)

    q = q_ref[...]
    qs = jnp.concatenate([q[:, g * HEAD_DIM:(g + 1) * HEAD_DIM] for g in range(GROUP)], axis=0)

    slot = lax.broadcasted_iota(jnp.int32, (1, rows), 1)
    shift = n_slc.bit_length() - 1
    cmp_idx = 4 * (slot & (n_slc - 1)) + lax.shift_right_logical(slot, shift)
    valid_c = (CMP_STRIDE * cmp_idx + CMP_BLOCK - 1 <= t) & (cmp_idx < rows - 1)
    s_c = _dot_nt(qs, kc_ref[...]).reshape(GROUP, Q_BLOCK, rows)
    s_c = jnp.where(valid_c[None], s_c, MASK_VALUE)
    e_c = jnp.where(valid_c[None], jnp.exp(s_c - jnp.max(s_c, axis=-1, keepdims=True)), 0.0)
    l_c = jnp.sum(e_c, axis=-1, keepdims=True)
    p_c = jnp.where(l_c > 0.0, e_c / l_c, 0.0)
    o_c = _dot(p_c.reshape(gq, rows).astype(_BF16), vc_ref[...])

    p_sum = p_c[0]
    for g in range(1, GROUP):
        p_sum = p_sum + p_c[g]
    parts = [p_sum[:, r * n_slc:(r + 1) * n_slc] for r in range(4)]
    blk = lax.broadcasted_iota(jnp.int32, (1, n_slc), 1)
    prev3 = jnp.where(blk == 0, 0.0, pltpu.roll(parts[3], 1, axis=1))
    p_slc = parts[0] + parts[1] + parts[2] + parts[3] + prev3
    cur = lax.shift_right_logical(t, SLC_BLOCK.bit_length() - 1)
    forced = (blk == 0) | (blk == cur) | (blk == cur - 1)
    score = jnp.where(forced, jnp.inf, jnp.where(blk <= cur, p_slc, -1.0))

    st = score.T
    bi = lax.broadcasted_iota(jnp.int32, (n_slc, Q_BLOCK), 0)
    sel_t = jnp.zeros((n_slc, Q_BLOCK), _F32)
    for _ in range(n_sel):
        mx = jnp.max(st, axis=0, keepdims=True)
        first = jnp.min(jnp.where(st == mx, bi, n_slc), axis=0, keepdims=True)
        pick = bi == first
        sel_t = jnp.where(pick, 1.0, sel_t)
        st = jnp.where(pick, -jnp.inf, st)
    sel = sel_t.T.astype(_BF16)

    def sel_step(c, carry):
        m, acc = carry
        k0 = pl.multiple_of(c * SEL_CHUNK, SEL_CHUNK)
        in_sel = _dot(sel, e_ref[c])
        kpos = k0 + lax.broadcasted_iota(jnp.int32, (1, SEL_CHUNK), 1)
        bias = jnp.where((in_sel > 0.5) & (kpos <= t), 0.0, MASK_VALUE)
        s = _dot_nt(qs, ks_ref[pl.ds(k0, SEL_CHUNK), :]).reshape(GROUP, Q_BLOCK, SEL_CHUNK) + bias[None]
        s = s.reshape(gq, SEL_CHUNK)
        m_new = jnp.maximum(m, jnp.max(s, axis=-1, keepdims=True))
        p = jnp.exp(s - m_new)
        acc = jnp.exp(m - m_new) * acc + _dot(p.astype(_BF16), vs_ref[pl.ds(k0, SEL_CHUNK), :])
        return m_new, acc

    n_chunks = (q0 + Q_BLOCK - 1) // SEL_CHUNK + 1
    _, acc_s = lax.fori_loop(0, n_chunks, sel_step,
                             (jnp.full((gq, 1), MASK_VALUE, _F32), jnp.zeros((gq, LANES), _F32)))
    o_s = acc_s[:, :HEAD_DIM] / acc_s[:, HEAD_DIM:HEAD_DIM + 1]

    w0 = pl.multiple_of(jnp.maximum(q0 - WINDOW, 0), Q_BLOCK)
    kpos_w = w0 + lax.broadcasted_iota(jnp.int32, (1, WIN_SPAN), 1)
    diff = t - kpos_w
    bias_w = jnp.where((diff >= 0) & (diff < WINDOW), 0.0, MASK_VALUE)
    s_w = _dot_nt(qs, kw_ref[pl.ds(w0, WIN_SPAN), :]).reshape(GROUP, Q_BLOCK, WIN_SPAN) + bias_w[None]
    s_w = s_w.reshape(gq, WIN_SPAN)
    p_w = jnp.exp(s_w - jnp.max(s_w, axis=-1, keepdims=True))
    acc_w = _dot(p_w.astype(_BF16), vw_ref[pl.ds(w0, WIN_SPAN), :])
    o_w = acc_w[:, :HEAD_DIM] / acc_w[:, HEAD_DIM:HEAD_DIM + 1]

    gates = gn_ref[...]
    outs = []
    for g in range(GROUP):
        rs = slice(g * Q_BLOCK, (g + 1) * Q_BLOCK)
        outs.append(gates[:, 3 * g:3 * g + 1] * o_c[rs] + gates[:, 3 * g + 1:3 * g + 2] * o_s[rs]
                    + gates[:, 3 * g + 2:3 * g + 3] * o_w[rs])
    o_ref[...] = jnp.concatenate(outs, axis=1)


def _nsa(q, gn, kvc, ks, vs, kw, vw, expand, batch, seq):
    n = q.shape[0]
    n_qb = seq // Q_BLOCK
    rows = kvc.shape[3]
    n_slc = seq // SLC_BLOCK
    qrow = lambda w: pl.BlockSpec((Q_BLOCK, w), lambda b, hh, i: (b * n_qb + i, hh))
    seq_rows = lambda w: pl.BlockSpec((None, seq, w), lambda b, hh, i: (hh, b, 0))
    in_specs = [
        qrow(GROUP * HEAD_DIM), qrow(LANES),
        pl.BlockSpec((None, None, None, rows, HEAD_DIM), lambda b, hh, i: (0, b, hh, 0, 0)),
        pl.BlockSpec((None, None, None, rows, HEAD_DIM), lambda b, hh, i: (1, b, hh, 0, 0)),
        seq_rows(HEAD_DIM), seq_rows(LANES), seq_rows(HEAD_DIM), seq_rows(LANES),
        pl.BlockSpec(expand.shape, lambda b, hh, i: (0, 0, 0)),
    ]
    return pl.pallas_call(
        functools.partial(_nsa_kernel, n_sel=min(N_SELECT, n_slc)),
        grid=(batch, KV_HEADS, n_qb), in_specs=in_specs, out_specs=qrow(GROUP * HEAD_DIM),
        out_shape=jax.ShapeDtypeStruct((n, D_MODEL), _F32),
        compiler_params=pltpu.CompilerParams(dimension_semantics=("arbitrary",) * 3, vmem_limit_bytes=VMEM_LIMIT_BYTES),
        name="nsa_attention",
    )(q, gn, kvc, kvc, ks, vs, kw, vw, expand)


def _memkv_kernel(mem_ref, w_ref, o_ref):
    o_ref[...] = _dot(mem_ref[...].astype(_BF16), w_ref[...]).astype(_BF16)


def _memkv(mem2, w_xkv):
    m = mem2.shape[0]
    return pl.pallas_call(
        _memkv_kernel, grid=(1,),
        in_specs=[pl.BlockSpec(mem2.shape, lambda i: (0, 0)), pl.BlockSpec(w_xkv.shape, lambda i: (0, 0))],
        out_specs=pl.BlockSpec((m, 2 * D_MODEL), lambda i: (0, 0)),
        out_shape=jax.ShapeDtypeStruct((m, 2 * D_MODEL), _BF16),
        compiler_params=pltpu.CompilerParams(vmem_limit_bytes=VMEM_LIMIT_BYTES),
        name="mem_kv",
    )(mem2, w_xkv)


def _trunk_kernel(a_ref, gb_ref, yb_ref, h_ref, kvm_ref, wo_ref, wxq_ref, wxo_ref, wf1_ref, wf2_ref,
                  g1_ref, b1_ref, g2_ref, b2_ref, g3_ref, b3_ref, o_ref):
    mix = a_ref[...] + gb_ref[...] * yb_ref[...]
    h1 = _layer_norm(ALPHA * h_ref[...] + _dot(mix.astype(_BF16), wo_ref[...]), g1_ref[...], b1_ref[...])

    qx = _dot(h1.astype(_BF16), wxq_ref[...]).astype(_BF16)
    heads = []
    for hh in range(XATTN_HEADS):
        cs = slice(hh * XATTN_HEAD_DIM, (hh + 1) * XATTN_HEAD_DIM)
        vcs = slice(D_MODEL + hh * XATTN_HEAD_DIM, D_MODEL + (hh + 1) * XATTN_HEAD_DIM)
        s = _dot_nt(qx[:, cs], kvm_ref[:, cs]) * (XATTN_HEAD_DIM ** -0.5)
        e = jnp.exp(s - jnp.max(s, axis=-1, keepdims=True))
        p = e / jnp.sum(e, axis=-1, keepdims=True)
        heads.append(_dot(p.astype(_BF16), kvm_ref[:, vcs]))
    xo = jnp.concatenate(heads, axis=1)
    h2 = _layer_norm(ALPHA * h1 + _dot(xo.astype(_BF16), wxo_ref[...]), g2_ref[...], b2_ref[...])

    h2b = h2.astype(_BF16)
    ff = jnp.zeros_like(h2)
    for c in range(D_FF // D_MODEL):
        cs = slice(c * D_MODEL, (c + 1) * D_MODEL)
        act = jnp.square(jnp.maximum(_dot(h2b, wf1_ref[:, cs]), 0.0))
        ff = ff + _dot(act.astype(_BF16), wf2_ref[cs, :])
    o_ref[...] = _layer_norm(ALPHA * h2 + ff, g3_ref[...], b3_ref[...])


def _trunk(a, gb, yb, h, kvm, weights, lns, seq, mem_len):
    n = a.shape[0]
    tm = ROW_TILE
    tiles_per_batch = seq // tm
    row = pl.BlockSpec((tm, D_MODEL), lambda i: (i, 0))
    in_specs = ([row, row, row, row, pl.BlockSpec((mem_len, 2 * D_MODEL), lambda i: (i // tiles_per_batch, 0))]
                + [_const_spec(w.shape) for w in weights] + [_const_spec(p.shape) for p in lns])
    return pl.pallas_call(
        _trunk_kernel, grid=(n // tm,), in_specs=in_specs, out_specs=row,
        out_shape=jax.ShapeDtypeStruct((n, D_MODEL), _F32),
        compiler_params=pltpu.CompilerParams(dimension_semantics=("arbitrary",), vmem_limit_bytes=VMEM_LIMIT_BYTES),
        name="trunk",
    )(a, gb, yb, h, kvm, *weights, *lns)


def _rope_constants():
    half = ROT_DIM // 2
    inv = ROPE_THETA ** (-jnp.arange(half, dtype=_F32) / half)
    d = jnp.arange(LANES) % HEAD_DIM
    inv_lane = jnp.where(d < ROT_DIM, inv[d % half], 0.0).astype(_F32)[None, :]
    neg_first = jnp.where(d < half, -1.0, 0.0).astype(_F32)[None, :]
    pos_second = jnp.where((d >= half) & (d < ROT_DIM), 1.0, 0.0).astype(_F32)[None, :]
    return inv_lane, neg_first, pos_second


def kernel(x, mem, positions, ln_in_g, ln_in_b, w_in, gmlp_ln_g, gmlp_ln_b, gmlp_ws, gmlp_bs, cmp_k_pe, cmp_k_w1, cmp_k_b1, cmp_k_w2, cmp_k_b2, cmp_v_pe, cmp_v_w1, cmp_v_b1, cmp_v_w2, cmp_v_b2, w_out, ln1_g, ln1_b, w_xq, w_xkv, w_xo, ln2_g, ln2_b, w_ff1, w_ff2, ln3_g, ln3_b):
    batch, seq, _ = x.shape
    mem_len = mem.shape[1]
    n = batch * seq
    n_slc = seq // SLC_BLOCK
    assert w_in.shape[0] == 1, "one layer"
    assert seq % SEL_CHUNK == 0 and seq >= WIN_SPAN and n_slc & (n_slc - 1) == 0

    rope_consts = _rope_constants()
    vec = lambda p: p.reshape(1, -1)

    wi = w_in[0]
    o_u, o_v, o_q, o_kv, o_gn, o_ga, o_gb = (0, 1024, 2048, 3072, 3072 + 6 * KV_DIM, 3120 + 6 * KV_DIM, 4144 + 6 * KV_DIM)
    w_gn = wi[:, o_gn:o_ga].reshape(D_MODEL, KV_HEADS, 3 * GROUP)
    w_gn = jnp.pad(w_gn, ((0, 0), (0, 0), (0, LANES - 3 * GROUP))).reshape(D_MODEL, KV_HEADS * LANES)
    w_parts = [wi[:, o_u:o_v], wi[:, o_v:o_q], wi[:, o_ga:o_gb], wi[:, o_q:o_kv], wi[:, o_gb:], wi[:, o_kv:o_gn], w_gn]
    w_parts = [w.astype(_BF16) for w in w_parts]

    h, a, gb, q, kvc, ks, vs, kw, vw, gn = _project(
        x.reshape(n, D_MODEL), positions.reshape(n, 1), vec(ln_in_g), vec(ln_in_b), w_parts,
        vec(gmlp_ln_g[0]), vec(gmlp_ln_b[0]), gmlp_ws[0], gmlp_bs[0].T, rope_consts)

    rows = seq // CMP_STRIDE
    flat = CMP_STRIDE * HEAD_DIM
    pos_end = jnp.pad(positions[:, CMP_BLOCK - 1::CMP_STRIDE], ((0, 0), (0, 1)))[:, :, None]
    pad_lanes = lambda w: jnp.pad(w, ((0, 0), (0, LANES - HEAD_DIM)))
    kvcmp = _compress(
        kvc.reshape(2 * KV_HEADS, batch, rows, flat), pos_end,
        jnp.stack([cmp_k_pe[0].reshape(2, flat), cmp_v_pe[0].reshape(2, flat)]),
        jnp.stack([cmp_k_w1[0], cmp_v_w1[0]]).astype(_BF16),
        jnp.stack([vec(cmp_k_b1[0]), vec(cmp_v_b1[0])]),
        jnp.stack([pad_lanes(cmp_k_w2[0]), pad_lanes(cmp_v_w2[0])]).astype(_BF16),
        jnp.stack([pad_lanes(vec(cmp_k_b2[0])), pad_lanes(vec(cmp_v_b2[0]))]),
        rope_consts, batch)

    key_blk = (jnp.arange(seq) // SLC_BLOCK).reshape(seq // SEL_CHUNK, 1, SEL_CHUNK)
    expand = (key_blk == jnp.arange(n_slc)[None, :, None]).astype(_BF16)
    yb = _nsa(q, gn, kvcmp, ks, vs, kw, vw, expand, batch, seq)

    kvm = _memkv(mem.reshape(batch * mem_len, D_MODEL), w_xkv[0].astype(_BF16))
    weights = [w.astype(_BF16) for w in (w_out[0], w_xq[0], w_xo[0], w_ff1[0], w_ff2[0])]
    lns = [vec(p[0]) for p in (ln1_g, ln1_b, ln2_g, ln2_b, ln3_g, ln3_b)]
    out = _trunk(a, gb, yb, h, kvm, weights, lns, seq, mem_len)
    return out.reshape(batch, seq, D_MODEL)
```

```python
import functools

import jax
import jax.numpy as jnp
from jax import lax
from jax.experimental import pallas as pl
from jax.experimental.pallas import tpu as pltpu

D_MODEL = 1024
LN_EPS = 1e-5
ALPHA = 2.0 ** 0.25
ROPE_THETA = 500000.0
GMLP_GROUPS = 8
GMLP_CHUNK = 128
NSA_HEADS = 16
HEAD_DIM = 64
KV_HEADS = 4
GROUP = NSA_HEADS // KV_HEADS
KV_DIM = KV_HEADS * HEAD_DIM
ROT_DIM = HEAD_DIM // 4
CMP_BLOCK = 32
CMP_STRIDE = 16
CMP_HIDDEN = 4 * HEAD_DIM
SLC_BLOCK = 64
N_SELECT = 16
WINDOW = 512
Q_BLOCK = 128
XATTN_HEADS = 4
XATTN_HEAD_DIM = D_MODEL // XATTN_HEADS
D_FF = 4 * D_MODEL

LANES = 128
ROW_TILE = 256
SEL_CHUNK = 512
WIN_SPAN = WINDOW + Q_BLOCK
MASK_VALUE = -1e30
VMEM_LIMIT_BYTES = 56 * 1024 * 1024

_F32 = jnp.float32
_BF16 = jnp.bfloat16


def _layer_norm(x, g, b):
    mu = jnp.mean(x, axis=-1, keepdims=True)
    xc = x - mu
    var = jnp.mean(xc * xc, axis=-1, keepdims=True)
    return xc * lax.rsqrt(var + LN_EPS) * g + b


def _dot(a, b):
    return jnp.dot(a, b, preferred_element_type=_F32)


def _dot_nt(a, b):
    return lax.dot_general(a, b, (((1,), (1,)), ((), ())), preferred_element_type=_F32)


def _rope_tables(pos_f32, inv_lane, neg_first, pos_second):
    ang = pos_f32 * inv_lane
    c = jnp.cos(ang)
    s = jnp.sin(ang)
    return c, s * neg_first, s * pos_second


def _rope_block(x, tables):
    c, s_first, s_second = tables
    half = ROT_DIM // 2
    up = pltpu.roll(x, LANES - half, axis=1)
    down = pltpu.roll(x, half, axis=1)
    return x * c + up * s_first + down * s_second


def _const_spec(shape):
    nd = len(shape)
    return pl.BlockSpec(shape, lambda *_: (0,) * nd, pipeline_mode=pl.Buffered(1))


def _proj_kernel(x_ref, pos_ref, lng_ref, lnb_ref, wu_ref, wv_ref, wga_ref, wq_ref, wgb_ref, wkv_ref, wgn_ref,
                 glng_ref, glnb_ref, ws_ref, bst_ref, inv_ref, m1_ref, m2_ref,
                 h_ref, a_ref, gb_ref, q_ref, kvc_ref, ks_ref, vs_ref, kw_ref, vw_ref, gn_ref, *, seq):
    tm = x_ref.shape[0]
    h = _layer_norm(x_ref[...], lng_ref[...], lnb_ref[...])
    h_ref[...] = h
    hb = h.astype(_BF16)

    u_act = jax.nn.gelu(_dot(hb, wu_ref[...]))
    vn = _layer_norm(jax.nn.gelu(_dot(hb, wv_ref[...])), glng_ref[...], glnb_ref[...]).astype(_BF16)
    gate_a = jax.nn.sigmoid(_dot(hb, wga_ref[...]))
    ua = u_act * gate_a
    ti = lax.broadcasted_iota(jnp.int32, (GMLP_CHUNK, GMLP_CHUNK), 0)
    si = lax.broadcasted_iota(jnp.int32, (GMLP_CHUNK, GMLP_CHUNK), 1)
    lower = si <= ti
    bst = bst_ref[...]
    for g in range(GMLP_GROUPS):
        w = jnp.where(lower, ws_ref[g], 0.0).astype(_BF16)
        bias = bst[:, g:g + 1]
        cs = slice(g * LANES, (g + 1) * LANES)
        for c in range(tm // GMLP_CHUNK):
            rs = slice(c * GMLP_CHUNK, (c + 1) * GMLP_CHUNK)
            mixed = _dot(w, vn[rs, cs]) + bias
            a_ref[rs, cs] = ua[rs, cs] * mixed

    gb_ref[...] = jax.nn.sigmoid(_dot(hb, wgb_ref[...]))
    gn_ref[...] = jax.nn.sigmoid(_dot(hb, wgn_ref[...]))

    tables = _rope_tables(pos_ref[...].astype(_F32), inv_ref[...], m1_ref[...], m2_ref[...])
    scale = HEAD_DIM ** -0.5
    zq = _dot(hb, wq_ref[...])
    for i in range(D_MODEL // LANES):
        cs = slice(i * LANES, (i + 1) * LANES)
        q_ref[:, cs] = (_rope_block(zq[:, cs], tables) * scale).astype(_BF16)

    zkv = _dot(hb, wkv_ref[...])
    ones_col = (lax.broadcasted_iota(jnp.int32, (tm, HEAD_DIM), 1) == 0).astype(_BF16)
    seq_pos = (pl.program_id(0) * tm + lax.broadcasted_iota(jnp.int32, (tm, LANES), 0)) & (seq - 1)
    blk_lane = lax.broadcasted_iota(jnp.int32, (tm, LANES), 1)
    blk_onehot = jnp.where(lax.shift_right_logical(seq_pos, SLC_BLOCK.bit_length() - 1) == blk_lane, 1.0, 0.0)
    k_tail = jnp.concatenate([jnp.zeros((tm, HEAD_DIM), _BF16), blk_onehot.astype(_BF16)], axis=1)
    for j in range(2 * KV_HEADS):
        kvc_ref[j] = zkv[:, j * HEAD_DIM:(j + 1) * HEAD_DIM]
    for base, k_out, v_out in ((2 * KV_DIM, ks_ref, vs_ref), (4 * KV_DIM, kw_ref, vw_ref)):
        for i in range(KV_DIM // LANES):
            kr = _rope_block(zkv[:, base + i * LANES: base + (i + 1) * LANES], tables).astype(_BF16)
            for hh in range(LANES // HEAD_DIM):
                k_head = kr[:, hh * HEAD_DIM:(hh + 1) * HEAD_DIM]
                if k_out is ks_ref:
                    k_out[2 * i + hh] = jnp.concatenate([k_head, k_tail], axis=1)
                else:
                    k_out[2 * i + hh] = k_head
        for hh in range(KV_HEADS):
            v = zkv[:, base + KV_DIM + hh * HEAD_DIM: base + KV_DIM + (hh + 1) * HEAD_DIM].astype(_BF16)
            v_out[hh] = jnp.concatenate([v, ones_col], axis=1)


def _project(x2, pos2, ln_g, ln_b, w_parts, gln_g, gln_b, ws, bst, rope_consts, seq):
    n = x2.shape[0]
    tm = ROW_TILE
    row = lambda shape: pl.BlockSpec(shape, lambda i: (i, 0))
    head_rows = lambda nh, w: pl.BlockSpec((nh, tm, w), lambda i: (0, i, 0))
    w_specs = [_const_spec(w.shape) for w in w_parts]
    in_specs = ([row((tm, D_MODEL)), row((tm, 1)), _const_spec(ln_g.shape), _const_spec(ln_b.shape)] + w_specs
                + [_const_spec(gln_g.shape), _const_spec(gln_b.shape), _const_spec(ws.shape), _const_spec(bst.shape)]
                + [_const_spec(c.shape) for c in rope_consts])
    out_shape = (
        jax.ShapeDtypeStruct((n, D_MODEL), _F32),
        jax.ShapeDtypeStruct((n, D_MODEL), _F32),
        jax.ShapeDtypeStruct((n, D_MODEL), _F32),
        jax.ShapeDtypeStruct((n, D_MODEL), _BF16),
        jax.ShapeDtypeStruct((2 * KV_HEADS, n, HEAD_DIM), _F32),
        jax.ShapeDtypeStruct((KV_HEADS, n, 2 * LANES), _BF16),
        jax.ShapeDtypeStruct((KV_HEADS, n, LANES), _BF16),
        jax.ShapeDtypeStruct((KV_HEADS, n, HEAD_DIM), _BF16),
        jax.ShapeDtypeStruct((KV_HEADS, n, LANES), _BF16),
        jax.ShapeDtypeStruct((n, KV_HEADS * LANES), _F32),
    )
    out_specs = (row((tm, D_MODEL)), row((tm, D_MODEL)), row((tm, D_MODEL)), row((tm, D_MODEL)),
                 head_rows(2 * KV_HEADS, HEAD_DIM), head_rows(KV_HEADS, 2 * LANES), head_rows(KV_HEADS, LANES),
                 head_rows(KV_HEADS, HEAD_DIM), head_rows(KV_HEADS, LANES), row((tm, KV_HEADS * LANES)))
    return pl.pallas_call(
        functools.partial(_proj_kernel, seq=seq), grid=(n // tm,), in_specs=in_specs, out_specs=out_specs, out_shape=out_shape,
        compiler_params=pltpu.CompilerParams(dimension_semantics=("arbitrary",), vmem_limit_bytes=VMEM_LIMIT_BYTES),
        name="nsa_proj",
    )(x2, pos2, ln_g, ln_b, *w_parts, gln_g, gln_b, ws, bst, *rope_consts)


def _compress_kernel(x_ref, pos_ref, pe_ref, w1_ref, b1_ref, w2_ref, b2_ref, inv_ref, m1_ref, m2_ref,
                     o_ref, perm_ref):
    rows = x_ref.shape[0]
    is_key = (pl.program_id(1) == 0).astype(_F32)
    x = x_ref[...]
    half = CMP_STRIDE * HEAD_DIM
    ya = _dot((x + pe_ref[0:1, :]).astype(_BF16), w1_ref[0:half, :])
    yb = _dot((x + pe_ref[1:2, :]).astype(_BF16), w1_ref[half:2 * half, :])
    hid = jax.nn.gelu(ya + pltpu.roll(yb, rows - 1, axis=0) + b1_ref[...])
    out = _dot(hid.astype(_BF16), w2_ref[...]) + b2_ref[...]
    tables = _rope_tables(pos_ref[...].astype(_F32) * is_key, inv_ref[...], m1_ref[...], m2_ref[...])
    perm_ref[...] = _rope_block(out, tables)
    groups = rows // 4
    for r in range(4):
        o_ref[r * groups:(r + 1) * groups, :] = perm_ref[pl.ds(r, groups, stride=4), :][:, :HEAD_DIM].astype(_BF16)


def _compress(kvc, pos_end, pe, w1, b1, w2, b2, rope_consts, batch):
    rows = kvc.shape[2]
    flat = CMP_STRIDE * HEAD_DIM
    in_specs = [
        pl.BlockSpec((None, None, rows, flat), lambda b, kv, hh: (kv * KV_HEADS + hh, b, 0, 0)),
        pl.BlockSpec((None, rows, 1), lambda b, kv, hh: (b, 0, 0)),
        pl.BlockSpec((None, 2, flat), lambda b, kv, hh: (kv, 0, 0)),
        pl.BlockSpec((None, 2 * flat, CMP_HIDDEN), lambda b, kv, hh: (kv, 0, 0)),
        pl.BlockSpec((None, 1, CMP_HIDDEN), lambda b, kv, hh: (kv, 0, 0)),
        pl.BlockSpec((None, CMP_HIDDEN, LANES), lambda b, kv, hh: (kv, 0, 0)),
        pl.BlockSpec((None, 1, LANES), lambda b, kv, hh: (kv, 0, 0)),
    ] + [pl.BlockSpec(c.shape, lambda b, kv, hh: (0, 0)) for c in rope_consts]
    return pl.pallas_call(
        _compress_kernel, grid=(batch, 2, KV_HEADS), in_specs=in_specs,
        out_specs=pl.BlockSpec((None, None, None, rows, HEAD_DIM), lambda b, kv, hh: (kv, b, hh, 0, 0)),
        out_shape=jax.ShapeDtypeStruct((2, batch, KV_HEADS, rows, HEAD_DIM), _BF16),
        scratch_shapes=[pltpu.VMEM((rows, LANES), _F32)],
        compiler_params=pltpu.CompilerParams(dimension_semantics=("arbitrary",) * 3, vmem_limit_bytes=VMEM_LIMIT_BYTES),
        name="nsa_compress",
    )(kvc, pos_end, pe, w1, b1, w2, b2, *rope_consts)


def _nsa_kernel(q_ref, gn_ref, kc_ref, vc_ref, ks_ref, vs_ref, kw_ref, vw_ref, o_ref, sa_ref, sb_ref, qx_ref, *, n_sel):
    rows = kc_ref.shape[0]
    n_slc = rows // 4
    seq = ks_ref.shape[0]
    gq = GROUP * Q_BLOCK
    q0 = pl.program_id(2) * Q_BLOCK
    t = q0 + lax.broadcasted_iota(jnp.int32, (Q_BLOCK, 1), 0)

    q = q_ref[...]
    q_heads = [q[:, g * HEAD_DIM:(g + 1) * HEAD_DIM] for g in range(GROUP)]
    qs = jnp.concatenate(q_heads, axis=0)

    slot = lax.broadcasted_iota(jnp.int32, (1, rows), 1)
    shift = n_slc.bit_length() - 1
    cmp_idx = 4 * (slot & (n_slc - 1)) + lax.shift_right_logical(slot, shift)
    valid_c = (CMP_STRIDE * cmp_idx + CMP_BLOCK - 1 <= t) & (cmp_idx < rows - 1)
    s_c = _dot_nt(qs, kc_ref[...]).reshape(GROUP, Q_BLOCK, rows)
    s_c = jnp.where(valid_c[None], s_c, MASK_VALUE)
    e_c = jnp.where(valid_c[None], jnp.exp(s_c - jnp.max(s_c, axis=-1, keepdims=True)), 0.0)
    l_c = jnp.sum(e_c, axis=-1, keepdims=True)
    p_c = jnp.where(l_c > 0.0, e_c / l_c, 0.0)
    o_c = _dot(p_c.reshape(gq, rows).astype(_BF16), vc_ref[...])

    p_sum = p_c[0]
    for g in range(1, GROUP):
        p_sum = p_sum + p_c[g]
    parts = [p_sum[:, r * n_slc:(r + 1) * n_slc] for r in range(4)]
    blk = lax.broadcasted_iota(jnp.int32, (1, n_slc), 1)
    prev3 = jnp.where(blk == 0, 0.0, pltpu.roll(parts[3], 1, axis=1))
    p_slc = parts[0] + parts[1] + parts[2] + parts[3] + prev3
    cur = lax.shift_right_logical(t, SLC_BLOCK.bit_length() - 1)
    forced = (blk == 0) | (blk == cur) | (blk == cur - 1)
    score = jnp.where(forced, jnp.inf, jnp.where(blk <= cur, p_slc, -1.0))

    st = jnp.where(forced, -jnp.inf, score).T
    bi = lax.broadcasted_iota(jnp.int32, (n_slc, Q_BLOCK), 0)
    sel_t = forced.astype(_F32).T
    for _ in range(n_sel - 3):
        mx = jnp.max(st, axis=0, keepdims=True)
        first = jnp.min(jnp.where(st == mx, bi, n_slc), axis=0, keepdims=True)
        pick = bi == first
        sel_t = jnp.where(pick, 1.0, sel_t)
        st = jnp.where(pick, -jnp.inf, st)
    past = bi < lax.shift_right_logical(q0, SLC_BLOCK.bit_length() - 1)
    pen = jnp.where((sel_t > 0.5) & past, 0.0, MASK_VALUE).T
    if n_slc < LANES:
        pen = jnp.concatenate([pen, jnp.zeros((Q_BLOCK, LANES - n_slc), _F32)], axis=1)
    pen = pen.astype(_BF16)
    gap = jnp.zeros((Q_BLOCK, LANES - HEAD_DIM), _BF16)
    for g in range(GROUP):
        qx_ref[g * Q_BLOCK:(g + 1) * Q_BLOCK, :] = jnp.concatenate([q_heads[g], gap, pen], axis=1)

    w0 = pl.multiple_of(jnp.maximum(q0 - WINDOW, 0), Q_BLOCK)
    kpos_w = w0 + lax.broadcasted_iota(jnp.int32, (1, WIN_SPAN), 1)
    diff = t - kpos_w
    bias_w = jnp.where((diff >= 0) & (diff < WINDOW), 0.0, MASK_VALUE)
    s_w = _dot_nt(qs, kw_ref[pl.ds(w0, WIN_SPAN), :]).reshape(GROUP, Q_BLOCK, WIN_SPAN) + bias_w[None]
    s_w = s_w.reshape(gq, WIN_SPAN)
    p_w = jnp.exp(s_w - jnp.max(s_w, axis=-1, keepdims=True))
    acc_w = _dot(p_w.astype(_BF16), vw_ref[pl.ds(w0, WIN_SPAN), :])
    o_w = acc_w[:, :HEAD_DIM] / acc_w[:, HEAD_DIM:HEAD_DIM + 1]

    n_seq_chunks = seq // SEL_CHUNK
    causal = jnp.where(lax.broadcasted_iota(jnp.int32, (Q_BLOCK, Q_BLOCK), 1)
                       <= lax.broadcasted_iota(jnp.int32, (Q_BLOCK, Q_BLOCK), 0), 0.0, MASK_VALUE)
    s_d = _dot_nt(qs, ks_ref[pl.ds(q0, Q_BLOCK), :HEAD_DIM]).reshape(GROUP, Q_BLOCK, Q_BLOCK) + causal[None]
    s_d = s_d.reshape(gq, Q_BLOCK)
    m_d = jnp.max(s_d, axis=-1, keepdims=True)
    acc_d = _dot(jnp.exp(s_d - m_d).astype(_BF16), vs_ref[pl.ds(q0, Q_BLOCK), :])

    def sel_scores(c, buf):
        k0 = pl.multiple_of(jnp.minimum(c, n_seq_chunks - 1) * SEL_CHUNK, SEL_CHUNK)
        buf[...] = _dot_nt(qx_ref[...], ks_ref[pl.ds(k0, SEL_CHUNK), :])

    def sel_consume(c, buf, m, acc):
        k0 = pl.multiple_of(jnp.minimum(c, n_seq_chunks - 1) * SEL_CHUNK, SEL_CHUNK)
        s = buf[...]
        m_new = jnp.maximum(m, jnp.max(s, axis=-1, keepdims=True))
        p = jnp.exp(s - m_new)
        acc = jnp.exp(m - m_new) * acc + _dot(p.astype(_BF16), vs_ref[pl.ds(k0, SEL_CHUNK), :])
        return m_new, acc

    def sel_pair(i, carry):
        m, acc = carry
        sel_scores(2 * i + 1, sb_ref)
        m, acc = sel_consume(2 * i, sa_ref, m, acc)
        sel_scores(2 * i + 2, sa_ref)
        return sel_consume(2 * i + 1, sb_ref, m, acc)

    n_pairs = (q0 + 2 * SEL_CHUNK - 1) // (2 * SEL_CHUNK)
    sel_scores(0, sa_ref)
    _, acc_s = lax.fori_loop(0, n_pairs, sel_pair, (m_d, acc_d))
    o_s = acc_s[:, :HEAD_DIM] / acc_s[:, HEAD_DIM:HEAD_DIM + 1]

    gates = gn_ref[...]
    outs = []
    for g in range(GROUP):
        rs = slice(g * Q_BLOCK, (g + 1) * Q_BLOCK)
        outs.append(gates[:, 3 * g:3 * g + 1] * o_c[rs] + gates[:, 3 * g + 1:3 * g + 2] * o_s[rs]
                    + gates[:, 3 * g + 2:3 * g + 3] * o_w[rs])
    o_ref[...] = jnp.concatenate(outs, axis=1)


def _nsa(q, gn, kvc, ks, vs, kw, vw, batch, seq):
    n = q.shape[0]
    n_qb = seq // Q_BLOCK
    rows = kvc.shape[3]
    n_slc = seq // SLC_BLOCK
    qrow = lambda w: pl.BlockSpec((Q_BLOCK, w), lambda b, hh, i: (b * n_qb + i, hh))
    seq_rows = lambda w: pl.BlockSpec((None, seq, w), lambda b, hh, i: (hh, b, 0))
    in_specs = [
        qrow(GROUP * HEAD_DIM), qrow(LANES),
        pl.BlockSpec((None, None, None, rows, HEAD_DIM), lambda b, hh, i: (0, b, hh, 0, 0)),
        pl.BlockSpec((None, None, None, rows, HEAD_DIM), lambda b, hh, i: (1, b, hh, 0, 0)),
        seq_rows(2 * LANES), seq_rows(LANES), seq_rows(HEAD_DIM), seq_rows(LANES),
    ]
    return pl.pallas_call(
        functools.partial(_nsa_kernel, n_sel=min(N_SELECT, n_slc)),
        grid=(batch, KV_HEADS, n_qb), in_specs=in_specs, out_specs=qrow(GROUP * HEAD_DIM),
        out_shape=jax.ShapeDtypeStruct((n, D_MODEL), _F32),
        scratch_shapes=[pltpu.VMEM((GROUP * Q_BLOCK, SEL_CHUNK), _F32)] * 2
        + [pltpu.VMEM((GROUP * Q_BLOCK, 2 * LANES), _BF16)],
        compiler_params=pltpu.CompilerParams(dimension_semantics=("arbitrary",) * 3, vmem_limit_bytes=VMEM_LIMIT_BYTES),
        name="nsa_attention",
    )(q, gn, kvc, kvc, ks, vs, kw, vw)


def _memkv_kernel(mem_ref, w_ref, o_ref):
    o_ref[...] = _dot(mem_ref[...].astype(_BF16), w_ref[...]).astype(_BF16)


def _memkv(mem2, w_xkv):
    m = mem2.shape[0]
    return pl.pallas_call(
        _memkv_kernel, grid=(1,),
        in_specs=[pl.BlockSpec(mem2.shape, lambda i: (0, 0)), pl.BlockSpec(w_xkv.shape, lambda i: (0, 0))],
        out_specs=pl.BlockSpec((m, 2 * D_MODEL), lambda i: (0, 0)),
        out_shape=jax.ShapeDtypeStruct((m, 2 * D_MODEL), _BF16),
        compiler_params=pltpu.CompilerParams(vmem_limit_bytes=VMEM_LIMIT_BYTES),
        name="mem_kv",
    )(mem2, w_xkv)


def _trunk_kernel(a_ref, gb_ref, yb_ref, h_ref, kvm_ref, wo_ref, wxq_ref, wxo_ref, wf1_ref, wf2_ref,
                  g1_ref, b1_ref, g2_ref, b2_ref, g3_ref, b3_ref, o_ref):
    mix = a_ref[...] + gb_ref[...] * yb_ref[...]
    h1 = _layer_norm(ALPHA * h_ref[...] + _dot(mix.astype(_BF16), wo_ref[...]), g1_ref[...], b1_ref[...])

    qx = _dot(h1.astype(_BF16), wxq_ref[...]).astype(_BF16)
    heads = []
    for hh in range(XATTN_HEADS):
        cs = slice(hh * XATTN_HEAD_DIM, (hh + 1) * XATTN_HEAD_DIM)
        vcs = slice(D_MODEL + hh * XATTN_HEAD_DIM, D_MODEL + (hh + 1) * XATTN_HEAD_DIM)
        s = _dot_nt(qx[:, cs], kvm_ref[:, cs]) * (XATTN_HEAD_DIM ** -0.5)
        e = jnp.exp(s - jnp.max(s, axis=-1, keepdims=True))
        p = e / jnp.sum(e, axis=-1, keepdims=True)
        heads.append(_dot(p.astype(_BF16), kvm_ref[:, vcs]))
    xo = jnp.concatenate(heads, axis=1)
    h2 = _layer_norm(ALPHA * h1 + _dot(xo.astype(_BF16), wxo_ref[...]), g2_ref[...], b2_ref[...])

    h2b = h2.astype(_BF16)
    ff = jnp.zeros_like(h2)
    for c in range(D_FF // D_MODEL):
        cs = slice(c * D_MODEL, (c + 1) * D_MODEL)
        act = jnp.square(jnp.maximum(_dot(h2b, wf1_ref[:, cs]), 0.0))
        ff = ff + _dot(act.astype(_BF16), wf2_ref[cs, :])
    o_ref[...] = _layer_norm(ALPHA * h2 + ff, g3_ref[...], b3_ref[...])


def _trunk(a, gb, yb, h, kvm, weights, lns, seq, mem_len):
    n = a.shape[0]
    tm = ROW_TILE
    tiles_per_batch = seq // tm
    row = pl.BlockSpec((tm, D_MODEL), lambda i: (i, 0))
    in_specs = ([row, row, row, row, pl.BlockSpec((mem_len, 2 * D_MODEL), lambda i: (i // tiles_per_batch, 0))]
                + [_const_spec(w.shape) for w in weights] + [_const_spec(p.shape) for p in lns])
    return pl.pallas_call(
        _trunk_kernel, grid=(n // tm,), in_specs=in_specs, out_specs=row,
        out_shape=jax.ShapeDtypeStruct((n, D_MODEL), _F32),
        compiler_params=pltpu.CompilerParams(dimension_semantics=("arbitrary",), vmem_limit_bytes=VMEM_LIMIT_BYTES),
        name="trunk",
    )(a, gb, yb, h, kvm, *weights, *lns)


def _rope_constants():
    half = ROT_DIM // 2
    inv = ROPE_THETA ** (-jnp.arange(half, dtype=_F32) / half)
    d = jnp.arange(LANES) % HEAD_DIM
    inv_lane = jnp.where(d < ROT_DIM, inv[d % half], 0.0).astype(_F32)[None, :]
    neg_first = jnp.where(d < half, -1.0, 0.0).astype(_F32)[None, :]
    pos_second = jnp.where((d >= half) & (d < ROT_DIM), 1.0, 0.0).astype(_F32)[None, :]
    return inv_lane, neg_first, pos_second


def kernel(x, mem, positions, ln_in_g, ln_in_b, w_in, gmlp_ln_g, gmlp_ln_b, gmlp_ws, gmlp_bs, cmp_k_pe, cmp_k_w1, cmp_k_b1, cmp_k_w2, cmp_k_b2, cmp_v_pe, cmp_v_w1, cmp_v_b1, cmp_v_w2, cmp_v_b2, w_out, ln1_g, ln1_b, w_xq, w_xkv, w_xo, ln2_g, ln2_b, w_ff1, w_ff2, ln3_g, ln3_b):
    batch, seq, _ = x.shape
    mem_len = mem.shape[1]
    n = batch * seq
    n_slc = seq // SLC_BLOCK
    assert w_in.shape[0] == 1, "one layer"
    assert seq % SEL_CHUNK == 0 and seq >= WIN_SPAN and n_slc & (n_slc - 1) == 0

    rope_consts = _rope_constants()
    vec = lambda p: p.reshape(1, -1)

    wi = w_in[0]
    o_u, o_v, o_q, o_kv, o_gn, o_ga, o_gb = (0, 1024, 2048, 3072, 3072 + 6 * KV_DIM, 3120 + 6 * KV_DIM, 4144 + 6 * KV_DIM)
    w_gn = wi[:, o_gn:o_ga].reshape(D_MODEL, KV_HEADS, 3 * GROUP)
    w_gn = jnp.pad(w_gn, ((0, 0), (0, 0), (0, LANES - 3 * GROUP))).reshape(D_MODEL, KV_HEADS * LANES)
    w_parts = [wi[:, o_u:o_v], wi[:, o_v:o_q], wi[:, o_ga:o_gb], wi[:, o_q:o_kv], wi[:, o_gb:], wi[:, o_kv:o_gn], w_gn]
    w_parts = [w.astype(_BF16) for w in w_parts]

    h, a, gb, q, kvc, ks, vs, kw, vw, gn = _project(
        x.reshape(n, D_MODEL), positions.reshape(n, 1), vec(ln_in_g), vec(ln_in_b), w_parts,
        vec(gmlp_ln_g[0]), vec(gmlp_ln_b[0]), gmlp_ws[0], gmlp_bs[0].T, rope_consts, seq)

    rows = seq // CMP_STRIDE
    flat = CMP_STRIDE * HEAD_DIM
    pos_end = jnp.pad(positions[:, CMP_BLOCK - 1::CMP_STRIDE], ((0, 0), (0, 1)))[:, :, None]
    pad_lanes = lambda w: jnp.pad(w, ((0, 0), (0, LANES - HEAD_DIM)))
    kvcmp = _compress(
        kvc.reshape(2 * KV_HEADS, batch, rows, flat), pos_end,
        jnp.stack([cmp_k_pe[0].reshape(2, flat), cmp_v_pe[0].reshape(2, flat)]),
        jnp.stack([cmp_k_w1[0], cmp_v_w1[0]]).astype(_BF16),
        jnp.stack([vec(cmp_k_b1[0]), vec(cmp_v_b1[0])]),
        jnp.stack([pad_lanes(cmp_k_w2[0]), pad_lanes(cmp_v_w2[0])]).astype(_BF16),
        jnp.stack([pad_lanes(vec(cmp_k_b2[0])), pad_lanes(vec(cmp_v_b2[0]))]),
        rope_consts, batch)

    yb = _nsa(q, gn, kvcmp, ks, vs, kw, vw, batch, seq)

    kvm = _memkv(mem.reshape(batch * mem_len, D_MODEL), w_xkv[0].astype(_BF16))
    weights = [w.astype(_BF16) for w in (w_out[0], w_xq[0], w_xo[0], w_ff1[0], w_ff2[0])]
    lns = [vec(p[0]) for p in (ln1_g, ln1_b, ln2_g, ln2_b, ln3_g, ln3_b)]
    out = _trunk(a, gb, yb, h, kvm, weights, lns, seq, mem_len)
    return out.reshape(batch, seq, D_MODEL)
```

```python
import functools
import math

import jax
import jax.numpy as jnp
from jax import lax
from jax.experimental import pallas as pl
from jax.experimental.pallas import tpu as pltpu

D_MODEL = 1024
LN_EPS = 1e-5
ALPHA = 2.0 ** 0.25
ROPE_THETA = 500000.0
GMLP_GROUPS = 8
GMLP_CHUNK = 128
NSA_HEADS = 16
HEAD_DIM = 64
KV_HEADS = 4
GROUP = NSA_HEADS // KV_HEADS
KV_DIM = KV_HEADS * HEAD_DIM
ROT_DIM = HEAD_DIM // 4
CMP_BLOCK = 32
CMP_STRIDE = 16
CMP_HIDDEN = 4 * HEAD_DIM
SLC_BLOCK = 64
N_SELECT = 16
WINDOW = 512
Q_BLOCK = 128
XATTN_HEADS = 4
XATTN_HEAD_DIM = D_MODEL // XATTN_HEADS
D_FF = 4 * D_MODEL

LANES = 128
ROW_TILE = 256
SEL_CHUNK = 512
WIN_SPAN = WINDOW + Q_BLOCK
MASK_VALUE = -1e30
LOG2_E = math.log2(math.e)
VMEM_LIMIT_BYTES = 56 * 1024 * 1024

_F32 = jnp.float32
_BF16 = jnp.bfloat16


def _layer_norm(x, g, b):
    mu = jnp.mean(x, axis=-1, keepdims=True)
    xc = x - mu
    var = jnp.mean(xc * xc, axis=-1, keepdims=True)
    return xc * lax.rsqrt(var + LN_EPS) * g + b


def _dot(a, b):
    return jnp.dot(a, b, preferred_element_type=_F32)


def _dot_nt(a, b):
    return lax.dot_general(a, b, (((1,), (1,)), ((), ())), preferred_element_type=_F32)


def _rope_tables(pos_f32, inv_lane, neg_first, pos_second):
    ang = pos_f32 * inv_lane
    c = jnp.cos(ang)
    s = jnp.sin(ang)
    return c, s * neg_first, s * pos_second


def _rope_block(x, tables):
    c, s_first, s_second = tables
    half = ROT_DIM // 2
    up = pltpu.roll(x, LANES - half, axis=1)
    down = pltpu.roll(x, half, axis=1)
    return x * c + up * s_first + down * s_second


def _const_spec(shape):
    nd = len(shape)
    return pl.BlockSpec(shape, lambda *_: (0,) * nd, pipeline_mode=pl.Buffered(1))


def _proj_kernel(x_ref, pos_ref, lng_ref, lnb_ref, wu_ref, wv_ref, wga_ref, wq_ref, wgb_ref, wkv_ref, wgn_ref,
                 glng_ref, glnb_ref, ws_ref, bst_ref, inv_ref, m1_ref, m2_ref,
                 h_ref, a_ref, gb_ref, qt_ref, kvc_ref, ks_ref, vst_ref, kw_ref, vwt_ref, gnt_ref, cmp_ref, *, seq):
    tm = x_ref.shape[0]
    h = _layer_norm(x_ref[...], lng_ref[...], lnb_ref[...])
    h_ref[...] = h
    hb = h.astype(_BF16)

    u_act = jax.nn.gelu(_dot(hb, wu_ref[...]))
    vn = _layer_norm(jax.nn.gelu(_dot(hb, wv_ref[...])), glng_ref[...], glnb_ref[...]).astype(_BF16)
    gate_a = jax.nn.sigmoid(_dot(hb, wga_ref[...]))
    ua = u_act * gate_a
    ti = lax.broadcasted_iota(jnp.int32, (GMLP_CHUNK, GMLP_CHUNK), 0)
    si = lax.broadcasted_iota(jnp.int32, (GMLP_CHUNK, GMLP_CHUNK), 1)
    lower = si <= ti
    bst = bst_ref[...]
    for g in range(GMLP_GROUPS):
        w = jnp.where(lower, ws_ref[g], 0.0).astype(_BF16)
        bias = bst[:, g:g + 1]
        cs = slice(g * LANES, (g + 1) * LANES)
        for c in range(tm // GMLP_CHUNK):
            rs = slice(c * GMLP_CHUNK, (c + 1) * GMLP_CHUNK)
            mixed = _dot(w, vn[rs, cs]) + bias
            a_ref[rs, cs] = ua[rs, cs] * mixed

    gb_ref[...] = jax.nn.sigmoid(_dot(hb, wgb_ref[...]))
    gn = jax.nn.sigmoid(_dot(hb, wgn_ref[...]))
    for hh in range(KV_HEADS):
        gnt_ref[hh * LANES:(hh + 1) * LANES, :] = gn[:, hh * LANES:(hh + 1) * LANES].T

    tables = _rope_tables(pos_ref[...].astype(_F32), inv_ref[...], m1_ref[...], m2_ref[...])
    q_scale = HEAD_DIM ** -0.5 * LOG2_E
    zq = _dot(hb, wq_ref[...])
    for i in range(D_MODEL // LANES):
        cs = slice(i * LANES, (i + 1) * LANES)
        qt_ref[cs, :] = (_rope_block(zq[:, cs], tables) * q_scale).T.astype(_BF16)

    zkv = _dot(hb, wkv_ref[...])
    heads_per_slab = LANES // HEAD_DIM
    for sl in range(2 * KV_DIM // LANES):
        cmp_ref[sl] = zkv[:, sl * LANES:(sl + 1) * LANES]
    for sl in range(2 * KV_DIM // LANES):
        for l in range(CMP_STRIDE):
            pair = cmp_ref[sl, pl.ds(l, tm // CMP_STRIDE, stride=CMP_STRIDE), :]
            for hh in range(heads_per_slab):
                kvc_ref[sl * heads_per_slab + hh, :, l * HEAD_DIM:(l + 1) * HEAD_DIM] = (
                    pair[:, hh * HEAD_DIM:(hh + 1) * HEAD_DIM])

    seq_pos = (pl.program_id(0) * tm + lax.broadcasted_iota(jnp.int32, (tm, LANES), 0)) & (seq - 1)
    blk_lane = lax.broadcasted_iota(jnp.int32, (tm, LANES), 1)
    blk_onehot = jnp.where(lax.shift_right_logical(seq_pos, SLC_BLOCK.bit_length() - 1) == blk_lane, 1.0, 0.0)
    k_tail = jnp.concatenate([jnp.zeros((tm, HEAD_DIM), _BF16), blk_onehot.astype(_BF16)], axis=1)
    ones_col = (lax.broadcasted_iota(jnp.int32, (tm, HEAD_DIM), 1) == 0).astype(_F32)
    for base, k_out, vt_out in ((2 * KV_DIM, ks_ref, vst_ref), (4 * KV_DIM, kw_ref, vwt_ref)):
        for i in range(KV_DIM // LANES):
            kr = _rope_block(zkv[:, base + i * LANES: base + (i + 1) * LANES], tables).astype(_BF16)
            for hh in range(LANES // HEAD_DIM):
                k_head = kr[:, hh * HEAD_DIM:(hh + 1) * HEAD_DIM]
                if k_out is ks_ref:
                    k_out[2 * i + hh] = jnp.concatenate([k_head, k_tail], axis=1)
                else:
                    k_out[2 * i + hh] = k_head
        for hh in range(KV_HEADS):
            v = zkv[:, base + KV_DIM + hh * HEAD_DIM: base + KV_DIM + (hh + 1) * HEAD_DIM]
            v_ext = jnp.concatenate([v, ones_col], axis=1)
            for c in range(tm // Q_BLOCK):
                vt_out[hh, c] = v_ext[c * Q_BLOCK:(c + 1) * Q_BLOCK, :].T.astype(_BF16)


def _project(x2, pos2, ln_g, ln_b, w_parts, gln_g, gln_b, ws, bst, rope_consts, seq):
    n = x2.shape[0]
    tm = ROW_TILE
    row = lambda shape: pl.BlockSpec(shape, lambda i: (i, 0))
    col = lambda shape: pl.BlockSpec(shape, lambda i: (0, i))
    head_rows = lambda nh, r, w: pl.BlockSpec((nh, r, w), lambda i: (0, i, 0))
    vt_spec = pl.BlockSpec((KV_HEADS, tm // Q_BLOCK, LANES, Q_BLOCK), lambda i: (0, i, 0, 0))
    w_specs = [_const_spec(w.shape) for w in w_parts]
    in_specs = ([row((tm, D_MODEL)), row((tm, 1)), _const_spec(ln_g.shape), _const_spec(ln_b.shape)] + w_specs
                + [_const_spec(gln_g.shape), _const_spec(gln_b.shape), _const_spec(ws.shape), _const_spec(bst.shape)]
                + [_const_spec(c.shape) for c in rope_consts])
    out_shape = (
        jax.ShapeDtypeStruct((n, D_MODEL), _F32),
        jax.ShapeDtypeStruct((n, D_MODEL), _F32),
        jax.ShapeDtypeStruct((n, D_MODEL), _F32),
        jax.ShapeDtypeStruct((D_MODEL, n), _BF16),
        jax.ShapeDtypeStruct((2 * KV_HEADS, n // CMP_STRIDE, CMP_STRIDE * HEAD_DIM), _F32),
        jax.ShapeDtypeStruct((KV_HEADS, n, 2 * LANES), _BF16),
        jax.ShapeDtypeStruct((KV_HEADS, n // Q_BLOCK, LANES, Q_BLOCK), _BF16),
        jax.ShapeDtypeStruct((KV_HEADS, n, HEAD_DIM), _BF16),
        jax.ShapeDtypeStruct((KV_HEADS, n // Q_BLOCK, LANES, Q_BLOCK), _BF16),
        jax.ShapeDtypeStruct((KV_HEADS * LANES, n), _F32),
    )
    out_specs = (row((tm, D_MODEL)), row((tm, D_MODEL)), row((tm, D_MODEL)), col((D_MODEL, tm)),
                 head_rows(2 * KV_HEADS, tm // CMP_STRIDE, CMP_STRIDE * HEAD_DIM),
                 head_rows(KV_HEADS, tm, 2 * LANES), vt_spec, head_rows(KV_HEADS, tm, HEAD_DIM), vt_spec,
                 col((KV_HEADS * LANES, tm)))
    return pl.pallas_call(
        functools.partial(_proj_kernel, seq=seq), grid=(n // tm,), in_specs=in_specs,
        out_specs=out_specs, out_shape=out_shape,
        scratch_shapes=[pltpu.VMEM((2 * KV_DIM // LANES, tm, LANES), _F32)],
        compiler_params=pltpu.CompilerParams(dimension_semantics=("arbitrary",), vmem_limit_bytes=VMEM_LIMIT_BYTES),
        name="nsa_proj",
    )(x2, pos2, ln_g, ln_b, *w_parts, gln_g, gln_b, ws, bst, *rope_consts)


def _compress_kernel(x_ref, pos_ref, pe_ref, w1_ref, b1_ref, w2_ref, b2_ref, inv_ref, m1_ref, m2_ref,
                     o_ref, ot_ref, perm_ref):
    rows = x_ref.shape[0]
    is_key = (pl.program_id(1) == 0).astype(_F32)
    x = x_ref[...]
    half = CMP_STRIDE * HEAD_DIM
    ya = _dot((x + pe_ref[0:1, :]).astype(_BF16), w1_ref[0:half, :])
    yb = _dot((x + pe_ref[1:2, :]).astype(_BF16), w1_ref[half:2 * half, :])
    hid = jax.nn.gelu(ya + pltpu.roll(yb, rows - 1, axis=0) + b1_ref[...])
    out = _dot(hid.astype(_BF16), w2_ref[...]) + b2_ref[...]
    tables = _rope_tables(pos_ref[...].astype(_F32) * is_key, inv_ref[...], m1_ref[...], m2_ref[...])
    perm_ref[...] = _rope_block(out, tables)
    groups = rows // 4
    for r in range(4):
        part = perm_ref[pl.ds(r, groups, stride=4), :]
        o_ref[r * groups:(r + 1) * groups, :] = part[:, :HEAD_DIM].astype(_BF16)
        ot_ref[:, r * groups:(r + 1) * groups] = part.T[:HEAD_DIM, :].astype(_BF16)


def _compress(kvc, pos_end, pe, w1, b1, w2, b2, rope_consts, batch):
    rows = kvc.shape[2]
    flat = CMP_STRIDE * HEAD_DIM
    in_specs = [
        pl.BlockSpec((None, None, rows, flat), lambda b, kv, hh: (kv * KV_HEADS + hh, b, 0, 0)),
        pl.BlockSpec((None, rows, 1), lambda b, kv, hh: (b, 0, 0)),
        pl.BlockSpec((None, 2, flat), lambda b, kv, hh: (kv, 0, 0)),
        pl.BlockSpec((None, 2 * flat, CMP_HIDDEN), lambda b, kv, hh: (kv, 0, 0)),
        pl.BlockSpec((None, 1, CMP_HIDDEN), lambda b, kv, hh: (kv, 0, 0)),
        pl.BlockSpec((None, CMP_HIDDEN, LANES), lambda b, kv, hh: (kv, 0, 0)),
        pl.BlockSpec((None, 1, LANES), lambda b, kv, hh: (kv, 0, 0)),
    ] + [pl.BlockSpec(c.shape, lambda b, kv, hh: (0, 0)) for c in rope_consts]
    return pl.pallas_call(
        _compress_kernel, grid=(batch, 2, KV_HEADS), in_specs=in_specs,
        out_specs=(pl.BlockSpec((None, None, None, rows, HEAD_DIM), lambda b, kv, hh: (kv, b, hh, 0, 0)),
                   pl.BlockSpec((None, None, None, HEAD_DIM, rows), lambda b, kv, hh: (kv, b, hh, 0, 0))),
        out_shape=(jax.ShapeDtypeStruct((2, batch, KV_HEADS, rows, HEAD_DIM), _BF16),
                   jax.ShapeDtypeStruct((2, batch, KV_HEADS, HEAD_DIM, rows), _BF16)),
        scratch_shapes=[pltpu.VMEM((rows, LANES), _F32)],
        compiler_params=pltpu.CompilerParams(dimension_semantics=("arbitrary",) * 3, vmem_limit_bytes=VMEM_LIMIT_BYTES),
        name="nsa_compress",
    )(kvc, pos_end, pe, w1, b1, w2, b2, *rope_consts)


def _per_head(fn, s, *shared):
    return jnp.concatenate([fn(s[:, g * Q_BLOCK:(g + 1) * Q_BLOCK], *shared) for g in range(GROUP)], axis=1)


def _nsa_kernel(qt_ref, gnt_ref, kc_ref, vct_ref, ks_ref, vst_ref, kw_ref, vwt_ref, o_ref, sa_ref, sb_ref, qx_ref,
                *, n_sel):
    rows = kc_ref.shape[0]
    n_slc = rows // 4
    seq = ks_ref.shape[0]
    qb = pl.program_id(2)
    q0 = qb * Q_BLOCK
    t = q0 + lax.broadcasted_iota(jnp.int32, (1, Q_BLOCK), 1)

    qt = qt_ref[...]
    q_t = jnp.concatenate([qt[g * HEAD_DIM:(g + 1) * HEAD_DIM, :] for g in range(GROUP)], axis=1)

    slot = lax.broadcasted_iota(jnp.int32, (rows, 1), 0)
    shift = n_slc.bit_length() - 1
    cmp_idx = 4 * (slot & (n_slc - 1)) + lax.shift_right_logical(slot, shift)
    valid_c = (CMP_STRIDE * cmp_idx + CMP_BLOCK - 1 <= t) & (cmp_idx < rows - 1)
    s_c = _per_head(lambda s, v: jnp.where(v, s, MASK_VALUE), _dot(kc_ref[...], q_t), valid_c)
    e_c = jnp.exp2(s_c - jnp.max(s_c, axis=0, keepdims=True))
    has_c = t >= CMP_BLOCK - 1
    inv_c = _per_head(lambda l, ok: jnp.where(ok, 1.0 / l, 0.0), jnp.sum(e_c, axis=0, keepdims=True), has_c)
    p_c = e_c * inv_c
    o_c = _dot(vct_ref[...], p_c.astype(_BF16))

    p_sum = p_c[:, :Q_BLOCK]
    for g in range(1, GROUP):
        p_sum = p_sum + p_c[:, g * Q_BLOCK:(g + 1) * Q_BLOCK]
    parts = [p_sum[r * n_slc:(r + 1) * n_slc, :] for r in range(4)]
    blk = lax.broadcasted_iota(jnp.int32, (n_slc, 1), 0)
    prev3 = jnp.where(blk == 0, 0.0, pltpu.roll(parts[3], 1, axis=0))
    p_slc = parts[0] + parts[1] + parts[2] + parts[3] + prev3
    cur = lax.shift_right_logical(t, SLC_BLOCK.bit_length() - 1)
    forced = (blk == 0) | (blk == cur) | (blk == cur - 1)

    st = jnp.where(forced, -jnp.inf, jnp.where(blk <= cur, p_slc, -1.0))
    bi = lax.broadcasted_iota(jnp.int32, (n_slc, Q_BLOCK), 0)
    sel = forced
    for _ in range(n_sel - 3):
        mx = jnp.max(st, axis=0, keepdims=True)
        first = jnp.min(jnp.where(st == mx, bi, n_slc), axis=0, keepdims=True)
        pick = bi == first
        sel = sel | pick
        st = jnp.where(pick, -jnp.inf, st)
    past = bi < lax.shift_right_logical(q0, SLC_BLOCK.bit_length() - 1)
    pen = jnp.where(sel & past, 0.0, MASK_VALUE)
    if n_slc < LANES:
        pen = jnp.concatenate([pen, jnp.zeros((LANES - n_slc, Q_BLOCK), _F32)], axis=0)
    pen = pen.astype(_BF16)
    qx_ref[0:HEAD_DIM, :] = q_t
    qx_ref[HEAD_DIM:LANES, :] = jnp.zeros((LANES - HEAD_DIM, GROUP * Q_BLOCK), _BF16)
    qx_ref[LANES:2 * LANES, :] = jnp.concatenate([pen] * GROUP, axis=1)

    w0 = pl.multiple_of(jnp.maximum(q0 - WINDOW, 0), Q_BLOCK)
    kpos_w = w0 + lax.broadcasted_iota(jnp.int32, (WIN_SPAN, 1), 0)
    diff = t - kpos_w
    bias_w = jnp.where((diff >= 0) & (diff < WINDOW), 0.0, MASK_VALUE)
    s_w = _per_head(lambda s, b: s + b, _dot(kw_ref[pl.ds(w0, WIN_SPAN), :], q_t), bias_w)
    p_w = jnp.exp2(s_w - jnp.max(s_w, axis=0, keepdims=True))
    wc = jnp.maximum(qb - WINDOW // Q_BLOCK, 0)
    v_w = jnp.concatenate([vwt_ref[wc + i] for i in range(WIN_SPAN // Q_BLOCK)], axis=1)
    acc_w = _dot(v_w, p_w.astype(_BF16))
    o_w = acc_w[:HEAD_DIM] * (1.0 / acc_w[HEAD_DIM:HEAD_DIM + 1])

    n_seq_chunks = seq // SEL_CHUNK
    sub = SEL_CHUNK // Q_BLOCK
    causal = jnp.where(lax.broadcasted_iota(jnp.int32, (Q_BLOCK, Q_BLOCK), 0)
                       <= lax.broadcasted_iota(jnp.int32, (Q_BLOCK, Q_BLOCK), 1), 0.0, MASK_VALUE)
    s_d = _per_head(lambda s, b: s + b, _dot(ks_ref[pl.ds(q0, Q_BLOCK), :HEAD_DIM], q_t), causal)
    m_d = jnp.max(s_d, axis=0, keepdims=True)
    acc_d = _dot(vst_ref[qb], jnp.exp2(s_d - m_d).astype(_BF16))

    def sel_scores(c, buf):
        k0 = pl.multiple_of(jnp.minimum(c, n_seq_chunks - 1) * SEL_CHUNK, SEL_CHUNK)
        buf[...] = _dot(ks_ref[pl.ds(k0, SEL_CHUNK), :], qx_ref[...])

    def sel_consume(c, buf, m, acc):
        c0 = jnp.minimum(c, n_seq_chunks - 1) * sub
        s = buf[...]
        m_new = jnp.maximum(m, jnp.max(s, axis=0, keepdims=True))
        p = jnp.exp2(s - m_new)
        v_t = jnp.concatenate([vst_ref[c0 + i] for i in range(sub)], axis=1)
        acc = jnp.exp2(m - m_new) * acc + _dot(v_t, p.astype(_BF16))
        return m_new, acc

    def sel_pair(i, carry):
        m, acc = carry
        sel_scores(2 * i + 1, sb_ref)
        m, acc = sel_consume(2 * i, sa_ref, m, acc)
        sel_scores(2 * i + 2, sa_ref)
        return sel_consume(2 * i + 1, sb_ref, m, acc)

    n_pairs = (q0 + 2 * SEL_CHUNK - 1) // (2 * SEL_CHUNK)
    sel_scores(0, sa_ref)
    _, acc_s = lax.fori_loop(0, n_pairs, sel_pair, (m_d, acc_d))
    o_s = acc_s[:HEAD_DIM] * (1.0 / acc_s[HEAD_DIM:HEAD_DIM + 1])

    gates = gnt_ref[...]
    outs = []
    for g in range(GROUP):
        ls = slice(g * Q_BLOCK, (g + 1) * Q_BLOCK)
        outs.append(gates[3 * g:3 * g + 1] * o_c[:, ls] + gates[3 * g + 1:3 * g + 2] * o_s[:, ls]
                    + gates[3 * g + 2:3 * g + 3] * o_w[:, ls])
    o_ref[...] = jnp.concatenate(outs, axis=0).T


def _nsa(qt, gnt, kc, vct, ks, vst, kw, vwt, batch, seq):
    n = qt.shape[1]
    n_qb = seq // Q_BLOCK
    rows = kc.shape[3]
    n_slc = seq // SLC_BLOCK
    qcol = lambda r: pl.BlockSpec((r, Q_BLOCK), lambda b, hh, i: (hh, b * n_qb + i))
    seq_rows = lambda w: pl.BlockSpec((None, seq, w), lambda b, hh, i: (hh, b, 0))
    vt_spec = pl.BlockSpec((None, n_qb, LANES, Q_BLOCK), lambda b, hh, i: (hh, b, 0, 0))
    in_specs = [
        qcol(GROUP * HEAD_DIM), qcol(LANES),
        pl.BlockSpec((None, None, None, rows, HEAD_DIM), lambda b, hh, i: (0, b, hh, 0, 0)),
        pl.BlockSpec((None, None, None, HEAD_DIM, rows), lambda b, hh, i: (1, b, hh, 0, 0)),
        seq_rows(2 * LANES), vt_spec, seq_rows(HEAD_DIM), vt_spec,
    ]
    return pl.pallas_call(
        functools.partial(_nsa_kernel, n_sel=min(N_SELECT, n_slc)),
        grid=(batch, KV_HEADS, n_qb), in_specs=in_specs,
        out_specs=pl.BlockSpec((Q_BLOCK, GROUP * HEAD_DIM), lambda b, hh, i: (b * n_qb + i, hh)),
        out_shape=jax.ShapeDtypeStruct((n, D_MODEL), _F32),
        scratch_shapes=[pltpu.VMEM((SEL_CHUNK, GROUP * Q_BLOCK), _F32)] * 2
        + [pltpu.VMEM((2 * LANES, GROUP * Q_BLOCK), _BF16)],
        compiler_params=pltpu.CompilerParams(dimension_semantics=("arbitrary",) * 3, vmem_limit_bytes=VMEM_LIMIT_BYTES),
        name="nsa_attention",
    )(qt, gnt, kc, vct, ks, vst, kw, vwt)


def _memkv_kernel(mem_ref, w_ref, o_ref):
    o_ref[...] = _dot(mem_ref[...].astype(_BF16), w_ref[...]).astype(_BF16)


def _memkv(mem2, w_xkv):
    m = mem2.shape[0]
    return pl.pallas_call(
        _memkv_kernel, grid=(1,),
        in_specs=[pl.BlockSpec(mem2.shape, lambda i: (0, 0)), pl.BlockSpec(w_xkv.shape, lambda i: (0, 0))],
        out_specs=pl.BlockSpec((m, 2 * D_MODEL), lambda i: (0, 0)),
        out_shape=jax.ShapeDtypeStruct((m, 2 * D_MODEL), _BF16),
        compiler_params=pltpu.CompilerParams(vmem_limit_bytes=VMEM_LIMIT_BYTES),
        name="mem_kv",
    )(mem2, w_xkv)


def _trunk_kernel(a_ref, gb_ref, yb_ref, h_ref, kvm_ref, wo_ref, wxq_ref, wxo_ref, wf1_ref, wf2_ref,
                  g1_ref, b1_ref, g2_ref, b2_ref, g3_ref, b3_ref, o_ref):
    mix = a_ref[...] + gb_ref[...] * yb_ref[...]
    h1 = _layer_norm(ALPHA * h_ref[...] + _dot(mix.astype(_BF16), wo_ref[...]), g1_ref[...], b1_ref[...])

    qx = _dot(h1.astype(_BF16), wxq_ref[...]).astype(_BF16)
    heads = []
    for hh in range(XATTN_HEADS):
        cs = slice(hh * XATTN_HEAD_DIM, (hh + 1) * XATTN_HEAD_DIM)
        vcs = slice(D_MODEL + hh * XATTN_HEAD_DIM, D_MODEL + (hh + 1) * XATTN_HEAD_DIM)
        s = _dot_nt(qx[:, cs], kvm_ref[:, cs]) * (XATTN_HEAD_DIM ** -0.5)
        e = jnp.exp(s - jnp.max(s, axis=-1, keepdims=True))
        p = e / jnp.sum(e, axis=-1, keepdims=True)
        heads.append(_dot(p.astype(_BF16), kvm_ref[:, vcs]))
    xo = jnp.concatenate(heads, axis=1)
    h2 = _layer_norm(ALPHA * h1 + _dot(xo.astype(_BF16), wxo_ref[...]), g2_ref[...], b2_ref[...])

    h2b = h2.astype(_BF16)
    ff = jnp.zeros_like(h2)
    for c in range(D_FF // D_MODEL):
        cs = slice(c * D_MODEL, (c + 1) * D_MODEL)
        act = jnp.square(jnp.maximum(_dot(h2b, wf1_ref[:, cs]), 0.0))
        ff = ff + _dot(act.astype(_BF16), wf2_ref[cs, :])
    o_ref[...] = _layer_norm(ALPHA * h2 + ff, g3_ref[...], b3_ref[...])


def _trunk(a, gb, yb, h, kvm, weights, lns, seq, mem_len):
    n = a.shape[0]
    tm = ROW_TILE
    tiles_per_batch = seq // tm
    row = pl.BlockSpec((tm, D_MODEL), lambda i: (i, 0))
    in_specs = ([row, row, row, row, pl.BlockSpec((mem_len, 2 * D_MODEL), lambda i: (i // tiles_per_batch, 0))]
                + [_const_spec(w.shape) for w in weights] + [_const_spec(p.shape) for p in lns])
    return pl.pallas_call(
        _trunk_kernel, grid=(n // tm,), in_specs=in_specs, out_specs=row,
        out_shape=jax.ShapeDtypeStruct((n, D_MODEL), _F32),
        compiler_params=pltpu.CompilerParams(dimension_semantics=("arbitrary",), vmem_limit_bytes=VMEM_LIMIT_BYTES),
        name="trunk",
    )(a, gb, yb, h, kvm, *weights, *lns)


def _rope_constants():
    half = ROT_DIM // 2
    inv = ROPE_THETA ** (-jnp.arange(half, dtype=_F32) / half)
    d = jnp.arange(LANES) % HEAD_DIM
    inv_lane = jnp.where(d < ROT_DIM, inv[d % half], 0.0).astype(_F32)[None, :]
    neg_first = jnp.where(d < half, -1.0, 0.0).astype(_F32)[None, :]
    pos_second = jnp.where((d >= half) & (d < ROT_DIM), 1.0, 0.0).astype(_F32)[None, :]
    return inv_lane, neg_first, pos_second


def kernel(x, mem, positions, ln_in_g, ln_in_b, w_in, gmlp_ln_g, gmlp_ln_b, gmlp_ws, gmlp_bs, cmp_k_pe, cmp_k_w1, cmp_k_b1, cmp_k_w2, cmp_k_b2, cmp_v_pe, cmp_v_w1, cmp_v_b1, cmp_v_w2, cmp_v_b2, w_out, ln1_g, ln1_b, w_xq, w_xkv, w_xo, ln2_g, ln2_b, w_ff1, w_ff2, ln3_g, ln3_b):
    batch, seq, _ = x.shape
    mem_len = mem.shape[1]
    n = batch * seq
    n_slc = seq // SLC_BLOCK
    assert w_in.shape[0] == 1, "one layer"
    assert seq % (2 * SEL_CHUNK) == 0 and seq >= WIN_SPAN and seq & (seq - 1) == 0 and n_slc <= LANES

    rope_consts = _rope_constants()
    vec = lambda p: p.reshape(1, -1)

    wi = w_in[0]
    o_u, o_v, o_q, o_kv, o_gn, o_ga, o_gb = (0, 1024, 2048, 3072, 3072 + 6 * KV_DIM, 3120 + 6 * KV_DIM, 4144 + 6 * KV_DIM)
    w_gn = wi[:, o_gn:o_ga].reshape(D_MODEL, KV_HEADS, 3 * GROUP)
    w_gn = jnp.pad(w_gn, ((0, 0), (0, 0), (0, LANES - 3 * GROUP))).reshape(D_MODEL, KV_HEADS * LANES)
    w_parts = [wi[:, o_u:o_v], wi[:, o_v:o_q], wi[:, o_ga:o_gb], wi[:, o_q:o_kv], wi[:, o_gb:], wi[:, o_kv:o_gn], w_gn]
    w_parts = [w.astype(_BF16) for w in w_parts]

    h, a, gb, qt, kvc, ks, vst, kw, vwt, gnt = _project(
        x.reshape(n, D_MODEL), positions.reshape(n, 1), vec(ln_in_g), vec(ln_in_b), w_parts,
        vec(gmlp_ln_g[0]), vec(gmlp_ln_b[0]), gmlp_ws[0], gmlp_bs[0].T, rope_consts, seq)

    rows = seq // CMP_STRIDE
    flat = CMP_STRIDE * HEAD_DIM
    pos_end = jnp.pad(positions[:, CMP_BLOCK - 1::CMP_STRIDE], ((0, 0), (0, 1)))[:, :, None]
    pad_lanes = lambda w: jnp.pad(w, ((0, 0), (0, LANES - HEAD_DIM)))
    kvcmp, kvcmp_t = _compress(
        kvc.reshape(2 * KV_HEADS, batch, rows, flat), pos_end,
        jnp.stack([cmp_k_pe[0].reshape(2, flat), cmp_v_pe[0].reshape(2, flat)]),
        jnp.stack([cmp_k_w1[0], cmp_v_w1[0]]).astype(_BF16),
        jnp.stack([vec(cmp_k_b1[0]), vec(cmp_v_b1[0])]),
        jnp.stack([pad_lanes(cmp_k_w2[0]), pad_lanes(cmp_v_w2[0])]).astype(_BF16),
        jnp.stack([pad_lanes(vec(cmp_k_b2[0])), pad_lanes(vec(cmp_v_b2[0]))]),
        rope_consts, batch)

    yb = _nsa(qt, gnt, kvcmp, kvcmp_t, ks, vst, kw, vwt, batch, seq)

    kvm = _memkv(mem.reshape(batch * mem_len, D_MODEL), w_xkv[0].astype(_BF16))
    weights = [w.astype(_BF16) for w in (w_out[0], w_xq[0], w_xo[0], w_ff1[0], w_ff2[0])]
    lns = [vec(p[0]) for p in (ln1_g, ln1_b, ln2_g, ln2_b, ln3_g, ln3_b)]
    out = _trunk(a, gb, yb, h, kvm, weights, lns, seq, mem_len)
    return out.reshape(batch, seq, D_MODEL)
```

```python
import functools
import math

import jax
import jax.numpy as jnp
from jax import lax
from jax.experimental import pallas as pl
from jax.experimental.pallas import tpu as pltpu

D_MODEL = 1024
LN_EPS = 1e-5
ALPHA = 2.0 ** 0.25
ROPE_THETA = 500000.0
GMLP_GROUPS = 8
GMLP_CHUNK = 128
NSA_HEADS = 16
HEAD_DIM = 64
KV_HEADS = 4
GROUP = NSA_HEADS // KV_HEADS
KV_DIM = KV_HEADS * HEAD_DIM
ROT_DIM = HEAD_DIM // 4
CMP_BLOCK = 32
CMP_STRIDE = 16
CMP_HIDDEN = 4 * HEAD_DIM
SLC_BLOCK = 64
N_SELECT = 16
WINDOW = 512
Q_BLOCK = 128
XATTN_HEADS = 4
XATTN_HEAD_DIM = D_MODEL // XATTN_HEADS
D_FF = 4 * D_MODEL

LANES = 128
ROW_TILE = 256
TRUNK_TILE = 512
TRUNK_SPLIT = 2
SEL_CHUNK = 512
WIN_SPAN = WINDOW + Q_BLOCK
MASK_VALUE = -1e30
LOG2_E = math.log2(math.e)
VMEM_LIMIT_BYTES = 56 * 1024 * 1024

_F32 = jnp.float32
_BF16 = jnp.bfloat16


def _layer_norm(x, g, b):
    mu = jnp.mean(x, axis=-1, keepdims=True)
    xc = x - mu
    var = jnp.mean(xc * xc, axis=-1, keepdims=True)
    return xc * lax.rsqrt(var + LN_EPS) * g + b


def _dot(a, b):
    return jnp.dot(a, b, preferred_element_type=_F32)


def _dot_nt(a, b):
    return lax.dot_general(a, b, (((1,), (1,)), ((), ())), preferred_element_type=_F32)


def _rope_tables(pos_f32, inv_lane, neg_first, pos_second):
    ang = pos_f32 * inv_lane
    c = jnp.cos(ang)
    s = jnp.sin(ang)
    return c, s * neg_first, s * pos_second


def _rope_block(x, tables):
    c, s_first, s_second = tables
    half = ROT_DIM // 2
    up = pltpu.roll(x, LANES - half, axis=1)
    down = pltpu.roll(x, half, axis=1)
    return x * c + up * s_first + down * s_second


def _const_spec(shape):
    nd = len(shape)
    return pl.BlockSpec(shape, lambda *_: (0,) * nd, pipeline_mode=pl.Buffered(1))


def _proj_kernel(x_ref, pos_ref, lng_ref, lnb_ref, wu_ref, wv_ref, wga_ref, wq_ref, wgb_ref, wkv_ref, wgn_ref,
                 glng_ref, glnb_ref, ws_ref, bst_ref, inv_ref, m1_ref, m2_ref,
                 h_ref, a_ref, gb_ref, qt_ref, kvc_ref, ks_ref, vst_ref, kw_ref, vwt_ref, gnt_ref, cmp_ref, *, seq):
    tm = x_ref.shape[0]
    h = _layer_norm(x_ref[...], lng_ref[...], lnb_ref[...])
    h_ref[...] = h
    hb = h.astype(_BF16)

    u_act = jax.nn.gelu(_dot(hb, wu_ref[...]))
    vn = _layer_norm(jax.nn.gelu(_dot(hb, wv_ref[...])), glng_ref[...], glnb_ref[...]).astype(_BF16)
    gate_a = jax.nn.sigmoid(_dot(hb, wga_ref[...]))
    ua = u_act * gate_a
    ti = lax.broadcasted_iota(jnp.int32, (GMLP_CHUNK, GMLP_CHUNK), 0)
    si = lax.broadcasted_iota(jnp.int32, (GMLP_CHUNK, GMLP_CHUNK), 1)
    lower = si <= ti
    bst = bst_ref[...]
    for g in range(GMLP_GROUPS):
        w = jnp.where(lower, ws_ref[g], 0.0).astype(_BF16)
        bias = bst[:, g:g + 1]
        cs = slice(g * LANES, (g + 1) * LANES)
        for c in range(tm // GMLP_CHUNK):
            rs = slice(c * GMLP_CHUNK, (c + 1) * GMLP_CHUNK)
            mixed = _dot(w, vn[rs, cs]) + bias
            a_ref[rs, cs] = ua[rs, cs] * mixed

    gb_ref[...] = jax.nn.sigmoid(_dot(hb, wgb_ref[...]))
    gn = jax.nn.sigmoid(_dot(hb, wgn_ref[...]))
    for hh in range(KV_HEADS):
        gnt_ref[hh * LANES:(hh + 1) * LANES, :] = gn[:, hh * LANES:(hh + 1) * LANES].T

    tables = _rope_tables(pos_ref[...].astype(_F32), inv_ref[...], m1_ref[...], m2_ref[...])
    q_scale = HEAD_DIM ** -0.5 * LOG2_E
    zq = _dot(hb, wq_ref[...])
    for i in range(D_MODEL // LANES):
        cs = slice(i * LANES, (i + 1) * LANES)
        qt_ref[cs, :] = (_rope_block(zq[:, cs], tables) * q_scale).T.astype(_BF16)

    zkv = _dot(hb, wkv_ref[...])
    heads_per_slab = LANES // HEAD_DIM
    for sl in range(2 * KV_DIM // LANES):
        cmp_ref[sl] = zkv[:, sl * LANES:(sl + 1) * LANES]
    for sl in range(2 * KV_DIM // LANES):
        for l in range(CMP_STRIDE):
            pair = cmp_ref[sl, pl.ds(l, tm // CMP_STRIDE, stride=CMP_STRIDE), :]
            for hh in range(heads_per_slab):
                kvc_ref[sl * heads_per_slab + hh, :, l * HEAD_DIM:(l + 1) * HEAD_DIM] = (
                    pair[:, hh * HEAD_DIM:(hh + 1) * HEAD_DIM])

    seq_pos = (pl.program_id(0) * tm + lax.broadcasted_iota(jnp.int32, (tm, LANES), 0)) & (seq - 1)
    blk_lane = lax.broadcasted_iota(jnp.int32, (tm, LANES), 1)
    blk_onehot = jnp.where(lax.shift_right_logical(seq_pos, SLC_BLOCK.bit_length() - 1) == blk_lane, 1.0, 0.0)
    k_tail = jnp.concatenate([jnp.zeros((tm, HEAD_DIM), _BF16), blk_onehot.astype(_BF16)], axis=1)
    ones_col = (lax.broadcasted_iota(jnp.int32, (tm, HEAD_DIM), 1) == 0).astype(_F32)
    for base, k_out, vt_out in ((2 * KV_DIM, ks_ref, vst_ref), (4 * KV_DIM, kw_ref, vwt_ref)):
        for i in range(KV_DIM // LANES):
            kr = _rope_block(zkv[:, base + i * LANES: base + (i + 1) * LANES], tables).astype(_BF16)
            for hh in range(LANES // HEAD_DIM):
                k_head = kr[:, hh * HEAD_DIM:(hh + 1) * HEAD_DIM]
                if k_out is ks_ref:
                    k_out[2 * i + hh] = jnp.concatenate([k_head, k_tail], axis=1)
                else:
                    k_out[2 * i + hh] = k_head
        for hh in range(KV_HEADS):
            v = zkv[:, base + KV_DIM + hh * HEAD_DIM: base + KV_DIM + (hh + 1) * HEAD_DIM]
            v_ext = jnp.concatenate([v, ones_col], axis=1)
            for c in range(tm // Q_BLOCK):
                vt_out[hh, c] = v_ext[c * Q_BLOCK:(c + 1) * Q_BLOCK, :].T.astype(_BF16)


def _project(x2, pos2, ln_g, ln_b, w_parts, gln_g, gln_b, ws, bst, rope_consts, seq):
    n = x2.shape[0]
    tm = ROW_TILE
    row = lambda shape: pl.BlockSpec(shape, lambda i: (i, 0))
    col = lambda shape: pl.BlockSpec(shape, lambda i: (0, i))
    head_rows = lambda nh, r, w: pl.BlockSpec((nh, r, w), lambda i: (0, i, 0))
    vt_spec = pl.BlockSpec((KV_HEADS, tm // Q_BLOCK, LANES, Q_BLOCK), lambda i: (0, i, 0, 0))
    w_specs = [_const_spec(w.shape) for w in w_parts]
    in_specs = ([row((tm, D_MODEL)), row((tm, 1)), _const_spec(ln_g.shape), _const_spec(ln_b.shape)] + w_specs
                + [_const_spec(gln_g.shape), _const_spec(gln_b.shape), _const_spec(ws.shape), _const_spec(bst.shape)]
                + [_const_spec(c.shape) for c in rope_consts])
    out_shape = (
        jax.ShapeDtypeStruct((n, D_MODEL), _F32),
        jax.ShapeDtypeStruct((n, D_MODEL), _F32),
        jax.ShapeDtypeStruct((n, D_MODEL), _F32),
        jax.ShapeDtypeStruct((D_MODEL, n), _BF16),
        jax.ShapeDtypeStruct((2 * KV_HEADS, n // CMP_STRIDE, CMP_STRIDE * HEAD_DIM), _F32),
        jax.ShapeDtypeStruct((KV_HEADS, n, 2 * LANES), _BF16),
        jax.ShapeDtypeStruct((KV_HEADS, n // Q_BLOCK, LANES, Q_BLOCK), _BF16),
        jax.ShapeDtypeStruct((KV_HEADS, n, HEAD_DIM), _BF16),
        jax.ShapeDtypeStruct((KV_HEADS, n // Q_BLOCK, LANES, Q_BLOCK), _BF16),
        jax.ShapeDtypeStruct((KV_HEADS * LANES, n), _F32),
    )
    out_specs = (row((tm, D_MODEL)), row((tm, D_MODEL)), row((tm, D_MODEL)), col((D_MODEL, tm)),
                 head_rows(2 * KV_HEADS, tm // CMP_STRIDE, CMP_STRIDE * HEAD_DIM),
                 head_rows(KV_HEADS, tm, 2 * LANES), vt_spec, head_rows(KV_HEADS, tm, HEAD_DIM), vt_spec,
                 col((KV_HEADS * LANES, tm)))
    return pl.pallas_call(
        functools.partial(_proj_kernel, seq=seq), grid=(n // tm,), in_specs=in_specs,
        out_specs=out_specs, out_shape=out_shape,
        scratch_shapes=[pltpu.VMEM((2 * KV_DIM // LANES, tm, LANES), _F32)],
        compiler_params=pltpu.CompilerParams(dimension_semantics=("arbitrary",), vmem_limit_bytes=VMEM_LIMIT_BYTES),
        name="nsa_proj",
    )(x2, pos2, ln_g, ln_b, *w_parts, gln_g, gln_b, ws, bst, *rope_consts)


def _compress_kernel(x_ref, pos_ref, pe_ref, w1_ref, b1_ref, w2_ref, b2_ref, inv_ref, m1_ref, m2_ref,
                     o_ref, ot_ref, perm_ref):
    rows = x_ref.shape[0]
    is_key = (pl.program_id(1) == 0).astype(_F32)
    x = x_ref[...]
    half = CMP_STRIDE * HEAD_DIM
    ya = _dot((x + pe_ref[0:1, :]).astype(_BF16), w1_ref[0:half, :])
    yb = _dot((x + pe_ref[1:2, :]).astype(_BF16), w1_ref[half:2 * half, :])
    hid = jax.nn.gelu(ya + pltpu.roll(yb, rows - 1, axis=0) + b1_ref[...])
    out = _dot(hid.astype(_BF16), w2_ref[...]) + b2_ref[...]
    tables = _rope_tables(pos_ref[...].astype(_F32) * is_key, inv_ref[...], m1_ref[...], m2_ref[...])
    perm_ref[...] = _rope_block(out, tables)
    groups = rows // 4
    for r in range(4):
        part = perm_ref[pl.ds(r, groups, stride=4), :]
        o_ref[r * groups:(r + 1) * groups, :] = part[:, :HEAD_DIM].astype(_BF16)
        ot_ref[:, r * groups:(r + 1) * groups] = part.T[:HEAD_DIM, :].astype(_BF16)


def _compress(kvc, pos_end, pe, w1, b1, w2, b2, rope_consts, batch):
    rows = kvc.shape[2]
    flat = CMP_STRIDE * HEAD_DIM
    in_specs = [
        pl.BlockSpec((None, None, rows, flat), lambda b, kv, hh: (kv * KV_HEADS + hh, b, 0, 0)),
        pl.BlockSpec((None, rows, 1), lambda b, kv, hh: (b, 0, 0)),
        pl.BlockSpec((None, 2, flat), lambda b, kv, hh: (kv, 0, 0)),
        pl.BlockSpec((None, 2 * flat, CMP_HIDDEN), lambda b, kv, hh: (kv, 0, 0)),
        pl.BlockSpec((None, 1, CMP_HIDDEN), lambda b, kv, hh: (kv, 0, 0)),
        pl.BlockSpec((None, CMP_HIDDEN, LANES), lambda b, kv, hh: (kv, 0, 0)),
        pl.BlockSpec((None, 1, LANES), lambda b, kv, hh: (kv, 0, 0)),
    ] + [pl.BlockSpec(c.shape, lambda b, kv, hh: (0, 0)) for c in rope_consts]
    return pl.pallas_call(
        _compress_kernel, grid=(batch, 2, KV_HEADS), in_specs=in_specs,
        out_specs=(pl.BlockSpec((None, None, None, rows, HEAD_DIM), lambda b, kv, hh: (kv, b, hh, 0, 0)),
                   pl.BlockSpec((None, None, None, HEAD_DIM, rows), lambda b, kv, hh: (kv, b, hh, 0, 0))),
        out_shape=(jax.ShapeDtypeStruct((2, batch, KV_HEADS, rows, HEAD_DIM), _BF16),
                   jax.ShapeDtypeStruct((2, batch, KV_HEADS, HEAD_DIM, rows), _BF16)),
        scratch_shapes=[pltpu.VMEM((rows, LANES), _F32)],
        compiler_params=pltpu.CompilerParams(dimension_semantics=("arbitrary",) * 3, vmem_limit_bytes=VMEM_LIMIT_BYTES),
        name="nsa_compress",
    )(kvc, pos_end, pe, w1, b1, w2, b2, *rope_consts)


def _per_head(fn, s, *shared):
    return jnp.concatenate([fn(s[:, g * Q_BLOCK:(g + 1) * Q_BLOCK], *shared) for g in range(GROUP)], axis=1)


def _nsa_kernel(qt_ref, gnt_ref, kc_ref, vct_ref, ks_ref, vst_ref, kw_ref, vwt_ref, o_ref, sa_ref, sb_ref, qx_ref,
                *, n_sel):
    rows = kc_ref.shape[0]
    n_slc = rows // 4
    seq = ks_ref.shape[0]
    qb = pl.program_id(2)
    q0 = qb * Q_BLOCK
    t = q0 + lax.broadcasted_iota(jnp.int32, (1, Q_BLOCK), 1)

    qt = qt_ref[...]
    q_t = jnp.concatenate([qt[g * HEAD_DIM:(g + 1) * HEAD_DIM, :] for g in range(GROUP)], axis=1)

    w0 = pl.multiple_of(jnp.maximum(q0 - WINDOW, 0), Q_BLOCK)
    raw_c = _dot(kc_ref[...], q_t)
    raw_w = _dot(kw_ref[pl.ds(w0, WIN_SPAN), :], q_t)
    raw_d = _dot(ks_ref[pl.ds(q0, Q_BLOCK), :HEAD_DIM], q_t)
    raw_0 = _dot(ks_ref[0:SEL_CHUNK, :HEAD_DIM], q_t)

    slot = lax.broadcasted_iota(jnp.int32, (rows, 1), 0)
    shift = n_slc.bit_length() - 1
    cmp_idx = 4 * (slot & (n_slc - 1)) + lax.shift_right_logical(slot, shift)
    valid_c = (CMP_STRIDE * cmp_idx + CMP_BLOCK - 1 <= t) & (cmp_idx < rows - 1)
    s_c = _per_head(lambda s, v: jnp.where(v, s, MASK_VALUE), raw_c, valid_c)
    e_c = jnp.exp2(s_c - jnp.max(s_c, axis=0, keepdims=True))
    has_c = t >= CMP_BLOCK - 1
    inv_c = _per_head(lambda l, ok: jnp.where(ok, 1.0 / l, 0.0), jnp.sum(e_c, axis=0, keepdims=True), has_c)
    p_c = e_c * inv_c
    o_c = _dot(vct_ref[...], p_c.astype(_BF16))

    p_sum = p_c[:, :Q_BLOCK]
    for g in range(1, GROUP):
        p_sum = p_sum + p_c[:, g * Q_BLOCK:(g + 1) * Q_BLOCK]
    parts = [p_sum[r * n_slc:(r + 1) * n_slc, :] for r in range(4)]
    blk = lax.broadcasted_iota(jnp.int32, (n_slc, 1), 0)
    prev3 = jnp.where(blk == 0, 0.0, pltpu.roll(parts[3], 1, axis=0))
    p_slc = parts[0] + parts[1] + parts[2] + parts[3] + prev3
    cur = lax.shift_right_logical(t, SLC_BLOCK.bit_length() - 1)
    forced = (blk == 0) | (blk == cur) | (blk == cur - 1)

    st = jnp.where(forced, -jnp.inf, jnp.where(blk <= cur, p_slc, -1.0))
    bi = lax.broadcasted_iota(jnp.int32, (n_slc, Q_BLOCK), 0)
    sel = forced
    for _ in range(n_sel - 3):
        mx = jnp.max(st, axis=0, keepdims=True)
        first = jnp.min(jnp.where(st == mx, bi, n_slc), axis=0, keepdims=True)
        pick = bi == first
        sel = sel | pick
        st = jnp.where(pick, -jnp.inf, st)
    past = bi < lax.shift_right_logical(q0, SLC_BLOCK.bit_length() - 1)
    pen = jnp.where(sel & past, 0.0, MASK_VALUE)
    if n_slc < LANES:
        pen = jnp.concatenate([pen, jnp.zeros((LANES - n_slc, Q_BLOCK), _F32)], axis=0)
    pen_heads = jnp.concatenate([pen] * GROUP, axis=1)
    for b in range(SEL_CHUNK // SLC_BLOCK):
        ks_rows = slice(b * SLC_BLOCK, (b + 1) * SLC_BLOCK)
        sa_ref[ks_rows, :] = raw_0[ks_rows, :] + pen_heads[b:b + 1, :]
    pen = pen.astype(_BF16)
    qx_ref[0:HEAD_DIM, :] = q_t
    qx_ref[HEAD_DIM:LANES, :] = jnp.zeros((LANES - HEAD_DIM, GROUP * Q_BLOCK), _BF16)
    qx_ref[LANES:2 * LANES, :] = jnp.concatenate([pen] * GROUP, axis=1)

    kpos_w = w0 + lax.broadcasted_iota(jnp.int32, (WIN_SPAN, 1), 0)
    diff = t - kpos_w
    bias_w = jnp.where((diff >= 0) & (diff < WINDOW), 0.0, MASK_VALUE)
    s_w = _per_head(lambda s, b: s + b, raw_w, bias_w)
    p_w = jnp.exp2(s_w - jnp.max(s_w, axis=0, keepdims=True))
    wc = jnp.maximum(qb - WINDOW // Q_BLOCK, 0)
    v_w = jnp.concatenate([vwt_ref[wc + i] for i in range(WIN_SPAN // Q_BLOCK)], axis=1)
    acc_w = _dot(v_w, p_w.astype(_BF16))
    o_w = acc_w[:HEAD_DIM] * (1.0 / acc_w[HEAD_DIM:HEAD_DIM + 1])

    n_seq_chunks = seq // SEL_CHUNK
    sub = SEL_CHUNK // Q_BLOCK
    causal = jnp.where(lax.broadcasted_iota(jnp.int32, (Q_BLOCK, Q_BLOCK), 0)
                       <= lax.broadcasted_iota(jnp.int32, (Q_BLOCK, Q_BLOCK), 1), 0.0, MASK_VALUE)
    s_d = _per_head(lambda s, b: s + b, raw_d, causal)
    m_d = jnp.max(s_d, axis=0, keepdims=True)
    acc_d = _dot(vst_ref[qb], jnp.exp2(s_d - m_d).astype(_BF16))

    def sel_scores(c, buf):
        k0 = pl.multiple_of(jnp.minimum(c, n_seq_chunks - 1) * SEL_CHUNK, SEL_CHUNK)
        buf[...] = _dot(ks_ref[pl.ds(k0, SEL_CHUNK), :], qx_ref[...])

    def sel_consume(c, buf, m, acc):
        c0 = jnp.minimum(c, n_seq_chunks - 1) * sub
        s = buf[...]
        m_new = jnp.maximum(m, jnp.max(s, axis=0, keepdims=True))
        p = jnp.exp2(s - m_new)
        v_t = jnp.concatenate([vst_ref[c0 + i] for i in range(sub)], axis=1)
        acc = jnp.exp2(m - m_new) * acc + _dot(v_t, p.astype(_BF16))
        return m_new, acc

    def sel_pair(i, carry):
        m, acc = carry
        sel_scores(2 * i + 1, sb_ref)
        m, acc = sel_consume(2 * i, sa_ref, m, acc)
        sel_scores(2 * i + 2, sa_ref)
        return sel_consume(2 * i + 1, sb_ref, m, acc)

    n_pairs = (q0 + 2 * SEL_CHUNK - 1) // (2 * SEL_CHUNK)
    _, acc_s = lax.fori_loop(0, n_pairs, sel_pair, (m_d, acc_d))
    o_s = acc_s[:HEAD_DIM] * (1.0 / acc_s[HEAD_DIM:HEAD_DIM + 1])

    gates = gnt_ref[...]
    outs = []
    for g in range(GROUP):
        ls = slice(g * Q_BLOCK, (g + 1) * Q_BLOCK)
        outs.append(gates[3 * g:3 * g + 1] * o_c[:, ls] + gates[3 * g + 1:3 * g + 2] * o_s[:, ls]
                    + gates[3 * g + 2:3 * g + 3] * o_w[:, ls])
    o_ref[...] = jnp.concatenate(outs, axis=0).T


def _nsa(qt, gnt, kc, vct, ks, vst, kw, vwt, batch, seq):
    n = qt.shape[1]
    n_qb = seq // Q_BLOCK
    rows = kc.shape[3]
    n_slc = seq // SLC_BLOCK
    qcol = lambda r: pl.BlockSpec((r, Q_BLOCK), lambda b, hh, i: (hh, b * n_qb + i))
    seq_rows = lambda w: pl.BlockSpec((None, seq, w), lambda b, hh, i: (hh, b, 0))
    vt_spec = pl.BlockSpec((None, n_qb, LANES, Q_BLOCK), lambda b, hh, i: (hh, b, 0, 0))
    in_specs = [
        qcol(GROUP * HEAD_DIM), qcol(LANES),
        pl.BlockSpec((None, None, None, rows, HEAD_DIM), lambda b, hh, i: (0, b, hh, 0, 0)),
        pl.BlockSpec((None, None, None, HEAD_DIM, rows), lambda b, hh, i: (1, b, hh, 0, 0)),
        seq_rows(2 * LANES), vt_spec, seq_rows(HEAD_DIM), vt_spec,
    ]
    return pl.pallas_call(
        functools.partial(_nsa_kernel, n_sel=min(N_SELECT, n_slc)),
        grid=(batch, KV_HEADS, n_qb), in_specs=in_specs,
        out_specs=pl.BlockSpec((Q_BLOCK, GROUP * HEAD_DIM), lambda b, hh, i: (b * n_qb + i, hh)),
        out_shape=jax.ShapeDtypeStruct((n, D_MODEL), _F32),
        scratch_shapes=[pltpu.VMEM((SEL_CHUNK, GROUP * Q_BLOCK), _F32)] * 2
        + [pltpu.VMEM((2 * LANES, GROUP * Q_BLOCK), _BF16)],
        compiler_params=pltpu.CompilerParams(dimension_semantics=("arbitrary",) * 3, vmem_limit_bytes=VMEM_LIMIT_BYTES),
        name="nsa_attention",
    )(qt, gnt, kc, vct, ks, vst, kw, vwt)


def _memkv_kernel(mem_ref, w_ref, o_ref):
    o_ref[...] = _dot(mem_ref[...].astype(_BF16), w_ref[...]).astype(_BF16)


def _memkv(mem2, w_xkv):
    m = mem2.shape[0]
    return pl.pallas_call(
        _memkv_kernel, grid=(1,),
        in_specs=[pl.BlockSpec(mem2.shape, lambda i: (0, 0)), pl.BlockSpec(w_xkv.shape, lambda i: (0, 0))],
        out_specs=pl.BlockSpec((m, 2 * D_MODEL), lambda i: (0, 0)),
        out_shape=jax.ShapeDtypeStruct((m, 2 * D_MODEL), _BF16),
        compiler_params=pltpu.CompilerParams(vmem_limit_bytes=VMEM_LIMIT_BYTES),
        name="mem_kv",
    )(mem2, w_xkv)


def _trunk_kernel(a_ref, gb_ref, yb_ref, h_ref, kvm_ref, wo_ref, wxq_ref, wxo_ref, wf1_ref, wf2_ref,
                  g1_ref, b1_ref, g2_ref, b2_ref, g3_ref, b3_ref, o_ref):
    tm = a_ref.shape[0]
    halves = [slice(i * (tm // TRUNK_SPLIT), (i + 1) * (tm // TRUNK_SPLIT)) for i in range(TRUNK_SPLIT)]
    both = lambda fn, *xs: [fn(*(x[i] for x in xs)) for i in range(TRUNK_SPLIT)]

    mix = [(a_ref[r, :] + gb_ref[r, :] * yb_ref[r, :]).astype(_BF16) for r in halves]
    y1 = both(lambda m: _dot(m, wo_ref[...]), mix)
    h1 = [_layer_norm(ALPHA * h_ref[r, :] + y, g1_ref[...], b1_ref[...]) for r, y in zip(halves, y1)]

    qx = both(lambda x: _dot(x.astype(_BF16), wxq_ref[...]).astype(_BF16), h1)
    heads = [[] for _ in range(TRUNK_SPLIT)]
    for hh in range(XATTN_HEADS):
        cs = slice(hh * XATTN_HEAD_DIM, (hh + 1) * XATTN_HEAD_DIM)
        vcs = slice(D_MODEL + hh * XATTN_HEAD_DIM, D_MODEL + (hh + 1) * XATTN_HEAD_DIM)
        s = both(lambda q: _dot_nt(q[:, cs], kvm_ref[:, cs]) * (XATTN_HEAD_DIM ** -0.5), qx)
        e = both(lambda x: jnp.exp(x - jnp.max(x, axis=-1, keepdims=True)), s)
        p = both(lambda x: (x * (1.0 / jnp.sum(x, axis=-1, keepdims=True))).astype(_BF16), e)
        for i, o in enumerate(both(lambda x: _dot(x, kvm_ref[:, vcs]), p)):
            heads[i].append(o)
    xo = both(lambda hs: jnp.concatenate(hs, axis=1).astype(_BF16), heads)
    y2 = both(lambda x: _dot(x, wxo_ref[...]), xo)
    h2 = both(lambda x, y: _layer_norm(ALPHA * x + y, g2_ref[...], b2_ref[...]), h1, y2)

    h2b = both(lambda x: x.astype(_BF16), h2)
    ff = [None] * TRUNK_SPLIT
    for c in range(D_FF // D_MODEL):
        cs = slice(c * D_MODEL, (c + 1) * D_MODEL)
        act = both(lambda x: jnp.square(jnp.maximum(_dot(x, wf1_ref[:, cs]), 0.0)).astype(_BF16), h2b)
        part = both(lambda x: _dot(x, wf2_ref[cs, :]), act)
        ff = part if c == 0 else both(lambda x, y: x + y, ff, part)
    for r, x, y in zip(halves, h2, ff):
        o_ref[r, :] = _layer_norm(ALPHA * x + y, g3_ref[...], b3_ref[...])


def _trunk(a, gb, yb, h, kvm, weights, lns, seq, mem_len):
    n = a.shape[0]
    tm = TRUNK_TILE
    tiles_per_batch = seq // tm
    row = pl.BlockSpec((tm, D_MODEL), lambda i: (i, 0))
    in_specs = ([row, row, row, row, pl.BlockSpec((mem_len, 2 * D_MODEL), lambda i: (i // tiles_per_batch, 0))]
                + [_const_spec(w.shape) for w in weights] + [_const_spec(p.shape) for p in lns])
    return pl.pallas_call(
        _trunk_kernel, grid=(n // tm,), in_specs=in_specs, out_specs=row,
        out_shape=jax.ShapeDtypeStruct((n, D_MODEL), _F32),
        compiler_params=pltpu.CompilerParams(dimension_semantics=("arbitrary",), vmem_limit_bytes=VMEM_LIMIT_BYTES),
        name="trunk",
    )(a, gb, yb, h, kvm, *weights, *lns)


def _rope_constants():
    half = ROT_DIM // 2
    inv = ROPE_THETA ** (-jnp.arange(half, dtype=_F32) / half)
    d = jnp.arange(LANES) % HEAD_DIM
    inv_lane = jnp.where(d < ROT_DIM, inv[d % half], 0.0).astype(_F32)[None, :]
    neg_first = jnp.where(d < half, -1.0, 0.0).astype(_F32)[None, :]
    pos_second = jnp.where((d >= half) & (d < ROT_DIM), 1.0, 0.0).astype(_F32)[None, :]
    return inv_lane, neg_first, pos_second


def kernel(x, mem, positions, ln_in_g, ln_in_b, w_in, gmlp_ln_g, gmlp_ln_b, gmlp_ws, gmlp_bs, cmp_k_pe, cmp_k_w1, cmp_k_b1, cmp_k_w2, cmp_k_b2, cmp_v_pe, cmp_v_w1, cmp_v_b1, cmp_v_w2, cmp_v_b2, w_out, ln1_g, ln1_b, w_xq, w_xkv, w_xo, ln2_g, ln2_b, w_ff1, w_ff2, ln3_g, ln3_b):
    batch, seq, _ = x.shape
    mem_len = mem.shape[1]
    n = batch * seq
    n_slc = seq // SLC_BLOCK
    assert w_in.shape[0] == 1, "one layer"
    assert seq % (2 * SEL_CHUNK) == 0 and seq >= WIN_SPAN and seq & (seq - 1) == 0 and n_slc <= LANES

    rope_consts = _rope_constants()
    vec = lambda p: p.reshape(1, -1)

    wi = w_in[0]
    o_u, o_v, o_q, o_kv, o_gn, o_ga, o_gb = (0, 1024, 2048, 3072, 3072 + 6 * KV_DIM, 3120 + 6 * KV_DIM, 4144 + 6 * KV_DIM)
    w_gn = wi[:, o_gn:o_ga].reshape(D_MODEL, KV_HEADS, 3 * GROUP)
    w_gn = jnp.pad(w_gn, ((0, 0), (0, 0), (0, LANES - 3 * GROUP))).reshape(D_MODEL, KV_HEADS * LANES)
    w_parts = [wi[:, o_u:o_v], wi[:, o_v:o_q], wi[:, o_ga:o_gb], wi[:, o_q:o_kv], wi[:, o_gb:], wi[:, o_kv:o_gn], w_gn]
    w_parts = [w.astype(_BF16) for w in w_parts]

    h, a, gb, qt, kvc, ks, vst, kw, vwt, gnt = _project(
        x.reshape(n, D_MODEL), positions.reshape(n, 1), vec(ln_in_g), vec(ln_in_b), w_parts,
        vec(gmlp_ln_g[0]), vec(gmlp_ln_b[0]), gmlp_ws[0], gmlp_bs[0].T, rope_consts, seq)

    rows = seq // CMP_STRIDE
    flat = CMP_STRIDE * HEAD_DIM
    pos_end = jnp.pad(positions[:, CMP_BLOCK - 1::CMP_STRIDE], ((0, 0), (0, 1)))[:, :, None]
    pad_lanes = lambda w: jnp.pad(w, ((0, 0), (0, LANES - HEAD_DIM)))
    kvcmp, kvcmp_t = _compress(
        kvc.reshape(2 * KV_HEADS, batch, rows, flat), pos_end,
        jnp.stack([cmp_k_pe[0].reshape(2, flat), cmp_v_pe[0].reshape(2, flat)]),
        jnp.stack([cmp_k_w1[0], cmp_v_w1[0]]).astype(_BF16),
        jnp.stack([vec(cmp_k_b1[0]), vec(cmp_v_b1[0])]),
        jnp.stack([pad_lanes(cmp_k_w2[0]), pad_lanes(cmp_v_w2[0])]).astype(_BF16),
        jnp.stack([pad_lanes(vec(cmp_k_b2[0])), pad_lanes(vec(cmp_v_b2[0]))]),
        rope_consts, batch)

    yb = _nsa(qt, gnt, kvcmp, kvcmp_t, ks, vst, kw, vwt, batch, seq)

    kvm = _memkv(mem.reshape(batch * mem_len, D_MODEL), w_xkv[0].astype(_BF16))
    weights = [w.astype(_BF16) for w in (w_out[0], w_xq[0], w_xo[0], w_ff1[0], w_ff2[0])]
    lns = [vec(p[0]) for p in (ln1_g, ln1_b, ln2_g, ln2_b, ln3_g, ln3_b)]
    out = _trunk(a, gb, yb, h, kvm, weights, lns, seq, mem_len)
    return out.reshape(batch, seq, D_MODEL)
```

```python
import functools
import math

import jax
import jax.numpy as jnp
from jax import lax
from jax.experimental import pallas as pl
from jax.experimental.pallas import tpu as pltpu

D_MODEL = 1024
LN_EPS = 1e-5
ALPHA = 2.0 ** 0.25
ROPE_THETA = 500000.0
GMLP_GROUPS = 8
GMLP_CHUNK = 128
NSA_HEADS = 16
HEAD_DIM = 64
KV_HEADS = 4
GROUP = NSA_HEADS // KV_HEADS
KV_DIM = KV_HEADS * HEAD_DIM
ROT_DIM = HEAD_DIM // 4
CMP_BLOCK = 32
CMP_STRIDE = 16
CMP_HIDDEN = 4 * HEAD_DIM
SLC_BLOCK = 64
N_SELECT = 16
WINDOW = 512
Q_BLOCK = 128
XATTN_HEADS = 4
XATTN_HEAD_DIM = D_MODEL // XATTN_HEADS
D_FF = 4 * D_MODEL

LANES = 128
PROJ_TILE = 512
PROJ_SPLIT = 2
TRUNK_TILE = 512
TRUNK_SPLIT = 2
SEL_CHUNK = 512
TRIP_CHUNKS = 4
WIN_SPAN = WINDOW + Q_BLOCK
MASK_VALUE = -1e30
LOG2_E = math.log2(math.e)
VMEM_LIMIT_BYTES = 56 * 1024 * 1024

_F32 = jnp.float32
_BF16 = jnp.bfloat16


def _layer_norm(x, g, b):
    mu = jnp.mean(x, axis=-1, keepdims=True)
    xc = x - mu
    var = jnp.mean(xc * xc, axis=-1, keepdims=True)
    return xc * lax.rsqrt(var + LN_EPS) * g + b


def _dot(a, b):
    return jnp.dot(a, b, preferred_element_type=_F32)


def _dot_nt(a, b):
    return lax.dot_general(a, b, (((1,), (1,)), ((), ())), preferred_element_type=_F32)


def _rope_tables(pos_f32, inv_lane, neg_first, pos_second):
    ang = pos_f32 * inv_lane
    c = jnp.cos(ang)
    s = jnp.sin(ang)
    return c, s * neg_first, s * pos_second


def _rope_block(x, tables):
    c, s_first, s_second = tables
    half = ROT_DIM // 2
    up = pltpu.roll(x, LANES - half, axis=1)
    down = pltpu.roll(x, half, axis=1)
    return x * c + up * s_first + down * s_second


def _const_spec(shape):
    nd = len(shape)
    return pl.BlockSpec(shape, lambda *_: (0,) * nd, pipeline_mode=pl.Buffered(1))


def _proj_kernel(x_ref, pos_ref, lng_ref, lnb_ref, wu_ref, wv_ref, wga_ref, wq_ref, wgb_ref, wkv_ref, wgn_ref,
                 glng_ref, glnb_ref, ws_ref, bst_ref, inv_ref, m1_ref, m2_ref,
                 h_ref, a_ref, gb_ref, qt_ref, kvc_ref, ks_ref, vst_ref, kw_ref, vwt_ref, gnt_ref, cmp_ref, *, seq):
    tm = x_ref.shape[0]
    part = tm // PROJ_SPLIT
    ti = lax.broadcasted_iota(jnp.int32, (GMLP_CHUNK, GMLP_CHUNK), 0)
    si = lax.broadcasted_iota(jnp.int32, (GMLP_CHUNK, GMLP_CHUNK), 1)
    w_tril = [jnp.where(si <= ti, ws_ref[g], 0.0).astype(_BF16) for g in range(GMLP_GROUPS)]
    bst = bst_ref[...]
    q_scale = HEAD_DIM ** -0.5 * LOG2_E
    heads_per_slab = LANES // HEAD_DIM
    blk_lane = lax.broadcasted_iota(jnp.int32, (part, LANES), 1)
    ones_col = (lax.broadcasted_iota(jnp.int32, (part, HEAD_DIM), 1) == 0).astype(_F32)

    def stages(r0):
        rs = slice(r0, r0 + part)
        h = _layer_norm(x_ref[rs, :], lng_ref[...], lnb_ref[...])
        h_ref[rs, :] = h
        hb = h.astype(_BF16)
        zu = _dot(hb, wu_ref[...])
        yield
        u_act = jax.nn.gelu(zu)
        zv = _dot(hb, wv_ref[...])
        yield
        vn = _layer_norm(jax.nn.gelu(zv), glng_ref[...], glnb_ref[...]).astype(_BF16)
        zga = _dot(hb, wga_ref[...])
        yield
        ua = u_act * jax.nn.sigmoid(zga)
        for g in range(GMLP_GROUPS):
            cs = slice(g * LANES, (g + 1) * LANES)
            for c in range(part // GMLP_CHUNK):
                cr = slice(c * GMLP_CHUNK, (c + 1) * GMLP_CHUNK)
                mixed = _dot(w_tril[g], vn[cr, cs]) + bst[:, g:g + 1]
                a_ref[r0 + c * GMLP_CHUNK:r0 + (c + 1) * GMLP_CHUNK, cs] = ua[cr, cs] * mixed
        zgb = _dot(hb, wgb_ref[...])
        yield
        gb_ref[rs, :] = jax.nn.sigmoid(zgb)
        zgn = _dot(hb, wgn_ref[...])
        yield
        gn = jax.nn.sigmoid(zgn)
        for hh in range(KV_HEADS):
            gnt_ref[hh * LANES:(hh + 1) * LANES, rs] = gn[:, hh * LANES:(hh + 1) * LANES].T
        zq = _dot(hb, wq_ref[...])
        yield
        tables = _rope_tables(pos_ref[rs, :].astype(_F32), inv_ref[...], m1_ref[...], m2_ref[...])
        for i in range(D_MODEL // LANES):
            cs = slice(i * LANES, (i + 1) * LANES)
            qt_ref[cs, rs] = (_rope_block(zq[:, cs], tables) * q_scale).T.astype(_BF16)
        zkv = _dot(hb, wkv_ref[...])
        yield
        for sl in range(2 * KV_DIM // LANES):
            cmp_ref[sl, rs, :] = zkv[:, sl * LANES:(sl + 1) * LANES]
        out_rows = slice(r0 // CMP_STRIDE, (r0 + part) // CMP_STRIDE)
        for sl in range(2 * KV_DIM // LANES):
            for l in range(CMP_STRIDE):
                pair = cmp_ref[sl, pl.ds(r0 + l, part // CMP_STRIDE, stride=CMP_STRIDE), :]
                for hh in range(heads_per_slab):
                    kvc_ref[sl * heads_per_slab + hh, out_rows, l * HEAD_DIM:(l + 1) * HEAD_DIM] = (
                        pair[:, hh * HEAD_DIM:(hh + 1) * HEAD_DIM])
        seq_pos = (pl.program_id(0) * tm + r0 + lax.broadcasted_iota(jnp.int32, (part, LANES), 0)) & (seq - 1)
        blk_onehot = jnp.where(lax.shift_right_logical(seq_pos, SLC_BLOCK.bit_length() - 1) == blk_lane, 1.0, 0.0)
        k_tail = jnp.concatenate([jnp.zeros((part, HEAD_DIM), _BF16), blk_onehot.astype(_BF16)], axis=1)
        for base, k_out, vt_out in ((2 * KV_DIM, ks_ref, vst_ref), (4 * KV_DIM, kw_ref, vwt_ref)):
            for i in range(KV_DIM // LANES):
                kr = _rope_block(zkv[:, base + i * LANES: base + (i + 1) * LANES], tables).astype(_BF16)
                for hh in range(heads_per_slab):
                    k_head = kr[:, hh * HEAD_DIM:(hh + 1) * HEAD_DIM]
                    if k_out is ks_ref:
                        k_out[2 * i + hh, rs, :] = jnp.concatenate([k_head, k_tail], axis=1)
                    else:
                        k_out[2 * i + hh, rs, :] = k_head
            for hh in range(KV_HEADS):
                v = zkv[:, base + KV_DIM + hh * HEAD_DIM: base + KV_DIM + (hh + 1) * HEAD_DIM]
                v_ext = jnp.concatenate([v, ones_col], axis=1)
                for c in range(part // Q_BLOCK):
                    vt_out[hh, r0 // Q_BLOCK + c] = v_ext[c * Q_BLOCK:(c + 1) * Q_BLOCK, :].T.astype(_BF16)
        yield

    for _ in zip(*[stages(i * part) for i in range(PROJ_SPLIT)]):
        pass


def _project(x2, pos2, ln_g, ln_b, w_parts, gln_g, gln_b, ws, bst, rope_consts, seq):
    n = x2.shape[0]
    tm = PROJ_TILE
    row = lambda shape: pl.BlockSpec(shape, lambda i: (i, 0))
    col = lambda shape: pl.BlockSpec(shape, lambda i: (0, i))
    head_rows = lambda nh, r, w: pl.BlockSpec((nh, r, w), lambda i: (0, i, 0))
    vt_spec = pl.BlockSpec((KV_HEADS, tm // Q_BLOCK, LANES, Q_BLOCK), lambda i: (0, i, 0, 0))
    w_specs = [_const_spec(w.shape) for w in w_parts]
    in_specs = ([row((tm, D_MODEL)), row((tm, 1)), _const_spec(ln_g.shape), _const_spec(ln_b.shape)] + w_specs
                + [_const_spec(gln_g.shape), _const_spec(gln_b.shape), _const_spec(ws.shape), _const_spec(bst.shape)]
                + [_const_spec(c.shape) for c in rope_consts])
    out_shape = (
        jax.ShapeDtypeStruct((n, D_MODEL), _F32),
        jax.ShapeDtypeStruct((n, D_MODEL), _F32),
        jax.ShapeDtypeStruct((n, D_MODEL), _F32),
        jax.ShapeDtypeStruct((D_MODEL, n), _BF16),
        jax.ShapeDtypeStruct((2 * KV_HEADS, n // CMP_STRIDE, CMP_STRIDE * HEAD_DIM), _F32),
        jax.ShapeDtypeStruct((KV_HEADS, n, 2 * LANES), _BF16),
        jax.ShapeDtypeStruct((KV_HEADS, n // Q_BLOCK, LANES, Q_BLOCK), _BF16),
        jax.ShapeDtypeStruct((KV_HEADS, n, HEAD_DIM), _BF16),
        jax.ShapeDtypeStruct((KV_HEADS, n // Q_BLOCK, LANES, Q_BLOCK), _BF16),
        jax.ShapeDtypeStruct((KV_HEADS * LANES, n), _F32),
    )
    out_specs = (row((tm, D_MODEL)), row((tm, D_MODEL)), row((tm, D_MODEL)), col((D_MODEL, tm)),
                 head_rows(2 * KV_HEADS, tm // CMP_STRIDE, CMP_STRIDE * HEAD_DIM),
                 head_rows(KV_HEADS, tm, 2 * LANES), vt_spec, head_rows(KV_HEADS, tm, HEAD_DIM), vt_spec,
                 col((KV_HEADS * LANES, tm)))
    return pl.pallas_call(
        functools.partial(_proj_kernel, seq=seq), grid=(n // tm,), in_specs=in_specs,
        out_specs=out_specs, out_shape=out_shape,
        scratch_shapes=[pltpu.VMEM((2 * KV_DIM // LANES, tm, LANES), _F32)],
        compiler_params=pltpu.CompilerParams(dimension_semantics=("arbitrary",), vmem_limit_bytes=VMEM_LIMIT_BYTES),
        name="nsa_proj",
    )(x2, pos2, ln_g, ln_b, *w_parts, gln_g, gln_b, ws, bst, *rope_consts)


def _compress_kernel(x_ref, pos_ref, pe_ref, w1_ref, b1_ref, w2_ref, b2_ref, inv_ref, m1_ref, m2_ref,
                     o_ref, ot_ref, perm_ref):
    rows = x_ref.shape[0]
    is_key = (pl.program_id(1) == 0).astype(_F32)
    x = x_ref[...]
    half = CMP_STRIDE * HEAD_DIM
    ya = _dot((x + pe_ref[0:1, :]).astype(_BF16), w1_ref[0:half, :])
    yb = _dot((x + pe_ref[1:2, :]).astype(_BF16), w1_ref[half:2 * half, :])
    hid = jax.nn.gelu(ya + pltpu.roll(yb, rows - 1, axis=0) + b1_ref[...])
    out = _dot(hid.astype(_BF16), w2_ref[...]) + b2_ref[...]
    tables = _rope_tables(pos_ref[...].astype(_F32) * is_key, inv_ref[...], m1_ref[...], m2_ref[...])
    perm_ref[...] = _rope_block(out, tables)
    groups = rows // 4
    for r in range(4):
        part = perm_ref[pl.ds(r, groups, stride=4), :]
        o_ref[r * groups:(r + 1) * groups, :] = part[:, :HEAD_DIM].astype(_BF16)
        ot_ref[:, r * groups:(r + 1) * groups] = part.T[:HEAD_DIM, :].astype(_BF16)


def _compress(kvc, pos_end, pe, w1, b1, w2, b2, rope_consts, batch):
    rows = kvc.shape[2]
    flat = CMP_STRIDE * HEAD_DIM
    in_specs = [
        pl.BlockSpec((None, None, rows, flat), lambda b, kv, hh: (kv * KV_HEADS + hh, b, 0, 0)),
        pl.BlockSpec((None, rows, 1), lambda b, kv, hh: (b, 0, 0)),
        pl.BlockSpec((None, 2, flat), lambda b, kv, hh: (kv, 0, 0)),
        pl.BlockSpec((None, 2 * flat, CMP_HIDDEN), lambda b, kv, hh: (kv, 0, 0)),
        pl.BlockSpec((None, 1, CMP_HIDDEN), lambda b, kv, hh: (kv, 0, 0)),
        pl.BlockSpec((None, CMP_HIDDEN, LANES), lambda b, kv, hh: (kv, 0, 0)),
        pl.BlockSpec((None, 1, LANES), lambda b, kv, hh: (kv, 0, 0)),
    ] + [pl.BlockSpec(c.shape, lambda b, kv, hh: (0, 0)) for c in rope_consts]
    return pl.pallas_call(
        _compress_kernel, grid=(batch, 2, KV_HEADS), in_specs=in_specs,
        out_specs=(pl.BlockSpec((None, None, None, rows, HEAD_DIM), lambda b, kv, hh: (kv, b, hh, 0, 0)),
                   pl.BlockSpec((None, None, None, HEAD_DIM, rows), lambda b, kv, hh: (kv, b, hh, 0, 0))),
        out_shape=(jax.ShapeDtypeStruct((2, batch, KV_HEADS, rows, HEAD_DIM), _BF16),
                   jax.ShapeDtypeStruct((2, batch, KV_HEADS, HEAD_DIM, rows), _BF16)),
        scratch_shapes=[pltpu.VMEM((rows, LANES), _F32)],
        compiler_params=pltpu.CompilerParams(dimension_semantics=("arbitrary",) * 3, vmem_limit_bytes=VMEM_LIMIT_BYTES),
        name="nsa_compress",
    )(kvc, pos_end, pe, w1, b1, w2, b2, *rope_consts)


def _per_head(fn, s, *shared):
    return jnp.concatenate([fn(s[:, g * Q_BLOCK:(g + 1) * Q_BLOCK], *shared) for g in range(GROUP)], axis=1)


def _nsa_kernel(qt_ref, gnt_ref, kc_ref, vct_ref, ks_ref, vst_ref, kw_ref, vwt_ref, o_ref, sa_ref, sb_ref, qx_ref,
                *, n_sel):
    rows = kc_ref.shape[0]
    n_slc = rows // 4
    seq = ks_ref.shape[0]
    qb = pl.program_id(2)
    q0 = qb * Q_BLOCK
    t = q0 + lax.broadcasted_iota(jnp.int32, (1, Q_BLOCK), 1)

    qt = qt_ref[...]
    q_t = jnp.concatenate([qt[g * HEAD_DIM:(g + 1) * HEAD_DIM, :] for g in range(GROUP)], axis=1)

    w0 = pl.multiple_of(jnp.maximum(q0 - WINDOW, 0), Q_BLOCK)
    raw_c = _dot(kc_ref[...], q_t)
    raw_w = _dot(kw_ref[pl.ds(w0, WIN_SPAN), :], q_t)
    raw_d = _dot(ks_ref[pl.ds(q0, Q_BLOCK), :HEAD_DIM], q_t)
    raw_0 = _dot(ks_ref[0:SEL_CHUNK, :HEAD_DIM], q_t)

    slot = lax.broadcasted_iota(jnp.int32, (rows, 1), 0)
    shift = n_slc.bit_length() - 1
    cmp_idx = 4 * (slot & (n_slc - 1)) + lax.shift_right_logical(slot, shift)
    valid_c = (CMP_STRIDE * cmp_idx + CMP_BLOCK - 1 <= t) & (cmp_idx < rows - 1)
    s_c = _per_head(lambda s, v: jnp.where(v, s, MASK_VALUE), raw_c, valid_c)
    e_c = jnp.exp2(s_c - jnp.max(s_c, axis=0, keepdims=True))
    has_c = t >= CMP_BLOCK - 1
    inv_c = _per_head(lambda l, ok: jnp.where(ok, 1.0 / l, 0.0), jnp.sum(e_c, axis=0, keepdims=True), has_c)
    p_c = e_c * inv_c
    o_c = _dot(vct_ref[...], p_c.astype(_BF16))

    p_sum = p_c[:, :Q_BLOCK]
    for g in range(1, GROUP):
        p_sum = p_sum + p_c[:, g * Q_BLOCK:(g + 1) * Q_BLOCK]
    parts = [p_sum[r * n_slc:(r + 1) * n_slc, :] for r in range(4)]
    blk = lax.broadcasted_iota(jnp.int32, (n_slc, 1), 0)
    prev3 = jnp.where(blk == 0, 0.0, pltpu.roll(parts[3], 1, axis=0))
    p_slc = parts[0] + parts[1] + parts[2] + parts[3] + prev3
    cur = lax.shift_right_logical(t, SLC_BLOCK.bit_length() - 1)
    forced = (blk == 0) | (blk == cur) | (blk == cur - 1)

    st = jnp.where(forced, -jnp.inf, jnp.where(blk <= cur, p_slc, -1.0))
    bi = lax.broadcasted_iota(jnp.int32, (n_slc, Q_BLOCK), 0)
    sel = forced
    for _ in range(n_sel - 3):
        mx = jnp.max(st, axis=0, keepdims=True)
        first = jnp.min(jnp.where(st == mx, bi, n_slc), axis=0, keepdims=True)
        pick = bi == first
        sel = sel | pick
        st = jnp.where(pick, -jnp.inf, st)
    past = bi < lax.shift_right_logical(q0, SLC_BLOCK.bit_length() - 1)
    pen = jnp.where(sel & past, 0.0, MASK_VALUE)
    if n_slc < LANES:
        pen = jnp.concatenate([pen, jnp.zeros((LANES - n_slc, Q_BLOCK), _F32)], axis=0)
    pen_heads = jnp.concatenate([pen] * GROUP, axis=1)
    for b in range(SEL_CHUNK // SLC_BLOCK):
        ks_rows = slice(b * SLC_BLOCK, (b + 1) * SLC_BLOCK)
        sa_ref[ks_rows, :] = raw_0[ks_rows, :] + pen_heads[b:b + 1, :]
    pen = pen.astype(_BF16)
    qx_ref[0:HEAD_DIM, :] = q_t
    qx_ref[HEAD_DIM:LANES, :] = jnp.zeros((LANES - HEAD_DIM, GROUP * Q_BLOCK), _BF16)
    qx_ref[LANES:2 * LANES, :] = jnp.concatenate([pen] * GROUP, axis=1)

    kpos_w = w0 + lax.broadcasted_iota(jnp.int32, (WIN_SPAN, 1), 0)
    diff = t - kpos_w
    bias_w = jnp.where((diff >= 0) & (diff < WINDOW), 0.0, MASK_VALUE)
    s_w = _per_head(lambda s, b: s + b, raw_w, bias_w)
    p_w = jnp.exp2(s_w - jnp.max(s_w, axis=0, keepdims=True))
    wc = jnp.maximum(qb - WINDOW // Q_BLOCK, 0)
    v_w = jnp.concatenate([vwt_ref[wc + i] for i in range(WIN_SPAN // Q_BLOCK)], axis=1)
    acc_w = _dot(v_w, p_w.astype(_BF16))
    o_w = acc_w[:HEAD_DIM] * (1.0 / acc_w[HEAD_DIM:HEAD_DIM + 1])

    n_seq_chunks = seq // SEL_CHUNK
    sub = SEL_CHUNK // Q_BLOCK
    causal = jnp.where(lax.broadcasted_iota(jnp.int32, (Q_BLOCK, Q_BLOCK), 0)
                       <= lax.broadcasted_iota(jnp.int32, (Q_BLOCK, Q_BLOCK), 1), 0.0, MASK_VALUE)
    s_d = _per_head(lambda s, b: s + b, raw_d, causal)
    m_d = jnp.max(s_d, axis=0, keepdims=True)
    acc_d = _dot(vst_ref[qb], jnp.exp2(s_d - m_d).astype(_BF16))

    def sel_scores(c, buf):
        k0 = pl.multiple_of(jnp.minimum(c, n_seq_chunks - 1) * SEL_CHUNK, SEL_CHUNK)
        buf[...] = _dot(ks_ref[pl.ds(k0, SEL_CHUNK), :], qx_ref[...])

    def sel_consume(c, buf, m, acc):
        c0 = jnp.minimum(c, n_seq_chunks - 1) * sub
        s = buf[...]
        m_new = jnp.maximum(m, jnp.max(s, axis=0, keepdims=True))
        p = jnp.exp2(s - m_new)
        v_t = jnp.concatenate([vst_ref[c0 + i] for i in range(sub)], axis=1)
        acc = jnp.exp2(m - m_new) * acc + _dot(v_t, p.astype(_BF16))
        return m_new, acc

    bufs = (sa_ref, sb_ref)

    def sel_trip(per_trip, base):
        def body(i, carry):
            m, acc = carry
            for j in range(per_trip):
                c = base + per_trip * i + j
                sel_scores(c + 1, bufs[(j + 1) % 2])
                m, acc = sel_consume(c, bufs[j % 2], m, acc)
            return m, acc
        return body

    n_chunks = (q0 + SEL_CHUNK - 1) // SEL_CHUNK
    n_long = n_chunks // TRIP_CHUNKS
    n_short = (n_chunks - n_long * TRIP_CHUNKS + 1) // 2
    carry = lax.fori_loop(0, n_long, sel_trip(TRIP_CHUNKS, 0), (m_d, acc_d))
    _, acc_s = lax.fori_loop(0, n_short, sel_trip(2, n_long * TRIP_CHUNKS), carry)
    o_s = acc_s[:HEAD_DIM] * (1.0 / acc_s[HEAD_DIM:HEAD_DIM + 1])

    gates = gnt_ref[...]
    outs = []
    for g in range(GROUP):
        ls = slice(g * Q_BLOCK, (g + 1) * Q_BLOCK)
        outs.append(gates[3 * g:3 * g + 1] * o_c[:, ls] + gates[3 * g + 1:3 * g + 2] * o_s[:, ls]
                    + gates[3 * g + 2:3 * g + 3] * o_w[:, ls])
    o_ref[...] = jnp.concatenate(outs, axis=0).T


def _nsa(qt, gnt, kc, vct, ks, vst, kw, vwt, batch, seq):
    n = qt.shape[1]
    n_qb = seq // Q_BLOCK
    rows = kc.shape[3]
    n_slc = seq // SLC_BLOCK
    qcol = lambda r: pl.BlockSpec((r, Q_BLOCK), lambda b, hh, i: (hh, b * n_qb + i))
    seq_rows = lambda w: pl.BlockSpec((None, seq, w), lambda b, hh, i: (hh, b, 0))
    vt_spec = pl.BlockSpec((None, n_qb, LANES, Q_BLOCK), lambda b, hh, i: (hh, b, 0, 0))
    in_specs = [
        qcol(GROUP * HEAD_DIM), qcol(LANES),
        pl.BlockSpec((None, None, None, rows, HEAD_DIM), lambda b, hh, i: (0, b, hh, 0, 0)),
        pl.BlockSpec((None, None, None, HEAD_DIM, rows), lambda b, hh, i: (1, b, hh, 0, 0)),
        seq_rows(2 * LANES), vt_spec, seq_rows(HEAD_DIM), vt_spec,
    ]
    return pl.pallas_call(
        functools.partial(_nsa_kernel, n_sel=min(N_SELECT, n_slc)),
        grid=(batch, KV_HEADS, n_qb), in_specs=in_specs,
        out_specs=pl.BlockSpec((Q_BLOCK, GROUP * HEAD_DIM), lambda b, hh, i: (b * n_qb + i, hh)),
        out_shape=jax.ShapeDtypeStruct((n, D_MODEL), _F32),
        scratch_shapes=[pltpu.VMEM((SEL_CHUNK, GROUP * Q_BLOCK), _F32)] * 2
        + [pltpu.VMEM((2 * LANES, GROUP * Q_BLOCK), _BF16)],
        compiler_params=pltpu.CompilerParams(dimension_semantics=("arbitrary",) * 3, vmem_limit_bytes=VMEM_LIMIT_BYTES),
        name="nsa_attention",
    )(qt, gnt, kc, vct, ks, vst, kw, vwt)


def _memkv_kernel(mem_ref, w_ref, o_ref):
    o_ref[...] = _dot(mem_ref[...].astype(_BF16), w_ref[...]).astype(_BF16)


def _memkv(mem2, w_xkv):
    m = mem2.shape[0]
    return pl.pallas_call(
        _memkv_kernel, grid=(1,),
        in_specs=[pl.BlockSpec(mem2.shape, lambda i: (0, 0)), pl.BlockSpec(w_xkv.shape, lambda i: (0, 0))],
        out_specs=pl.BlockSpec((m, 2 * D_MODEL), lambda i: (0, 0)),
        out_shape=jax.ShapeDtypeStruct((m, 2 * D_MODEL), _BF16),
        compiler_params=pltpu.CompilerParams(vmem_limit_bytes=VMEM_LIMIT_BYTES),
        name="mem_kv",
    )(mem2, w_xkv)


def _trunk_kernel(a_ref, gb_ref, yb_ref, h_ref, kvm_ref, wo_ref, wxq_ref, wxo_ref, wf1_ref, wf2_ref,
                  g1_ref, b1_ref, g2_ref, b2_ref, g3_ref, b3_ref, o_ref):
    tm = a_ref.shape[0]
    halves = [slice(i * (tm // TRUNK_SPLIT), (i + 1) * (tm // TRUNK_SPLIT)) for i in range(TRUNK_SPLIT)]
    both = lambda fn, *xs: [fn(*(x[i] for x in xs)) for i in range(TRUNK_SPLIT)]

    mix = [(a_ref[r, :] + gb_ref[r, :] * yb_ref[r, :]).astype(_BF16) for r in halves]
    y1 = both(lambda m: _dot(m, wo_ref[...]), mix)
    h1 = [_layer_norm(ALPHA * h_ref[r, :] + y, g1_ref[...], b1_ref[...]) for r, y in zip(halves, y1)]

    qx = both(lambda x: _dot(x.astype(_BF16), wxq_ref[...]).astype(_BF16), h1)
    heads = [[] for _ in range(TRUNK_SPLIT)]
    for hh in range(XATTN_HEADS):
        cs = slice(hh * XATTN_HEAD_DIM, (hh + 1) * XATTN_HEAD_DIM)
        vcs = slice(D_MODEL + hh * XATTN_HEAD_DIM, D_MODEL + (hh + 1) * XATTN_HEAD_DIM)
        s = both(lambda q: _dot_nt(q[:, cs], kvm_ref[:, cs]) * (XATTN_HEAD_DIM ** -0.5), qx)
        e = both(lambda x: jnp.exp(x - jnp.max(x, axis=-1, keepdims=True)), s)
        p = both(lambda x: (x * (1.0 / jnp.sum(x, axis=-1, keepdims=True))).astype(_BF16), e)
        for i, o in enumerate(both(lambda x: _dot(x, kvm_ref[:, vcs]), p)):
            heads[i].append(o)
    xo = both(lambda hs: jnp.concatenate(hs, axis=1).astype(_BF16), heads)
    y2 = both(lambda x: _dot(x, wxo_ref[...]), xo)
    h2 = both(lambda x, y: _layer_norm(ALPHA * x + y, g2_ref[...], b2_ref[...]), h1, y2)

    h2b = both(lambda x: x.astype(_BF16), h2)
    ff = [None] * TRUNK_SPLIT
    for c in range(D_FF // D_MODEL):
        cs = slice(c * D_MODEL, (c + 1) * D_MODEL)
        act = both(lambda x: jnp.square(jnp.maximum(_dot(x, wf1_ref[:, cs]), 0.0)).astype(_BF16), h2b)
        part = both(lambda x: _dot(x, wf2_ref[cs, :]), act)
        ff = part if c == 0 else both(lambda x, y: x + y, ff, part)
    for r, x, y in zip(halves, h2, ff):
        o_ref[r, :] = _layer_norm(ALPHA * x + y, g3_ref[...], b3_ref[...])


def _trunk(a, gb, yb, h, kvm, weights, lns, seq, mem_len):
    n = a.shape[0]
    tm = TRUNK_TILE
    tiles_per_batch = seq // tm
    row = pl.BlockSpec((tm, D_MODEL), lambda i: (i, 0))
    in_specs = ([row, row, row, row, pl.BlockSpec((mem_len, 2 * D_MODEL), lambda i: (i // tiles_per_batch, 0))]
                + [_const_spec(w.shape) for w in weights] + [_const_spec(p.shape) for p in lns])
    return pl.pallas_call(
        _trunk_kernel, grid=(n // tm,), in_specs=in_specs, out_specs=row,
        out_shape=jax.ShapeDtypeStruct((n, D_MODEL), _F32),
        compiler_params=pltpu.CompilerParams(dimension_semantics=("arbitrary",), vmem_limit_bytes=VMEM_LIMIT_BYTES),
        name="trunk",
    )(a, gb, yb, h, kvm, *weights, *lns)


def _rope_constants():
    half = ROT_DIM // 2
    inv = ROPE_THETA ** (-jnp.arange(half, dtype=_F32) / half)
    d = jnp.arange(LANES) % HEAD_DIM
    inv_lane = jnp.where(d < ROT_DIM, inv[d % half], 0.0).astype(_F32)[None, :]
    neg_first = jnp.where(d < half, -1.0, 0.0).astype(_F32)[None, :]
    pos_second = jnp.where((d >= half) & (d < ROT_DIM), 1.0, 0.0).astype(_F32)[None, :]
    return inv_lane, neg_first, pos_second


def kernel(x, mem, positions, ln_in_g, ln_in_b, w_in, gmlp_ln_g, gmlp_ln_b, gmlp_ws, gmlp_bs, cmp_k_pe, cmp_k_w1, cmp_k_b1, cmp_k_w2, cmp_k_b2, cmp_v_pe, cmp_v_w1, cmp_v_b1, cmp_v_w2, cmp_v_b2, w_out, ln1_g, ln1_b, w_xq, w_xkv, w_xo, ln2_g, ln2_b, w_ff1, w_ff2, ln3_g, ln3_b):
    batch, seq, _ = x.shape
    mem_len = mem.shape[1]
    n = batch * seq
    n_slc = seq // SLC_BLOCK
    assert w_in.shape[0] == 1, "one layer"
    assert seq % (TRIP_CHUNKS * SEL_CHUNK) == 0 and seq >= WIN_SPAN and seq & (seq - 1) == 0 and n_slc <= LANES

    rope_consts = _rope_constants()
    vec = lambda p: p.reshape(1, -1)

    wi = w_in[0]
    o_u, o_v, o_q, o_kv, o_gn, o_ga, o_gb = (0, 1024, 2048, 3072, 3072 + 6 * KV_DIM, 3120 + 6 * KV_DIM, 4144 + 6 * KV_DIM)
    w_gn = wi[:, o_gn:o_ga].reshape(D_MODEL, KV_HEADS, 3 * GROUP)
    w_gn = jnp.pad(w_gn, ((0, 0), (0, 0), (0, LANES - 3 * GROUP))).reshape(D_MODEL, KV_HEADS * LANES)
    w_parts = [wi[:, o_u:o_v], wi[:, o_v:o_q], wi[:, o_ga:o_gb], wi[:, o_q:o_kv], wi[:, o_gb:], wi[:, o_kv:o_gn], w_gn]
    w_parts = [w.astype(_BF16) for w in w_parts]

    h, a, gb, qt, kvc, ks, vst, kw, vwt, gnt = _project(
        x.reshape(n, D_MODEL), positions.reshape(n, 1), vec(ln_in_g), vec(ln_in_b), w_parts,
        vec(gmlp_ln_g[0]), vec(gmlp_ln_b[0]), gmlp_ws[0], gmlp_bs[0].T, rope_consts, seq)

    rows = seq // CMP_STRIDE
    flat = CMP_STRIDE * HEAD_DIM
    pos_end = jnp.pad(positions[:, CMP_BLOCK - 1::CMP_STRIDE], ((0, 0), (0, 1)))[:, :, None]
    pad_lanes = lambda w: jnp.pad(w, ((0, 0), (0, LANES - HEAD_DIM)))
    kvcmp, kvcmp_t = _compress(
        kvc.reshape(2 * KV_HEADS, batch, rows, flat), pos_end,
        jnp.stack([cmp_k_pe[0].reshape(2, flat), cmp_v_pe[0].reshape(2, flat)]),
        jnp.stack([cmp_k_w1[0], cmp_v_w1[0]]).astype(_BF16),
        jnp.stack([vec(cmp_k_b1[0]), vec(cmp_v_b1[0])]),
        jnp.stack([pad_lanes(cmp_k_w2[0]), pad_lanes(cmp_v_w2[0])]).astype(_BF16),
        jnp.stack([pad_lanes(vec(cmp_k_b2[0])), pad_lanes(vec(cmp_v_b2[0]))]),
        rope_consts, batch)

    yb = _nsa(qt, gnt, kvcmp, kvcmp_t, ks, vst, kw, vwt, batch, seq)

    kvm = _memkv(mem.reshape(batch * mem_len, D_MODEL), w_xkv[0].astype(_BF16))
    weights = [w.astype(_BF16) for w in (w_out[0], w_xq[0], w_xo[0], w_ff1[0], w_ff2[0])]
    lns = [vec(p[0]) for p in (ln1_g, ln1_b, ln2_g, ln2_b, ln3_g, ln3_b)]
    out = _trunk(a, gb, yb, h, kvm, weights, lns, seq, mem_len)
    return out.reshape(batch, seq, D_MODEL)
```

```python
import functools
import math

import jax
import jax.numpy as jnp
from jax import lax
from jax.experimental import pallas as pl
from jax.experimental.pallas import tpu as pltpu

D_MODEL = 1024
LN_EPS = 1e-5
ALPHA = 2.0 ** 0.25
ROPE_THETA = 500000.0
GMLP_GROUPS = 8
GMLP_CHUNK = 128
NSA_HEADS = 16
HEAD_DIM = 64
KV_HEADS = 4
GROUP = NSA_HEADS // KV_HEADS
KV_DIM = KV_HEADS * HEAD_DIM
ROT_DIM = HEAD_DIM // 4
CMP_BLOCK = 32
CMP_STRIDE = 16
CMP_HIDDEN = 4 * HEAD_DIM
SLC_BLOCK = 64
N_SELECT = 16
WINDOW = 512
Q_BLOCK = 128
KV_CHUNK = 128
XATTN_HEADS = 4
XATTN_HEAD_DIM = D_MODEL // XATTN_HEADS
D_FF = 4 * D_MODEL

LANES = 128
PROJ_TILE = 512
PROJ_SPLIT = 2
TRUNK_TILE = 512
TRUNK_SPLIT = 2
SEL_CHUNK = 512
TRIP_CHUNKS = 4
WIN_SPAN = WINDOW + Q_BLOCK
WIN_PIECES = 5
MASK_VALUE = -1e30
LOG2_E = math.log2(math.e)
VMEM_LIMIT_BYTES = 56 * 1024 * 1024

_F32 = jnp.float32
_BF16 = jnp.bfloat16


def _layer_norm(x, g, b):
    mu = jnp.mean(x, axis=-1, keepdims=True)
    xc = x - mu
    var = jnp.mean(xc * xc, axis=-1, keepdims=True)
    return xc * lax.rsqrt(var + LN_EPS) * g + b


def _dot(a, b):
    return jnp.dot(a, b, preferred_element_type=_F32)


def _dot_nt(a, b):
    return lax.dot_general(a, b, (((1,), (1,)), ((), ())), preferred_element_type=_F32)


def _rope_tables(pos_f32, inv_lane, neg_first, pos_second):
    ang = pos_f32 * inv_lane
    c = jnp.cos(ang)
    s = jnp.sin(ang)
    return c, s * neg_first, s * pos_second


def _rope_block(x, tables):
    c, s_first, s_second = tables
    half = ROT_DIM // 2
    up = pltpu.roll(x, LANES - half, axis=1)
    down = pltpu.roll(x, half, axis=1)
    return x * c + up * s_first + down * s_second


def _const_spec(shape):
    nd = len(shape)
    return pl.BlockSpec(shape, lambda *_: (0,) * nd, pipeline_mode=pl.Buffered(1))


def _proj_kernel(x_ref, pos_ref, lng_ref, lnb_ref, wu_ref, wv_ref, wga_ref, wq_ref, wgb_ref, wkv_ref, wgn_ref,
                 glng_ref, glnb_ref, ws_ref, bst_ref, inv_ref, m1_ref, m2_ref,
                 h_ref, a_ref, gb_ref, qt_ref, kvc_ref, ks_ref, vst_ref, kw_ref, vwt_ref, gnt_ref, cmp_ref, *, seq):
    tm = x_ref.shape[0]
    part = tm // PROJ_SPLIT
    ti = lax.broadcasted_iota(jnp.int32, (GMLP_CHUNK, GMLP_CHUNK), 0)
    si = lax.broadcasted_iota(jnp.int32, (GMLP_CHUNK, GMLP_CHUNK), 1)
    w_tril = [jnp.where(si <= ti, ws_ref[g], 0.0).astype(_BF16) for g in range(GMLP_GROUPS)]
    bst = bst_ref[...]
    q_scale = HEAD_DIM ** -0.5 * LOG2_E
    heads_per_slab = LANES // HEAD_DIM
    blk_lane = lax.broadcasted_iota(jnp.int32, (part, LANES), 1)
    ones_col = (lax.broadcasted_iota(jnp.int32, (part, HEAD_DIM), 1) == 0).astype(_F32)

    def stages(r0):
        rs = slice(r0, r0 + part)
        h = _layer_norm(x_ref[rs, :], lng_ref[...], lnb_ref[...])
        h_ref[rs, :] = h
        hb = h.astype(_BF16)
        zu = _dot(hb, wu_ref[...])
        yield
        u_act = jax.nn.gelu(zu)
        zv = _dot(hb, wv_ref[...])
        yield
        vn = _layer_norm(jax.nn.gelu(zv), glng_ref[...], glnb_ref[...]).astype(_BF16)
        zga = _dot(hb, wga_ref[...])
        yield
        ua = u_act * jax.nn.sigmoid(zga)
        for g in range(GMLP_GROUPS):
            cs = slice(g * LANES, (g + 1) * LANES)
            for c in range(part // GMLP_CHUNK):
                cr = slice(c * GMLP_CHUNK, (c + 1) * GMLP_CHUNK)
                mixed = _dot(w_tril[g], vn[cr, cs]) + bst[:, g:g + 1]
                a_ref[r0 + c * GMLP_CHUNK:r0 + (c + 1) * GMLP_CHUNK, cs] = ua[cr, cs] * mixed
        zgb = _dot(hb, wgb_ref[...])
        yield
        gb_ref[rs, :] = jax.nn.sigmoid(zgb)
        zgn = _dot(hb, wgn_ref[...])
        yield
        gn = jax.nn.sigmoid(zgn)
        for hh in range(KV_HEADS):
            gnt_ref[hh * LANES:(hh + 1) * LANES, rs] = gn[:, hh * LANES:(hh + 1) * LANES].T
        zq = _dot(hb, wq_ref[...])
        yield
        tables = _rope_tables(pos_ref[rs, :].astype(_F32), inv_ref[...], m1_ref[...], m2_ref[...])
        for i in range(D_MODEL // LANES):
            cs = slice(i * LANES, (i + 1) * LANES)
            qt_ref[cs, rs] = (_rope_block(zq[:, cs], tables) * q_scale).T.astype(_BF16)
        zkv = _dot(hb, wkv_ref[...])
        yield
        for sl in range(2 * KV_DIM // LANES):
            cmp_ref[sl, rs, :] = zkv[:, sl * LANES:(sl + 1) * LANES]
        out_rows = slice(r0 // CMP_STRIDE, (r0 + part) // CMP_STRIDE)
        for sl in range(2 * KV_DIM // LANES):
            for l in range(CMP_STRIDE):
                pair = cmp_ref[sl, pl.ds(r0 + l, part // CMP_STRIDE, stride=CMP_STRIDE), :]
                for hh in range(heads_per_slab):
                    kvc_ref[sl * heads_per_slab + hh, out_rows, l * HEAD_DIM:(l + 1) * HEAD_DIM] = (
                        pair[:, hh * HEAD_DIM:(hh + 1) * HEAD_DIM])
        seq_pos = (pl.program_id(0) * tm + r0 + lax.broadcasted_iota(jnp.int32, (part, LANES), 0)) & (seq - 1)
        blk_onehot = jnp.where(lax.shift_right_logical(seq_pos, SLC_BLOCK.bit_length() - 1) == blk_lane, 1.0, 0.0)
        k_tail = jnp.concatenate([jnp.zeros((part, HEAD_DIM), _BF16), blk_onehot.astype(_BF16)], axis=1)
        for base, k_out, vt_out in ((2 * KV_DIM, ks_ref, vst_ref), (4 * KV_DIM, kw_ref, vwt_ref)):
            for i in range(KV_DIM // LANES):
                kr = _rope_block(zkv[:, base + i * LANES: base + (i + 1) * LANES], tables).astype(_BF16)
                for hh in range(heads_per_slab):
                    k_head = kr[:, hh * HEAD_DIM:(hh + 1) * HEAD_DIM]
                    if k_out is ks_ref:
                        k_out[2 * i + hh, rs, :] = jnp.concatenate([k_head, k_tail], axis=1)
                    else:
                        k_out[2 * i + hh, rs, :] = k_head
            for hh in range(KV_HEADS):
                v = zkv[:, base + KV_DIM + hh * HEAD_DIM: base + KV_DIM + (hh + 1) * HEAD_DIM]
                v_ext = jnp.concatenate([v, ones_col], axis=1)
                for c in range(part // KV_CHUNK):
                    vt_out[hh, r0 // KV_CHUNK + c] = v_ext[c * KV_CHUNK:(c + 1) * KV_CHUNK, :].T.astype(_BF16)
        yield

    for _ in zip(*[stages(i * part) for i in range(PROJ_SPLIT)]):
        pass


def _project(x2, pos2, ln_g, ln_b, w_parts, gln_g, gln_b, ws, bst, rope_consts, seq):
    n = x2.shape[0]
    tm = PROJ_TILE
    row = lambda shape: pl.BlockSpec(shape, lambda i: (i, 0))
    col = lambda shape: pl.BlockSpec(shape, lambda i: (0, i))
    head_rows = lambda nh, r, w: pl.BlockSpec((nh, r, w), lambda i: (0, i, 0))
    vt_spec = pl.BlockSpec((KV_HEADS, tm // KV_CHUNK, LANES, KV_CHUNK), lambda i: (0, i, 0, 0))
    w_specs = [_const_spec(w.shape) for w in w_parts]
    in_specs = ([row((tm, D_MODEL)), row((tm, 1)), _const_spec(ln_g.shape), _const_spec(ln_b.shape)] + w_specs
                + [_const_spec(gln_g.shape), _const_spec(gln_b.shape), _const_spec(ws.shape), _const_spec(bst.shape)]
                + [_const_spec(c.shape) for c in rope_consts])
    out_shape = (
        jax.ShapeDtypeStruct((n, D_MODEL), _F32),
        jax.ShapeDtypeStruct((n, D_MODEL), _F32),
        jax.ShapeDtypeStruct((n, D_MODEL), _F32),
        jax.ShapeDtypeStruct((D_MODEL, n), _BF16),
        jax.ShapeDtypeStruct((2 * KV_HEADS, n // CMP_STRIDE, CMP_STRIDE * HEAD_DIM), _F32),
        jax.ShapeDtypeStruct((KV_HEADS, n, 2 * LANES), _BF16),
        jax.ShapeDtypeStruct((KV_HEADS, n // KV_CHUNK, LANES, KV_CHUNK), _BF16),
        jax.ShapeDtypeStruct((KV_HEADS, n, HEAD_DIM), _BF16),
        jax.ShapeDtypeStruct((KV_HEADS, n // KV_CHUNK, LANES, KV_CHUNK), _BF16),
        jax.ShapeDtypeStruct((KV_HEADS * LANES, n), _F32),
    )
    out_specs = (row((tm, D_MODEL)), row((tm, D_MODEL)), row((tm, D_MODEL)), col((D_MODEL, tm)),
                 head_rows(2 * KV_HEADS, tm // CMP_STRIDE, CMP_STRIDE * HEAD_DIM),
                 head_rows(KV_HEADS, tm, 2 * LANES), vt_spec, head_rows(KV_HEADS, tm, HEAD_DIM), vt_spec,
                 col((KV_HEADS * LANES, tm)))
    return pl.pallas_call(
        functools.partial(_proj_kernel, seq=seq), grid=(n // tm,), in_specs=in_specs,
        out_specs=out_specs, out_shape=out_shape,
        scratch_shapes=[pltpu.VMEM((2 * KV_DIM // LANES, tm, LANES), _F32)],
        compiler_params=pltpu.CompilerParams(dimension_semantics=("arbitrary",), vmem_limit_bytes=VMEM_LIMIT_BYTES),
        name="nsa_proj",
    )(x2, pos2, ln_g, ln_b, *w_parts, gln_g, gln_b, ws, bst, *rope_consts)


def _compress_kernel(x_ref, pos_ref, pe_ref, w1_ref, b1_ref, w2_ref, b2_ref, inv_ref, m1_ref, m2_ref,
                     o_ref, ot_ref, perm_ref):
    rows = x_ref.shape[0]
    is_key = (pl.program_id(1) == 0).astype(_F32)
    x = x_ref[...]
    half = CMP_STRIDE * HEAD_DIM
    ya = _dot((x + pe_ref[0:1, :]).astype(_BF16), w1_ref[0:half, :])
    yb = _dot((x + pe_ref[1:2, :]).astype(_BF16), w1_ref[half:2 * half, :])
    hid = jax.nn.gelu(ya + pltpu.roll(yb, rows - 1, axis=0) + b1_ref[...])
    out = _dot(hid.astype(_BF16), w2_ref[...]) + b2_ref[...]
    tables = _rope_tables(pos_ref[...].astype(_F32) * is_key, inv_ref[...], m1_ref[...], m2_ref[...])
    perm_ref[...] = _rope_block(out, tables)
    groups = rows // 4
    for r in range(4):
        part = perm_ref[pl.ds(r, groups, stride=4), :]
        o_ref[r * groups:(r + 1) * groups, :] = part[:, :HEAD_DIM].astype(_BF16)
        ot_ref[:, r * groups:(r + 1) * groups] = part.T[:HEAD_DIM, :].astype(_BF16)


def _compress(kvc, pos_end, pe, w1, b1, w2, b2, rope_consts, batch):
    rows = kvc.shape[2]
    flat = CMP_STRIDE * HEAD_DIM
    in_specs = [
        pl.BlockSpec((None, None, rows, flat), lambda b, kv, hh: (kv * KV_HEADS + hh, b, 0, 0)),
        pl.BlockSpec((None, rows, 1), lambda b, kv, hh: (b, 0, 0)),
        pl.BlockSpec((None, 2, flat), lambda b, kv, hh: (kv, 0, 0)),
        pl.BlockSpec((None, 2 * flat, CMP_HIDDEN), lambda b, kv, hh: (kv, 0, 0)),
        pl.BlockSpec((None, 1, CMP_HIDDEN), lambda b, kv, hh: (kv, 0, 0)),
        pl.BlockSpec((None, CMP_HIDDEN, LANES), lambda b, kv, hh: (kv, 0, 0)),
        pl.BlockSpec((None, 1, LANES), lambda b, kv, hh: (kv, 0, 0)),
    ] + [pl.BlockSpec(c.shape, lambda b, kv, hh: (0, 0)) for c in rope_consts]
    return pl.pallas_call(
        _compress_kernel, grid=(batch, 2, KV_HEADS), in_specs=in_specs,
        out_specs=(pl.BlockSpec((None, None, None, rows, HEAD_DIM), lambda b, kv, hh: (kv, b, hh, 0, 0)),
                   pl.BlockSpec((None, None, None, HEAD_DIM, rows), lambda b, kv, hh: (kv, b, hh, 0, 0))),
        out_shape=(jax.ShapeDtypeStruct((2, batch, KV_HEADS, rows, HEAD_DIM), _BF16),
                   jax.ShapeDtypeStruct((2, batch, KV_HEADS, HEAD_DIM, rows), _BF16)),
        scratch_shapes=[pltpu.VMEM((rows, LANES), _F32)],
        compiler_params=pltpu.CompilerParams(dimension_semantics=("arbitrary",) * 3, vmem_limit_bytes=VMEM_LIMIT_BYTES),
        name="nsa_compress",
    )(kvc, pos_end, pe, w1, b1, w2, b2, *rope_consts)


def _per_head(fn, s, *shared):
    return jnp.concatenate([fn(s[:, g * Q_BLOCK:(g + 1) * Q_BLOCK], *shared) for g in range(GROUP)], axis=1)


def _nsa_kernel(qt_ref, gnt_ref, kc_ref, vct_ref, ks_ref, vst_ref, kw_ref, vwt_ref, cend_ref, wbias_ref,
                o_ref, sa_ref, sb_ref, qx_ref,
                *, n_sel):
    rows = kc_ref.shape[0]
    n_slc = rows // 4
    seq = ks_ref.shape[0]
    qb = pl.program_id(2)
    q0 = qb * Q_BLOCK
    t = q0 + lax.broadcasted_iota(jnp.int32, (1, Q_BLOCK), 1)

    qt = qt_ref[...]
    q_t = jnp.concatenate([qt[g * HEAD_DIM:(g + 1) * HEAD_DIM, :] for g in range(GROUP)], axis=1)

    w0 = pl.multiple_of(jnp.maximum(q0 - WINDOW, 0), KV_CHUNK)
    raw_c = _dot(kc_ref[...], q_t)
    raw_w = _dot(kw_ref[pl.ds(w0, WIN_SPAN), :], q_t)
    raw_d = _dot(ks_ref[pl.ds(q0, Q_BLOCK), :HEAD_DIM], q_t)
    raw_0 = _dot(ks_ref[0:SEL_CHUNK, :HEAD_DIM], q_t)

    valid_c = cend_ref[...] <= t
    s_c = _per_head(lambda s, v: jnp.where(v, s, MASK_VALUE), raw_c, valid_c)
    e_c = jnp.exp2(s_c - jnp.max(s_c, axis=0, keepdims=True))
    has_c = t >= CMP_BLOCK - 1
    inv_c = _per_head(lambda l, ok: jnp.where(ok, 1.0 / l, 0.0), jnp.sum(e_c, axis=0, keepdims=True), has_c)
    p_c = e_c * inv_c
    o_c = _dot(vct_ref[...], p_c.astype(_BF16))

    p_sum = p_c[:, :Q_BLOCK]
    for g in range(1, GROUP):
        p_sum = p_sum + p_c[:, g * Q_BLOCK:(g + 1) * Q_BLOCK]
    parts = [p_sum[r * n_slc:(r + 1) * n_slc, :] for r in range(4)]
    blk = lax.broadcasted_iota(jnp.int32, (n_slc, 1), 0)
    prev3 = jnp.where(blk == 0, 0.0, pltpu.roll(parts[3], 1, axis=0))
    p_slc = parts[0] + parts[1] + parts[2] + parts[3] + prev3
    cur = lax.shift_right_logical(t, SLC_BLOCK.bit_length() - 1)
    forced = (blk == 0) | (blk == cur) | (blk == cur - 1)

    st = jnp.where(forced, -jnp.inf, jnp.where(blk <= cur, p_slc, -1.0))
    bi = lax.broadcasted_iota(jnp.int32, (n_slc, Q_BLOCK), 0)
    bf = bi.astype(_F32)
    bias_w = wbias_ref[jnp.minimum(qb, WINDOW // Q_BLOCK)]
    s_w = _per_head(lambda s, b: s + b, raw_w, bias_w)
    m_w = jnp.max(s_w, axis=0, keepdims=True)
    piece = WIN_SPAN // WIN_PIECES
    causal = jnp.where(lax.broadcasted_iota(jnp.int32, (Q_BLOCK, Q_BLOCK), 0)
                       <= lax.broadcasted_iota(jnp.int32, (Q_BLOCK, Q_BLOCK), 1), 0.0, MASK_VALUE)
    s_d = _per_head(lambda s, b: s + b, raw_d, causal)
    m_d = jnp.max(s_d, axis=0, keepdims=True)

    p_w_parts = []
    for r in range(n_sel - 3):
        mx = jnp.max(st, axis=0, keepdims=True)
        first = jnp.min(jnp.where(st == mx, bf, float(n_slc)), axis=0, keepdims=True)
        if r < WIN_PIECES:
            part = jnp.exp2(s_w[r * piece:(r + 1) * piece, :] - m_w)
            p_w_parts.append(part.astype(_BF16))
            col_max = jnp.max(part, axis=0, keepdims=True)
            tie = col_max[:, :Q_BLOCK]
            for g in range(1, GROUP):
                tie = jnp.maximum(tie, col_max[:, g * Q_BLOCK:(g + 1) * Q_BLOCK])
            first = first + tie * 0.0
        st = jnp.where(bf == first, -jnp.inf, st)
    p_d = jnp.exp2(s_d - m_d).astype(_BF16)
    sel = st == -jnp.inf
    past = bi < lax.shift_right_logical(q0, SLC_BLOCK.bit_length() - 1)
    pen = jnp.where(sel & past, 0.0, MASK_VALUE)
    if n_slc < LANES:
        pen = jnp.concatenate([pen, jnp.zeros((LANES - n_slc, Q_BLOCK), _F32)], axis=0)
    pen_heads = jnp.concatenate([pen] * GROUP, axis=1)
    for b in range(SEL_CHUNK // SLC_BLOCK):
        ks_rows = slice(b * SLC_BLOCK, (b + 1) * SLC_BLOCK)
        sa_ref[ks_rows, :] = raw_0[ks_rows, :] + pen_heads[b:b + 1, :]
    pen = pen.astype(_BF16)
    qx_ref[0:HEAD_DIM, :] = q_t
    qx_ref[HEAD_DIM:LANES, :] = jnp.zeros((LANES - HEAD_DIM, GROUP * Q_BLOCK), _BF16)
    qx_ref[LANES:2 * LANES, :] = jnp.concatenate([pen] * GROUP, axis=1)

    wc = lax.shift_right_logical(w0, KV_CHUNK.bit_length() - 1)
    v_w = jnp.concatenate([vwt_ref[wc + i] for i in range(WIN_SPAN // KV_CHUNK)], axis=1)
    acc_w = _dot(v_w, jnp.concatenate(p_w_parts, axis=0))
    o_w = acc_w[:HEAD_DIM] * (1.0 / acc_w[HEAD_DIM:HEAD_DIM + 1])

    n_seq_chunks = seq // SEL_CHUNK
    sub = SEL_CHUNK // KV_CHUNK
    acc_d = _dot(vst_ref[qb], p_d)

    def sel_scores(c, buf):
        k0 = pl.multiple_of(jnp.minimum(c, n_seq_chunks - 1) * SEL_CHUNK, SEL_CHUNK)
        buf[...] = _dot(ks_ref[pl.ds(k0, SEL_CHUNK), :], qx_ref[...])

    def sel_consume(c, buf, m, acc):
        c0 = jnp.minimum(c, n_seq_chunks - 1) * sub
        s = buf[...]
        m_new = jnp.maximum(m, jnp.max(s, axis=0, keepdims=True))
        p = jnp.exp2(s - m_new)
        v_t = jnp.concatenate([vst_ref[c0 + i] for i in range(sub)], axis=1)
        acc = jnp.exp2(m - m_new) * acc + _dot(v_t, p.astype(_BF16))
        return m_new, acc

    bufs = (sa_ref, sb_ref)

    def sel_trip(per_trip, base):
        def body(i, carry):
            m, acc = carry
            for j in range(per_trip):
                c = base + per_trip * i + j
                sel_scores(c + 1, bufs[(j + 1) % 2])
                m, acc = sel_consume(c, bufs[j % 2], m, acc)
            return m, acc
        return body

    n_chunks = (q0 + SEL_CHUNK - 1) // SEL_CHUNK
    n_long = n_chunks // TRIP_CHUNKS
    n_short = (n_chunks - n_long * TRIP_CHUNKS + 1) // 2
    carry = lax.fori_loop(0, n_long, sel_trip(TRIP_CHUNKS, 0), (m_d, acc_d))
    _, acc_s = lax.fori_loop(0, n_short, sel_trip(2, n_long * TRIP_CHUNKS), carry)
    o_s = acc_s[:HEAD_DIM] * (1.0 / acc_s[HEAD_DIM:HEAD_DIM + 1])

    gates = gnt_ref[...]
    outs = []
    for g in range(GROUP):
        ls = slice(g * Q_BLOCK, (g + 1) * Q_BLOCK)
        outs.append(gates[3 * g:3 * g + 1] * o_c[:, ls] + gates[3 * g + 1:3 * g + 2] * o_s[:, ls]
                    + gates[3 * g + 2:3 * g + 3] * o_w[:, ls])
    o_ref[...] = jnp.concatenate(outs, axis=0).T


def _nsa(qt, gnt, kc, vct, ks, vst, kw, vwt, batch, seq):
    n = qt.shape[1]
    n_qb = seq // Q_BLOCK
    rows = kc.shape[3]
    n_slc = seq // SLC_BLOCK
    qcol = lambda r: pl.BlockSpec((r, Q_BLOCK), lambda b, hh, i: (hh, b * n_qb + i))
    seq_rows = lambda w: pl.BlockSpec((None, seq, w), lambda b, hh, i: (hh, b, 0))
    vt_spec = pl.BlockSpec((None, seq // KV_CHUNK, LANES, KV_CHUNK), lambda b, hh, i: (hh, b, 0, 0))
    slot = jnp.arange(rows)
    cmp_idx = 4 * (slot % n_slc) + slot // n_slc
    cmp_end = jnp.where(cmp_idx < rows - 1, CMP_STRIDE * cmp_idx + CMP_BLOCK - 1, jnp.iinfo(jnp.int32).max)
    cmp_end = jnp.broadcast_to(cmp_end.astype(jnp.int32)[:, None], (rows, Q_BLOCK))
    off = Q_BLOCK * jnp.arange(WINDOW // Q_BLOCK + 1)[:, None, None]
    diff = off + jnp.arange(Q_BLOCK)[None, None, :] - jnp.arange(WIN_SPAN)[None, :, None]
    win_bias = jnp.where((diff >= 0) & (diff < WINDOW), 0.0, MASK_VALUE).astype(_F32)
    in_specs = [
        qcol(GROUP * HEAD_DIM), qcol(LANES),
        pl.BlockSpec((None, None, None, rows, HEAD_DIM), lambda b, hh, i: (0, b, hh, 0, 0)),
        pl.BlockSpec((None, None, None, HEAD_DIM, rows), lambda b, hh, i: (1, b, hh, 0, 0)),
        seq_rows(2 * LANES), vt_spec, seq_rows(HEAD_DIM), vt_spec,
        _const_spec(cmp_end.shape), _const_spec(win_bias.shape),
    ]
    return pl.pallas_call(
        functools.partial(_nsa_kernel, n_sel=min(N_SELECT, n_slc)),
        grid=(batch, KV_HEADS, n_qb), in_specs=in_specs,
        out_specs=pl.BlockSpec((Q_BLOCK, GROUP * HEAD_DIM), lambda b, hh, i: (b * n_qb + i, hh)),
        out_shape=jax.ShapeDtypeStruct((n, D_MODEL), _F32),
        scratch_shapes=[pltpu.VMEM((SEL_CHUNK, GROUP * Q_BLOCK), _F32)] * 2
        + [pltpu.VMEM((2 * LANES, GROUP * Q_BLOCK), _BF16)],
        compiler_params=pltpu.CompilerParams(dimension_semantics=("arbitrary",) * 3, vmem_limit_bytes=VMEM_LIMIT_BYTES),
        name="nsa_attention",
    )(qt, gnt, kc, vct, ks, vst, kw, vwt, cmp_end, win_bias)


def _memkv_kernel(mem_ref, w_ref, o_ref):
    o_ref[...] = _dot(mem_ref[...].astype(_BF16), w_ref[...]).astype(_BF16)


def _memkv(mem2, w_xkv):
    m = mem2.shape[0]
    return pl.pallas_call(
        _memkv_kernel, grid=(1,),
        in_specs=[pl.BlockSpec(mem2.shape, lambda i: (0, 0)), pl.BlockSpec(w_xkv.shape, lambda i: (0, 0))],
        out_specs=pl.BlockSpec((m, 2 * D_MODEL), lambda i: (0, 0)),
        out_shape=jax.ShapeDtypeStruct((m, 2 * D_MODEL), _BF16),
        compiler_params=pltpu.CompilerParams(vmem_limit_bytes=VMEM_LIMIT_BYTES),
        name="mem_kv",
    )(mem2, w_xkv)


def _trunk_kernel(a_ref, gb_ref, yb_ref, h_ref, kvm_ref, wo_ref, wxq_ref, wxo_ref, wf1_ref, wf2_ref,
                  g1_ref, b1_ref, g2_ref, b2_ref, g3_ref, b3_ref, o_ref):
    tm = a_ref.shape[0]
    halves = [slice(i * (tm // TRUNK_SPLIT), (i + 1) * (tm // TRUNK_SPLIT)) for i in range(TRUNK_SPLIT)]
    both = lambda fn, *xs: [fn(*(x[i] for x in xs)) for i in range(TRUNK_SPLIT)]

    mix = [(a_ref[r, :] + gb_ref[r, :] * yb_ref[r, :]).astype(_BF16) for r in halves]
    y1 = both(lambda m: _dot(m, wo_ref[...]), mix)
    h1 = [_layer_norm(ALPHA * h_ref[r, :] + y, g1_ref[...], b1_ref[...]) for r, y in zip(halves, y1)]

    qx = both(lambda x: _dot(x.astype(_BF16), wxq_ref[...]).astype(_BF16), h1)
    heads = [[] for _ in range(TRUNK_SPLIT)]
    for hh in range(XATTN_HEADS):
        cs = slice(hh * XATTN_HEAD_DIM, (hh + 1) * XATTN_HEAD_DIM)
        vcs = slice(D_MODEL + hh * XATTN_HEAD_DIM, D_MODEL + (hh + 1) * XATTN_HEAD_DIM)
        s = both(lambda q: _dot_nt(q[:, cs], kvm_ref[:, cs]) * (XATTN_HEAD_DIM ** -0.5), qx)
        e = both(lambda x: jnp.exp(x - jnp.max(x, axis=-1, keepdims=True)), s)
        p = both(lambda x: (x * (1.0 / jnp.sum(x, axis=-1, keepdims=True))).astype(_BF16), e)
        for i, o in enumerate(both(lambda x: _dot(x, kvm_ref[:, vcs]), p)):
            heads[i].append(o)
    xo = both(lambda hs: jnp.concatenate(hs, axis=1).astype(_BF16), heads)
    y2 = both(lambda x: _dot(x, wxo_ref[...]), xo)
    h2 = both(lambda x, y: _layer_norm(ALPHA * x + y, g2_ref[...], b2_ref[...]), h1, y2)

    h2b = both(lambda x: x.astype(_BF16), h2)
    ff = [None] * TRUNK_SPLIT
    for c in range(D_FF // D_MODEL):
        cs = slice(c * D_MODEL, (c + 1) * D_MODEL)
        act = both(lambda x: jnp.square(jnp.maximum(_dot(x, wf1_ref[:, cs]), 0.0)).astype(_BF16), h2b)
        part = both(lambda x: _dot(x, wf2_ref[cs, :]), act)
        ff = part if c == 0 else both(lambda x, y: x + y, ff, part)
    for r, x, y in zip(halves, h2, ff):
        o_ref[r, :] = _layer_norm(ALPHA * x + y, g3_ref[...], b3_ref[...])


def _trunk(a, gb, yb, h, kvm, weights, lns, seq, mem_len):
    n = a.shape[0]
    tm = TRUNK_TILE
    tiles_per_batch = seq // tm
    row = pl.BlockSpec((tm, D_MODEL), lambda i: (i, 0))
    in_specs = ([row, row, row, row, pl.BlockSpec((mem_len, 2 * D_MODEL), lambda i: (i // tiles_per_batch, 0))]
                + [_const_spec(w.shape) for w in weights] + [_const_spec(p.shape) for p in lns])
    return pl.pallas_call(
        _trunk_kernel, grid=(n // tm,), in_specs=in_specs, out_specs=row,
        out_shape=jax.ShapeDtypeStruct((n, D_MODEL), _F32),
        compiler_params=pltpu.CompilerParams(dimension_semantics=("arbitrary",), vmem_limit_bytes=VMEM_LIMIT_BYTES),
        name="trunk",
    )(a, gb, yb, h, kvm, *weights, *lns)


def _rope_constants():
    half = ROT_DIM // 2
    inv = ROPE_THETA ** (-jnp.arange(half, dtype=_F32) / half)
    d = jnp.arange(LANES) % HEAD_DIM
    inv_lane = jnp.where(d < ROT_DIM, inv[d % half], 0.0).astype(_F32)[None, :]
    neg_first = jnp.where(d < half, -1.0, 0.0).astype(_F32)[None, :]
    pos_second = jnp.where((d >= half) & (d < ROT_DIM), 1.0, 0.0).astype(_F32)[None, :]
    return inv_lane, neg_first, pos_second


def kernel(x, mem, positions, ln_in_g, ln_in_b, w_in, gmlp_ln_g, gmlp_ln_b, gmlp_ws, gmlp_bs, cmp_k_pe, cmp_k_w1, cmp_k_b1, cmp_k_w2, cmp_k_b2, cmp_v_pe, cmp_v_w1, cmp_v_b1, cmp_v_w2, cmp_v_b2, w_out, ln1_g, ln1_b, w_xq, w_xkv, w_xo, ln2_g, ln2_b, w_ff1, w_ff2, ln3_g, ln3_b):
    batch, seq, _ = x.shape
    mem_len = mem.shape[1]
    n = batch * seq
    n_slc = seq // SLC_BLOCK
    assert w_in.shape[0] == 1, "one layer"
    assert seq % (TRIP_CHUNKS * SEL_CHUNK) == 0 and seq >= WIN_SPAN and seq & (seq - 1) == 0 and n_slc <= LANES

    rope_consts = _rope_constants()
    vec = lambda p: p.reshape(1, -1)

    wi = w_in[0]
    o_u, o_v, o_q, o_kv, o_gn, o_ga, o_gb = (0, 1024, 2048, 3072, 3072 + 6 * KV_DIM, 3120 + 6 * KV_DIM, 4144 + 6 * KV_DIM)
    w_gn = wi[:, o_gn:o_ga].reshape(D_MODEL, KV_HEADS, 3 * GROUP)
    w_gn = jnp.pad(w_gn, ((0, 0), (0, 0), (0, LANES - 3 * GROUP))).reshape(D_MODEL, KV_HEADS * LANES)
    w_parts = [wi[:, o_u:o_v], wi[:, o_v:o_q], wi[:, o_ga:o_gb], wi[:, o_q:o_kv], wi[:, o_gb:], wi[:, o_kv:o_gn], w_gn]
    w_parts = [w.astype(_BF16) for w in w_parts]

    h, a, gb, qt, kvc, ks, vst, kw, vwt, gnt = _project(
        x.reshape(n, D_MODEL), positions.reshape(n, 1), vec(ln_in_g), vec(ln_in_b), w_parts,
        vec(gmlp_ln_g[0]), vec(gmlp_ln_b[0]), gmlp_ws[0], gmlp_bs[0].T, rope_consts, seq)

    rows = seq // CMP_STRIDE
    flat = CMP_STRIDE * HEAD_DIM
    pos_end = jnp.pad(positions[:, CMP_BLOCK - 1::CMP_STRIDE], ((0, 0), (0, 1)))[:, :, None]
    pad_lanes = lambda w: jnp.pad(w, ((0, 0), (0, LANES - HEAD_DIM)))
    kvcmp, kvcmp_t = _compress(
        kvc.reshape(2 * KV_HEADS, batch, rows, flat), pos_end,
        jnp.stack([cmp_k_pe[0].reshape(2, flat), cmp_v_pe[0].reshape(2, flat)]),
        jnp.stack([cmp_k_w1[0], cmp_v_w1[0]]).astype(_BF16),
        jnp.stack([vec(cmp_k_b1[0]), vec(cmp_v_b1[0])]),
        jnp.stack([pad_lanes(cmp_k_w2[0]), pad_lanes(cmp_v_w2[0])]).astype(_BF16),
        jnp.stack([pad_lanes(vec(cmp_k_b2[0])), pad_lanes(vec(cmp_v_b2[0]))]),
        rope_consts, batch)

    yb = _nsa(qt, gnt, kvcmp, kvcmp_t, ks, vst, kw, vwt, batch, seq)

    kvm = _memkv(mem.reshape(batch * mem_len, D_MODEL), w_xkv[0].astype(_BF16))
    weights = [w.astype(_BF16) for w in (w_out[0], w_xq[0], w_xo[0], w_ff1[0], w_ff2[0])]
    lns = [vec(p[0]) for p in (ln1_g, ln1_b, ln2_g, ln2_b, ln3_g, ln3_b)]
    out = _trunk(a, gb, yb, h, kvm, weights, lns, seq, mem_len)
    return out.reshape(batch, seq, D_MODEL)
```

```python
import functools
import math

import jax
import jax.numpy as jnp
from jax import lax
from jax.experimental import pallas as pl
from jax.experimental.pallas import tpu as pltpu

D_MODEL = 1024
LN_EPS = 1e-5
ALPHA = 2.0 ** 0.25
ROPE_THETA = 500000.0
GMLP_GROUPS = 8
GMLP_CHUNK = 128
NSA_HEADS = 16
HEAD_DIM = 64
KV_HEADS = 4
GROUP = NSA_HEADS // KV_HEADS
KV_DIM = KV_HEADS * HEAD_DIM
ROT_DIM = HEAD_DIM // 4
CMP_BLOCK = 32
CMP_STRIDE = 16
CMP_HIDDEN = 4 * HEAD_DIM
SLC_BLOCK = 64
N_SELECT = 16
WINDOW = 512
Q_BLOCK = 128
KV_CHUNK = 128
V_ROWS = 80
XATTN_HEADS = 4
XATTN_HEAD_DIM = D_MODEL // XATTN_HEADS
D_FF = 4 * D_MODEL

LANES = 128
PROJ_TILE = 512
PROJ_SPLIT = 2
TRUNK_TILE = 512
TRUNK_SPLIT = 2
SEL_CHUNK = 512
TRIP_CHUNKS = (8, 4, 2)
HEADS_PER_STEP = 1
WIN_SPAN = WINDOW + Q_BLOCK
WIN_PIECES = 5
MASK_VALUE = -1e30
LOG2_E = math.log2(math.e)
VMEM_LIMIT_BYTES = 56 * 1024 * 1024

_F32 = jnp.float32
_BF16 = jnp.bfloat16


def _layer_norm(x, g, b):
    mu = jnp.mean(x, axis=-1, keepdims=True)
    xc = x - mu
    var = jnp.mean(xc * xc, axis=-1, keepdims=True)
    return xc * lax.rsqrt(var + LN_EPS) * g + b


def _dot(a, b):
    return jnp.dot(a, b, preferred_element_type=_F32)


def _dot_nt(a, b):
    return lax.dot_general(a, b, (((1,), (1,)), ((), ())), preferred_element_type=_F32)


def _rope_tables(pos_f32, inv_lane, neg_first, pos_second):
    ang = pos_f32 * inv_lane
    c = jnp.cos(ang)
    s = jnp.sin(ang)
    return c, s * neg_first, s * pos_second


def _rope_block(x, tables):
    c, s_first, s_second = tables
    half = ROT_DIM // 2
    up = pltpu.roll(x, LANES - half, axis=1)
    down = pltpu.roll(x, half, axis=1)
    return x * c + up * s_first + down * s_second


def _const_spec(shape):
    nd = len(shape)
    return pl.BlockSpec(shape, lambda *_: (0,) * nd, pipeline_mode=pl.Buffered(1))


def _proj_kernel(x_ref, pos_ref, lng_ref, lnb_ref, wu_ref, wv_ref, wga_ref, wq_ref, wgb_ref, wkv_ref, wgn_ref,
                 glng_ref, glnb_ref, ws_ref, bst_ref, inv_ref, m1_ref, m2_ref,
                 h_ref, a_ref, gb_ref, qt_ref, kvc_ref, ks_ref, vst_ref, kw_ref, vwt_ref, gnt_ref, cmp_ref, *, seq):
    tm = x_ref.shape[0]
    part = tm // PROJ_SPLIT
    ti = lax.broadcasted_iota(jnp.int32, (GMLP_CHUNK, GMLP_CHUNK), 0)
    si = lax.broadcasted_iota(jnp.int32, (GMLP_CHUNK, GMLP_CHUNK), 1)
    w_tril = [jnp.where(si <= ti, ws_ref[g], 0.0).astype(_BF16) for g in range(GMLP_GROUPS)]
    bst = bst_ref[...]
    q_scale = HEAD_DIM ** -0.5 * LOG2_E
    heads_per_slab = LANES // HEAD_DIM
    blk_lane = lax.broadcasted_iota(jnp.int32, (part, LANES), 1)
    ones_col = (lax.broadcasted_iota(jnp.int32, (part, HEAD_DIM), 1) == 0).astype(_F32)

    def stages(r0):
        rs = slice(r0, r0 + part)
        h = _layer_norm(x_ref[rs, :], lng_ref[...], lnb_ref[...])
        h_ref[rs, :] = h
        hb = h.astype(_BF16)
        zu = _dot(hb, wu_ref[...])
        yield
        u_act = jax.nn.gelu(zu)
        zv = _dot(hb, wv_ref[...])
        yield
        vn = _layer_norm(jax.nn.gelu(zv), glng_ref[...], glnb_ref[...]).astype(_BF16)
        zga = _dot(hb, wga_ref[...])
        yield
        ua = u_act * jax.nn.sigmoid(zga)
        for g in range(GMLP_GROUPS):
            cs = slice(g * LANES, (g + 1) * LANES)
            for c in range(part // GMLP_CHUNK):
                cr = slice(c * GMLP_CHUNK, (c + 1) * GMLP_CHUNK)
                mixed = _dot(w_tril[g], vn[cr, cs]) + bst[:, g:g + 1]
                a_ref[r0 + c * GMLP_CHUNK:r0 + (c + 1) * GMLP_CHUNK, cs] = ua[cr, cs] * mixed
        zgb = _dot(hb, wgb_ref[...])
        yield
        gb_ref[rs, :] = jax.nn.sigmoid(zgb)
        zgn = _dot(hb, wgn_ref[...])
        yield
        gn = jax.nn.sigmoid(zgn)
        for hh in range(KV_HEADS):
            gnt_ref[hh * LANES:(hh + 1) * LANES, rs] = gn[:, hh * LANES:(hh + 1) * LANES].T
        zq = _dot(hb, wq_ref[...])
        yield
        tables = _rope_tables(pos_ref[rs, :].astype(_F32), inv_ref[...], m1_ref[...], m2_ref[...])
        for i in range(D_MODEL // LANES):
            cs = slice(i * LANES, (i + 1) * LANES)
            qt_ref[cs, rs] = (_rope_block(zq[:, cs], tables) * q_scale).T.astype(_BF16)
        zkv = _dot(hb, wkv_ref[...])
        yield
        for sl in range(2 * KV_DIM // LANES):
            cmp_ref[sl, rs, :] = zkv[:, sl * LANES:(sl + 1) * LANES]
        out_rows = slice(r0 // CMP_STRIDE, (r0 + part) // CMP_STRIDE)
        for sl in range(2 * KV_DIM // LANES):
            for l in range(CMP_STRIDE):
                pair = cmp_ref[sl, pl.ds(r0 + l, part // CMP_STRIDE, stride=CMP_STRIDE), :]
                for hh in range(heads_per_slab):
                    kvc_ref[sl * heads_per_slab + hh, out_rows, l * HEAD_DIM:(l + 1) * HEAD_DIM] = (
                        pair[:, hh * HEAD_DIM:(hh + 1) * HEAD_DIM])
        seq_pos = (pl.program_id(0) * tm + r0 + lax.broadcasted_iota(jnp.int32, (part, LANES), 0)) & (seq - 1)
        blk_onehot = jnp.where(lax.shift_right_logical(seq_pos, SLC_BLOCK.bit_length() - 1) == blk_lane, 1.0, 0.0)
        k_tail = jnp.concatenate([jnp.zeros((part, HEAD_DIM), _BF16), blk_onehot.astype(_BF16)], axis=1)
        for base, k_out, vt_out in ((2 * KV_DIM, ks_ref, vst_ref), (4 * KV_DIM, kw_ref, vwt_ref)):
            for i in range(KV_DIM // LANES):
                kr = _rope_block(zkv[:, base + i * LANES: base + (i + 1) * LANES], tables).astype(_BF16)
                for hh in range(heads_per_slab):
                    k_head = kr[:, hh * HEAD_DIM:(hh + 1) * HEAD_DIM]
                    if k_out is ks_ref:
                        k_out[2 * i + hh, rs, :] = jnp.concatenate([k_head, k_tail], axis=1)
                    else:
                        k_out[2 * i + hh, rs, :] = k_head
            for hh in range(KV_HEADS):
                v = zkv[:, base + KV_DIM + hh * HEAD_DIM: base + KV_DIM + (hh + 1) * HEAD_DIM]
                v_ext = jnp.concatenate([v, ones_col], axis=1)
                for c in range(part // KV_CHUNK):
                    vt_out[hh, r0 // KV_CHUNK + c] = v_ext[c * KV_CHUNK:(c + 1) * KV_CHUNK, :].T[:V_ROWS].astype(_BF16)
        yield

    for _ in zip(*[stages(i * part) for i in range(PROJ_SPLIT)]):
        pass


def _project(x2, pos2, ln_g, ln_b, w_parts, gln_g, gln_b, ws, bst, rope_consts, seq):
    n = x2.shape[0]
    tm = PROJ_TILE
    row = lambda shape: pl.BlockSpec(shape, lambda i: (i, 0))
    col = lambda shape: pl.BlockSpec(shape, lambda i: (0, i))
    head_rows = lambda nh, r, w: pl.BlockSpec((nh, r, w), lambda i: (0, i, 0))
    vt_spec = pl.BlockSpec((KV_HEADS, tm // KV_CHUNK, V_ROWS, KV_CHUNK), lambda i: (0, i, 0, 0))
    w_specs = [_const_spec(w.shape) for w in w_parts]
    in_specs = ([row((tm, D_MODEL)), row((tm, 1)), _const_spec(ln_g.shape), _const_spec(ln_b.shape)] + w_specs
                + [_const_spec(gln_g.shape), _const_spec(gln_b.shape), _const_spec(ws.shape), _const_spec(bst.shape)]
                + [_const_spec(c.shape) for c in rope_consts])
    out_shape = (
        jax.ShapeDtypeStruct((n, D_MODEL), _F32),
        jax.ShapeDtypeStruct((n, D_MODEL), _F32),
        jax.ShapeDtypeStruct((n, D_MODEL), _F32),
        jax.ShapeDtypeStruct((D_MODEL, n), _BF16),
        jax.ShapeDtypeStruct((2 * KV_HEADS, n // CMP_STRIDE, CMP_STRIDE * HEAD_DIM), _F32),
        jax.ShapeDtypeStruct((KV_HEADS, n, 2 * LANES), _BF16),
        jax.ShapeDtypeStruct((KV_HEADS, n // KV_CHUNK, V_ROWS, KV_CHUNK), _BF16),
        jax.ShapeDtypeStruct((KV_HEADS, n, HEAD_DIM), _BF16),
        jax.ShapeDtypeStruct((KV_HEADS, n // KV_CHUNK, V_ROWS, KV_CHUNK), _BF16),
        jax.ShapeDtypeStruct((KV_HEADS * LANES, n), _F32),
    )
    out_specs = (row((tm, D_MODEL)), row((tm, D_MODEL)), row((tm, D_MODEL)), col((D_MODEL, tm)),
                 head_rows(2 * KV_HEADS, tm // CMP_STRIDE, CMP_STRIDE * HEAD_DIM),
                 head_rows(KV_HEADS, tm, 2 * LANES), vt_spec, head_rows(KV_HEADS, tm, HEAD_DIM), vt_spec,
                 col((KV_HEADS * LANES, tm)))
    return pl.pallas_call(
        functools.partial(_proj_kernel, seq=seq), grid=(n // tm,), in_specs=in_specs,
        out_specs=out_specs, out_shape=out_shape,
        scratch_shapes=[pltpu.VMEM((2 * KV_DIM // LANES, tm, LANES), _F32)],
        compiler_params=pltpu.CompilerParams(dimension_semantics=("arbitrary",), vmem_limit_bytes=VMEM_LIMIT_BYTES),
        name="nsa_proj",
    )(x2, pos2, ln_g, ln_b, *w_parts, gln_g, gln_b, ws, bst, *rope_consts)


def _compress_kernel(x_ref, pos_ref, pe_ref, w1_ref, b1_ref, w2_ref, b2_ref, inv_ref, m1_ref, m2_ref,
                     o_ref, ot_ref, perm_ref):
    rows = x_ref.shape[0]
    is_key = (pl.program_id(1) == 0).astype(_F32)
    x = x_ref[...]
    half = CMP_STRIDE * HEAD_DIM
    ya = _dot((x + pe_ref[0:1, :]).astype(_BF16), w1_ref[0:half, :])
    yb = _dot((x + pe_ref[1:2, :]).astype(_BF16), w1_ref[half:2 * half, :])
    hid = jax.nn.gelu(ya + pltpu.roll(yb, rows - 1, axis=0) + b1_ref[...])
    out = _dot(hid.astype(_BF16), w2_ref[...]) + b2_ref[...]
    tables = _rope_tables(pos_ref[...].astype(_F32) * is_key, inv_ref[...], m1_ref[...], m2_ref[...])
    perm_ref[...] = _rope_block(out, tables)
    groups = rows // 4
    for r in range(4):
        part = perm_ref[pl.ds(r, groups, stride=4), :]
        o_ref[r * groups:(r + 1) * groups, :] = part[:, :HEAD_DIM].astype(_BF16)
        ot_ref[:, r * groups:(r + 1) * groups] = part.T[:HEAD_DIM, :].astype(_BF16)


def _compress(kvc, pos_end, pe, w1, b1, w2, b2, rope_consts, batch):
    rows = kvc.shape[2]
    flat = CMP_STRIDE * HEAD_DIM
    in_specs = [
        pl.BlockSpec((None, None, rows, flat), lambda b, kv, hh: (kv * KV_HEADS + hh, b, 0, 0)),
        pl.BlockSpec((None, rows, 1), lambda b, kv, hh: (b, 0, 0)),
        pl.BlockSpec((None, 2, flat), lambda b, kv, hh: (kv, 0, 0)),
        pl.BlockSpec((None, 2 * flat, CMP_HIDDEN), lambda b, kv, hh: (kv, 0, 0)),
        pl.BlockSpec((None, 1, CMP_HIDDEN), lambda b, kv, hh: (kv, 0, 0)),
        pl.BlockSpec((None, CMP_HIDDEN, LANES), lambda b, kv, hh: (kv, 0, 0)),
        pl.BlockSpec((None, 1, LANES), lambda b, kv, hh: (kv, 0, 0)),
    ] + [pl.BlockSpec(c.shape, lambda b, kv, hh: (0, 0)) for c in rope_consts]
    return pl.pallas_call(
        _compress_kernel, grid=(batch, 2, KV_HEADS), in_specs=in_specs,
        out_specs=(pl.BlockSpec((None, None, None, rows, HEAD_DIM), lambda b, kv, hh: (kv, b, hh, 0, 0)),
                   pl.BlockSpec((None, None, None, HEAD_DIM, rows), lambda b, kv, hh: (kv, b, hh, 0, 0))),
        out_shape=(jax.ShapeDtypeStruct((2, batch, KV_HEADS, rows, HEAD_DIM), _BF16),
                   jax.ShapeDtypeStruct((2, batch, KV_HEADS, HEAD_DIM, rows), _BF16)),
        scratch_shapes=[pltpu.VMEM((rows, LANES), _F32)],
        compiler_params=pltpu.CompilerParams(dimension_semantics=("arbitrary",) * 3, vmem_limit_bytes=VMEM_LIMIT_BYTES),
        name="nsa_compress",
    )(kvc, pos_end, pe, w1, b1, w2, b2, *rope_consts)


def _per_head(fn, s, *shared):
    return jnp.concatenate([fn(s[:, g * Q_BLOCK:(g + 1) * Q_BLOCK], *shared) for g in range(GROUP)], axis=1)


def _nsa_kernel(qt_ref, gnt_ref, kc_ref, vct_ref, ks_ref, vst_ref, kw_ref, vwt_ref, cend_ref, wbias_ref,
                o_ref, sa_ref, sb_ref, qx_ref,
                *, n_sel):
    rows = kc_ref.shape[1]
    n_slc = rows // 4
    seq = ks_ref.shape[1]
    heads = range(HEADS_PER_STEP)
    qb = pl.program_id(2)
    q0 = qb * Q_BLOCK
    t = q0 + lax.broadcasted_iota(jnp.int32, (1, Q_BLOCK), 1)
    w0 = pl.multiple_of(jnp.maximum(q0 - WINDOW, 0), KV_CHUNK)
    wc = lax.shift_right_logical(w0, KV_CHUNK.bit_length() - 1)
    n_seq_chunks = seq // SEL_CHUNK
    sub = SEL_CHUNK // KV_CHUNK
    piece = WIN_SPAN // WIN_PIECES

    valid_c = cend_ref[...] <= t
    has_c = t >= CMP_BLOCK - 1
    blk = lax.broadcasted_iota(jnp.int32, (n_slc, 1), 0)
    cur = lax.shift_right_logical(t, SLC_BLOCK.bit_length() - 1)
    forced = (blk == 0) | (blk == cur) | (blk == cur - 1)
    bi = lax.broadcasted_iota(jnp.int32, (n_slc, Q_BLOCK), 0)
    bf = bi.astype(_F32)
    past = bi < lax.shift_right_logical(q0, SLC_BLOCK.bit_length() - 1)
    bias_w = wbias_ref[jnp.minimum(qb, WINDOW // Q_BLOCK)]
    causal = jnp.where(lax.broadcasted_iota(jnp.int32, (Q_BLOCK, Q_BLOCK), 0)
                       <= lax.broadcasted_iota(jnp.int32, (Q_BLOCK, Q_BLOCK), 1), 0.0, MASK_VALUE)
    out = [None] * HEADS_PER_STEP

    def before_loop(h):
        qt = qt_ref[h * GROUP * HEAD_DIM:(h + 1) * GROUP * HEAD_DIM, :]
        q_t = jnp.concatenate([qt[g * HEAD_DIM:(g + 1) * HEAD_DIM, :] for g in range(GROUP)], axis=1)
        raw_c = _dot(kc_ref[h], q_t)
        raw_w = _dot(kw_ref[h, pl.ds(w0, WIN_SPAN), :], q_t)
        raw_d = _dot(ks_ref[h, pl.ds(q0, Q_BLOCK), :HEAD_DIM], q_t)
        raw_0 = _dot(ks_ref[h, 0:SEL_CHUNK, :HEAD_DIM], q_t)
        yield

        s_c = _per_head(lambda s, v: jnp.where(v, s, MASK_VALUE), raw_c, valid_c)
        e_c = jnp.exp2(s_c - jnp.max(s_c, axis=0, keepdims=True))
        inv_c = _per_head(lambda l, ok: jnp.where(ok, 1.0 / l, 0.0), jnp.sum(e_c, axis=0, keepdims=True), has_c)
        p_c = e_c * inv_c
        o_c = _dot(vct_ref[h], p_c.astype(_BF16))
        yield

        p_sum = p_c[:, :Q_BLOCK]
        for g in range(1, GROUP):
            p_sum = p_sum + p_c[:, g * Q_BLOCK:(g + 1) * Q_BLOCK]
        parts = [p_sum[r * n_slc:(r + 1) * n_slc, :] for r in range(4)]
        prev3 = jnp.where(blk == 0, 0.0, pltpu.roll(parts[3], 1, axis=0))
        p_slc = parts[0] + parts[1] + parts[2] + parts[3] + prev3

        st = jnp.where(forced, -jnp.inf, jnp.where(blk <= cur, p_slc, -1.0))
        s_w = _per_head(lambda s, b: s + b, raw_w, bias_w)
        m_w = jnp.max(s_w, axis=0, keepdims=True)
        s_d = _per_head(lambda s, b: s + b, raw_d, causal)
        m_d = jnp.max(s_d, axis=0, keepdims=True)

        p_w_parts = []
        for r in range(n_sel - 3):
            mx = jnp.max(st, axis=0, keepdims=True)
            first = jnp.min(jnp.where(st == mx, bf, float(n_slc)), axis=0, keepdims=True)
            if r < WIN_PIECES:
                part = jnp.exp2(s_w[r * piece:(r + 1) * piece, :] - m_w)
                p_w_parts.append(part.astype(_BF16))
                col_max = jnp.max(part, axis=0, keepdims=True)
                tie = col_max[:, :Q_BLOCK]
                for g in range(1, GROUP):
                    tie = jnp.maximum(tie, col_max[:, g * Q_BLOCK:(g + 1) * Q_BLOCK])
                first = first + tie * 0.0
            st = jnp.where(bf == first, -jnp.inf, st)
            yield
        p_d = jnp.exp2(s_d - m_d).astype(_BF16)
        sel = st == -jnp.inf
        pen = jnp.where(sel & past, 0.0, MASK_VALUE)
        if n_slc < LANES:
            pen = jnp.concatenate([pen, jnp.zeros((LANES - n_slc, Q_BLOCK), _F32)], axis=0)
        pen_heads = jnp.concatenate([pen] * GROUP, axis=1)
        for b in range(SEL_CHUNK // SLC_BLOCK):
            ks_rows = slice(b * SLC_BLOCK, (b + 1) * SLC_BLOCK)
            sa_ref[h, ks_rows, :] = raw_0[ks_rows, :] + pen_heads[b:b + 1, :]
        qx_ref[h, 0:HEAD_DIM, :] = q_t
        qx_ref[h, HEAD_DIM:LANES, :] = jnp.zeros((LANES - HEAD_DIM, GROUP * Q_BLOCK), _BF16)
        qx_ref[h, LANES:2 * LANES, :] = jnp.concatenate([pen.astype(_BF16)] * GROUP, axis=1)
        yield

        v_w = jnp.concatenate([vwt_ref[h, wc + i] for i in range(WIN_SPAN // KV_CHUNK)], axis=1)
        acc_w = _dot(v_w, jnp.concatenate(p_w_parts, axis=0))
        acc_d = _dot(vst_ref[h, qb], p_d)
        yield
        o_w = acc_w[:HEAD_DIM] * (1.0 / acc_w[HEAD_DIM:HEAD_DIM + 1])
        out[h] = (o_c, o_w, m_d, acc_d)
        yield

    for _ in zip(*[before_loop(h) for h in heads]):
        pass

    def sel_scores(h, c, buf):
        k0 = pl.multiple_of(jnp.minimum(c, n_seq_chunks - 1) * SEL_CHUNK, SEL_CHUNK)
        buf[h] = _dot(ks_ref[h, pl.ds(k0, SEL_CHUNK), :], qx_ref[h])

    def sel_consume(h, c, buf, m, acc):
        c0 = jnp.minimum(c, n_seq_chunks - 1) * sub
        s = buf[h]
        m_new = jnp.maximum(m, jnp.max(s, axis=0, keepdims=True))
        p = jnp.exp2(s - m_new)
        v_t = jnp.concatenate([vst_ref[h, c0 + i] for i in range(sub)], axis=1)
        acc = jnp.exp2(m - m_new) * acc + _dot(v_t, p.astype(_BF16))
        return m_new, acc

    bufs = (sa_ref, sb_ref)

    def sel_trip(per_trip, base):
        def body(i, carry):
            carry = list(carry)
            for j in range(per_trip):
                c = base + per_trip * i + j
                for h in heads:
                    sel_scores(h, c + 1, bufs[(j + 1) % 2])
                for h in heads:
                    carry[h] = sel_consume(h, c, bufs[j % 2], *carry[h])
            return tuple(carry)
        return body

    n_chunks = (q0 + SEL_CHUNK - 1) // SEL_CHUNK
    carry, done = tuple((out[h][2], out[h][3]) for h in heads), 0
    for per_trip in TRIP_CHUNKS:
        left = n_chunks - done
        trips = (left + 1) // 2 if per_trip == TRIP_CHUNKS[-1] else left // per_trip
        carry = lax.fori_loop(0, trips, sel_trip(per_trip, done), carry)
        done = done + trips * per_trip

    for h in heads:
        o_c, o_w = out[h][0], out[h][1]
        acc_s = carry[h][1]
        o_s = acc_s[:HEAD_DIM] * (1.0 / acc_s[HEAD_DIM:HEAD_DIM + 1])
        gates = gnt_ref[h * LANES:(h + 1) * LANES, :]
        outs = []
        for g in range(GROUP):
            ls = slice(g * Q_BLOCK, (g + 1) * Q_BLOCK)
            outs.append(gates[3 * g:3 * g + 1] * o_c[:, ls] + gates[3 * g + 1:3 * g + 2] * o_s[:, ls]
                        + gates[3 * g + 2:3 * g + 3] * o_w[:, ls])
        o_ref[:, h * GROUP * HEAD_DIM:(h + 1) * GROUP * HEAD_DIM] = jnp.concatenate(outs, axis=0).T


def _nsa(qt, gnt, kc, vct, ks, vst, kw, vwt, batch, seq):
    n = qt.shape[1]
    n_qb = seq // Q_BLOCK
    rows = kc.shape[3]
    n_slc = seq // SLC_BLOCK
    hps = HEADS_PER_STEP
    qcol = lambda r: pl.BlockSpec((hps * r, Q_BLOCK), lambda b, hp, i: (hp, b * n_qb + i))
    seq_rows = lambda w: pl.BlockSpec((hps, seq, w), lambda b, hp, i: (hp, b, 0))
    vt_spec = pl.BlockSpec((hps, seq // KV_CHUNK, V_ROWS, KV_CHUNK), lambda b, hp, i: (hp, b, 0, 0))
    slot = jnp.arange(rows)
    cmp_idx = 4 * (slot % n_slc) + slot // n_slc
    cmp_end = jnp.where(cmp_idx < rows - 1, CMP_STRIDE * cmp_idx + CMP_BLOCK - 1, jnp.iinfo(jnp.int32).max)
    cmp_end = jnp.broadcast_to(cmp_end.astype(jnp.int32)[:, None], (rows, Q_BLOCK))
    off = Q_BLOCK * jnp.arange(WINDOW // Q_BLOCK + 1)[:, None, None]
    diff = off + jnp.arange(Q_BLOCK)[None, None, :] - jnp.arange(WIN_SPAN)[None, :, None]
    win_bias = jnp.where((diff >= 0) & (diff < WINDOW), 0.0, MASK_VALUE).astype(_F32)
    in_specs = [
        qcol(GROUP * HEAD_DIM), qcol(LANES),
        pl.BlockSpec((None, None, hps, rows, HEAD_DIM), lambda b, hp, i: (0, b, hp, 0, 0)),
        pl.BlockSpec((None, None, hps, HEAD_DIM, rows), lambda b, hp, i: (1, b, hp, 0, 0)),
        seq_rows(2 * LANES), vt_spec, seq_rows(HEAD_DIM), vt_spec,
        _const_spec(cmp_end.shape), _const_spec(win_bias.shape),
    ]
    return pl.pallas_call(
        functools.partial(_nsa_kernel, n_sel=min(N_SELECT, n_slc)),
        grid=(batch, KV_HEADS // hps, n_qb), in_specs=in_specs,
        out_specs=pl.BlockSpec((Q_BLOCK, hps * GROUP * HEAD_DIM), lambda b, hp, i: (b * n_qb + i, hp)),
        out_shape=jax.ShapeDtypeStruct((n, D_MODEL), _F32),
        scratch_shapes=[pltpu.VMEM((hps, SEL_CHUNK, GROUP * Q_BLOCK), _F32)] * 2
        + [pltpu.VMEM((hps, 2 * LANES, GROUP * Q_BLOCK), _BF16)],
        compiler_params=pltpu.CompilerParams(dimension_semantics=("arbitrary",) * 3, vmem_limit_bytes=VMEM_LIMIT_BYTES),
        name="nsa_attention",
    )(qt, gnt, kc, vct, ks, vst, kw, vwt, cmp_end, win_bias)


def _memkv_kernel(mem_ref, w_ref, o_ref):
    o_ref[...] = _dot(mem_ref[...].astype(_BF16), w_ref[...]).astype(_BF16)


def _memkv(mem2, w_xkv):
    m = mem2.shape[0]
    return pl.pallas_call(
        _memkv_kernel, grid=(1,),
        in_specs=[pl.BlockSpec(mem2.shape, lambda i: (0, 0)), pl.BlockSpec(w_xkv.shape, lambda i: (0, 0))],
        out_specs=pl.BlockSpec((m, 2 * D_MODEL), lambda i: (0, 0)),
        out_shape=jax.ShapeDtypeStruct((m, 2 * D_MODEL), _BF16),
        compiler_params=pltpu.CompilerParams(vmem_limit_bytes=VMEM_LIMIT_BYTES),
        name="mem_kv",
    )(mem2, w_xkv)


def _trunk_kernel(a_ref, gb_ref, yb_ref, h_ref, kvm_ref, wo_ref, wxq_ref, wxo_ref, wf1_ref, wf2_ref,
                  g1_ref, b1_ref, g2_ref, b2_ref, g3_ref, b3_ref, o_ref):
    tm = a_ref.shape[0]
    halves = [slice(i * (tm // TRUNK_SPLIT), (i + 1) * (tm // TRUNK_SPLIT)) for i in range(TRUNK_SPLIT)]
    both = lambda fn, *xs: [fn(*(x[i] for x in xs)) for i in range(TRUNK_SPLIT)]

    mix = [(a_ref[r, :] + gb_ref[r, :] * yb_ref[r, :]).astype(_BF16) for r in halves]
    y1 = both(lambda m: _dot(m, wo_ref[...]), mix)
    h1 = [_layer_norm(ALPHA * h_ref[r, :] + y, g1_ref[...], b1_ref[...]) for r, y in zip(halves, y1)]

    qx = both(lambda x: _dot(x.astype(_BF16), wxq_ref[...]).astype(_BF16), h1)
    heads = [[] for _ in range(TRUNK_SPLIT)]
    for hh in range(XATTN_HEADS):
        cs = slice(hh * XATTN_HEAD_DIM, (hh + 1) * XATTN_HEAD_DIM)
        vcs = slice(D_MODEL + hh * XATTN_HEAD_DIM, D_MODEL + (hh + 1) * XATTN_HEAD_DIM)
        s = both(lambda q: _dot_nt(q[:, cs], kvm_ref[:, cs]) * (XATTN_HEAD_DIM ** -0.5), qx)
        e = both(lambda x: jnp.exp(x - jnp.max(x, axis=-1, keepdims=True)), s)
        p = both(lambda x: (x * (1.0 / jnp.sum(x, axis=-1, keepdims=True))).astype(_BF16), e)
        for i, o in enumerate(both(lambda x: _dot(x, kvm_ref[:, vcs]), p)):
            heads[i].append(o)
    xo = both(lambda hs: jnp.concatenate(hs, axis=1).astype(_BF16), heads)
    y2 = both(lambda x: _dot(x, wxo_ref[...]), xo)
    h2 = both(lambda x, y: _layer_norm(ALPHA * x + y, g2_ref[...], b2_ref[...]), h1, y2)

    h2b = both(lambda x: x.astype(_BF16), h2)
    ff = [None] * TRUNK_SPLIT
    for c in range(D_FF // D_MODEL):
        cs = slice(c * D_MODEL, (c + 1) * D_MODEL)
        act = both(lambda x: jnp.square(jnp.maximum(_dot(x, wf1_ref[:, cs]), 0.0)).astype(_BF16), h2b)
        part = both(lambda x: _dot(x, wf2_ref[cs, :]), act)
        ff = part if c == 0 else both(lambda x, y: x + y, ff, part)
    for r, x, y in zip(halves, h2, ff):
        o_ref[r, :] = _layer_norm(ALPHA * x + y, g3_ref[...], b3_ref[...])


def _trunk(a, gb, yb, h, kvm, weights, lns, seq, mem_len):
    n = a.shape[0]
    tm = TRUNK_TILE
    tiles_per_batch = seq // tm
    row = pl.BlockSpec((tm, D_MODEL), lambda i: (i, 0))
    in_specs = ([row, row, row, row, pl.BlockSpec((mem_len, 2 * D_MODEL), lambda i: (i // tiles_per_batch, 0))]
                + [_const_spec(w.shape) for w in weights] + [_const_spec(p.shape) for p in lns])
    return pl.pallas_call(
        _trunk_kernel, grid=(n // tm,), in_specs=in_specs, out_specs=row,
        out_shape=jax.ShapeDtypeStruct((n, D_MODEL), _F32),
        compiler_params=pltpu.CompilerParams(dimension_semantics=("arbitrary",), vmem_limit_bytes=VMEM_LIMIT_BYTES),
        name="trunk",
    )(a, gb, yb, h, kvm, *weights, *lns)


def _rope_constants():
    half = ROT_DIM // 2
    inv = ROPE_THETA ** (-jnp.arange(half, dtype=_F32) / half)
    d = jnp.arange(LANES) % HEAD_DIM
    inv_lane = jnp.where(d < ROT_DIM, inv[d % half], 0.0).astype(_F32)[None, :]
    neg_first = jnp.where(d < half, -1.0, 0.0).astype(_F32)[None, :]
    pos_second = jnp.where((d >= half) & (d < ROT_DIM), 1.0, 0.0).astype(_F32)[None, :]
    return inv_lane, neg_first, pos_second


def kernel(x, mem, positions, ln_in_g, ln_in_b, w_in, gmlp_ln_g, gmlp_ln_b, gmlp_ws, gmlp_bs, cmp_k_pe, cmp_k_w1, cmp_k_b1, cmp_k_w2, cmp_k_b2, cmp_v_pe, cmp_v_w1, cmp_v_b1, cmp_v_w2, cmp_v_b2, w_out, ln1_g, ln1_b, w_xq, w_xkv, w_xo, ln2_g, ln2_b, w_ff1, w_ff2, ln3_g, ln3_b):
    batch, seq, _ = x.shape
    mem_len = mem.shape[1]
    n = batch * seq
    n_slc = seq // SLC_BLOCK
    assert w_in.shape[0] == 1, "one layer"
    assert seq % SEL_CHUNK == 0 and TRIP_CHUNKS[-1] == 2 and seq >= WIN_SPAN and seq & (seq - 1) == 0 and n_slc <= LANES

    rope_consts = _rope_constants()
    vec = lambda p: p.reshape(1, -1)

    wi = w_in[0]
    o_u, o_v, o_q, o_kv, o_gn, o_ga, o_gb = (0, 1024, 2048, 3072, 3072 + 6 * KV_DIM, 3120 + 6 * KV_DIM, 4144 + 6 * KV_DIM)
    w_gn = wi[:, o_gn:o_ga].reshape(D_MODEL, KV_HEADS, 3 * GROUP)
    w_gn = jnp.pad(w_gn, ((0, 0), (0, 0), (0, LANES - 3 * GROUP))).reshape(D_MODEL, KV_HEADS * LANES)
    w_parts = [wi[:, o_u:o_v], wi[:, o_v:o_q], wi[:, o_ga:o_gb], wi[:, o_q:o_kv], wi[:, o_gb:], wi[:, o_kv:o_gn], w_gn]
    w_parts = [w.astype(_BF16) for w in w_parts]

    h, a, gb, qt, kvc, ks, vst, kw, vwt, gnt = _project(
        x.reshape(n, D_MODEL), positions.reshape(n, 1), vec(ln_in_g), vec(ln_in_b), w_parts,
        vec(gmlp_ln_g[0]), vec(gmlp_ln_b[0]), gmlp_ws[0], gmlp_bs[0].T, rope_consts, seq)

    rows = seq // CMP_STRIDE
    flat = CMP_STRIDE * HEAD_DIM
    pos_end = jnp.pad(positions[:, CMP_BLOCK - 1::CMP_STRIDE], ((0, 0), (0, 1)))[:, :, None]
    pad_lanes = lambda w: jnp.pad(w, ((0, 0), (0, LANES - HEAD_DIM)))
    kvcmp, kvcmp_t = _compress(
        kvc.reshape(2 * KV_HEADS, batch, rows, flat), pos_end,
        jnp.stack([cmp_k_pe[0].reshape(2, flat), cmp_v_pe[0].reshape(2, flat)]),
        jnp.stack([cmp_k_w1[0], cmp_v_w1[0]]).astype(_BF16),
        jnp.stack([vec(cmp_k_b1[0]), vec(cmp_v_b1[0])]),
        jnp.stack([pad_lanes(cmp_k_w2[0]), pad_lanes(cmp_v_w2[0])]).astype(_BF16),
        jnp.stack([pad_lanes(vec(cmp_k_b2[0])), pad_lanes(vec(cmp_v_b2[0]))]),
        rope_consts, batch)

    yb = _nsa(qt, gnt, kvcmp, kvcmp_t, ks, vst, kw, vwt, batch, seq)

    kvm = _memkv(mem.reshape(batch * mem_len, D_MODEL), w_xkv[0].astype(_BF16))
    weights = [w.astype(_BF16) for w in (w_out[0], w_xq[0], w_xo[0], w_ff1[0], w_ff2[0])]
    lns = [vec(p[0]) for p in (ln1_g, ln1_b, ln2_g, ln2_b, ln3_g, ln3_b)]
    out = _trunk(a, gb, yb, h, kvm, weights, lns, seq, mem_len)
    return out.reshape(batch, seq, D_MODEL)
```

```python
import functools
import math

import jax
import jax.numpy as jnp
from jax import lax
from jax.experimental import pallas as pl
from jax.experimental.pallas import tpu as pltpu

D_MODEL = 1024
LN_EPS = 1e-5
ALPHA = 2.0 ** 0.25
ROPE_THETA = 500000.0
GMLP_GROUPS = 8
GMLP_CHUNK = 128
NSA_HEADS = 16
HEAD_DIM = 64
KV_HEADS = 4
GROUP = NSA_HEADS // KV_HEADS
KV_DIM = KV_HEADS * HEAD_DIM
ROT_DIM = HEAD_DIM // 4
CMP_BLOCK = 32
CMP_STRIDE = 16
CMP_HIDDEN = 4 * HEAD_DIM
SLC_BLOCK = 64
N_SELECT = 16
WINDOW = 512
Q_BLOCK = 128
KV_CHUNK = 128
V_ROWS = 80
XATTN_HEADS = 4
XATTN_HEAD_DIM = D_MODEL // XATTN_HEADS
D_FF = 4 * D_MODEL

LANES = 128
PROJ_TILE = 512
PROJ_SPLIT = 2
TRUNK_TILE = 512
TRUNK_SPLIT = 2
SEL_CHUNK = 512
TRIP_CHUNKS = (8, 4, 2)
HEADS_PER_STEP = 1
WIN_SPAN = WINDOW + Q_BLOCK
SELECT_BLOCKS = (32, 64, 128)
WIN_PIECES = 5
MASK_VALUE = -1e30
LOG2_E = math.log2(math.e)
VMEM_LIMIT_BYTES = 56 * 1024 * 1024

_F32 = jnp.float32
_BF16 = jnp.bfloat16


def _layer_norm(x, g, b):
    mu = jnp.mean(x, axis=-1, keepdims=True)
    xc = x - mu
    var = jnp.mean(xc * xc, axis=-1, keepdims=True)
    return xc * lax.rsqrt(var + LN_EPS) * g + b


def _dot(a, b):
    return jnp.dot(a, b, preferred_element_type=_F32)


def _dot_nt(a, b):
    return lax.dot_general(a, b, (((1,), (1,)), ((), ())), preferred_element_type=_F32)


def _rope_tables(pos_f32, inv_lane, neg_first, pos_second):
    ang = pos_f32 * inv_lane
    c = jnp.cos(ang)
    s = jnp.sin(ang)
    return c, s * neg_first, s * pos_second


def _rope_block(x, tables):
    c, s_first, s_second = tables
    half = ROT_DIM // 2
    up = pltpu.roll(x, LANES - half, axis=1)
    down = pltpu.roll(x, half, axis=1)
    return x * c + up * s_first + down * s_second


def _const_spec(shape):
    nd = len(shape)
    return pl.BlockSpec(shape, lambda *_: (0,) * nd, pipeline_mode=pl.Buffered(1))


def _proj_kernel(x_ref, pos_ref, lng_ref, lnb_ref, wu_ref, wv_ref, wga_ref, wq_ref, wgb_ref, wkv_ref, wgn_ref,
                 glng_ref, glnb_ref, ws_ref, bst_ref, inv_ref, m1_ref, m2_ref,
                 h_ref, a_ref, gb_ref, qt_ref, kvc_ref, ks_ref, vst_ref, kw_ref, vwt_ref, gnt_ref, cmp_ref, *, seq):
    tm = x_ref.shape[0]
    part = tm // PROJ_SPLIT
    ti = lax.broadcasted_iota(jnp.int32, (GMLP_CHUNK, GMLP_CHUNK), 0)
    si = lax.broadcasted_iota(jnp.int32, (GMLP_CHUNK, GMLP_CHUNK), 1)
    w_tril = [jnp.where(si <= ti, ws_ref[g], 0.0).astype(_BF16) for g in range(GMLP_GROUPS)]
    bst = bst_ref[...]
    q_scale = HEAD_DIM ** -0.5 * LOG2_E
    heads_per_slab = LANES // HEAD_DIM
    blk_lane = lax.broadcasted_iota(jnp.int32, (part, LANES), 1)
    ones_col = (lax.broadcasted_iota(jnp.int32, (part, HEAD_DIM), 1) == 0).astype(_F32)

    def stages(r0):
        rs = slice(r0, r0 + part)
        h = _layer_norm(x_ref[rs, :], lng_ref[...], lnb_ref[...])
        h_ref[rs, :] = h
        hb = h.astype(_BF16)
        zu = _dot(hb, wu_ref[...])
        yield
        u_act = jax.nn.gelu(zu)
        zv = _dot(hb, wv_ref[...])
        yield
        vn = _layer_norm(jax.nn.gelu(zv), glng_ref[...], glnb_ref[...]).astype(_BF16)
        zga = _dot(hb, wga_ref[...])
        yield
        ua = u_act * jax.nn.sigmoid(zga)
        for g in range(GMLP_GROUPS):
            cs = slice(g * LANES, (g + 1) * LANES)
            for c in range(part // GMLP_CHUNK):
                cr = slice(c * GMLP_CHUNK, (c + 1) * GMLP_CHUNK)
                mixed = _dot(w_tril[g], vn[cr, cs]) + bst[:, g:g + 1]
                a_ref[r0 + c * GMLP_CHUNK:r0 + (c + 1) * GMLP_CHUNK, cs] = ua[cr, cs] * mixed
        zgb = _dot(hb, wgb_ref[...])
        yield
        gb_ref[rs, :] = jax.nn.sigmoid(zgb)
        zgn = _dot(hb, wgn_ref[...])
        yield
        gn = jax.nn.sigmoid(zgn)
        for hh in range(KV_HEADS):
            gnt_ref[hh * LANES:(hh + 1) * LANES, rs] = gn[:, hh * LANES:(hh + 1) * LANES].T
        zq = _dot(hb, wq_ref[...])
        yield
        tables = _rope_tables(pos_ref[rs, :].astype(_F32), inv_ref[...], m1_ref[...], m2_ref[...])
        for i in range(D_MODEL // LANES):
            cs = slice(i * LANES, (i + 1) * LANES)
            qt_ref[cs, rs] = (_rope_block(zq[:, cs], tables) * q_scale).T.astype(_BF16)
        zkv = _dot(hb, wkv_ref[...])
        yield
        for sl in range(2 * KV_DIM // LANES):
            cmp_ref[sl, rs, :] = zkv[:, sl * LANES:(sl + 1) * LANES]
        out_rows = slice(r0 // CMP_STRIDE, (r0 + part) // CMP_STRIDE)
        for sl in range(2 * KV_DIM // LANES):
            for l in range(CMP_STRIDE):
                pair = cmp_ref[sl, pl.ds(r0 + l, part // CMP_STRIDE, stride=CMP_STRIDE), :]
                for hh in range(heads_per_slab):
                    kvc_ref[sl * heads_per_slab + hh, out_rows, l * HEAD_DIM:(l + 1) * HEAD_DIM] = (
                        pair[:, hh * HEAD_DIM:(hh + 1) * HEAD_DIM])
        seq_pos = (pl.program_id(0) * tm + r0 + lax.broadcasted_iota(jnp.int32, (part, LANES), 0)) & (seq - 1)
        blk_onehot = jnp.where(lax.shift_right_logical(seq_pos, SLC_BLOCK.bit_length() - 1) == blk_lane, 1.0, 0.0)
        k_tail = jnp.concatenate([jnp.zeros((part, HEAD_DIM), _BF16), blk_onehot.astype(_BF16)], axis=1)
        for base, k_out, vt_out in ((2 * KV_DIM, ks_ref, vst_ref), (4 * KV_DIM, kw_ref, vwt_ref)):
            for i in range(KV_DIM // LANES):
                kr = _rope_block(zkv[:, base + i * LANES: base + (i + 1) * LANES], tables).astype(_BF16)
                for hh in range(heads_per_slab):
                    k_head = kr[:, hh * HEAD_DIM:(hh + 1) * HEAD_DIM]
                    if k_out is ks_ref:
                        k_out[2 * i + hh, rs, :] = jnp.concatenate([k_head, k_tail], axis=1)
                    else:
                        k_out[2 * i + hh, rs, :] = k_head
            for hh in range(KV_HEADS):
                v = zkv[:, base + KV_DIM + hh * HEAD_DIM: base + KV_DIM + (hh + 1) * HEAD_DIM]
                v_ext = jnp.concatenate([v, ones_col], axis=1)
                for c in range(part // KV_CHUNK):
                    vt_out[hh, r0 // KV_CHUNK + c] = v_ext[c * KV_CHUNK:(c + 1) * KV_CHUNK, :].T[:V_ROWS].astype(_BF16)
        yield

    for _ in zip(*[stages(i * part) for i in range(PROJ_SPLIT)]):
        pass


def _project(x2, pos2, ln_g, ln_b, w_parts, gln_g, gln_b, ws, bst, rope_consts, seq):
    n = x2.shape[0]
    tm = PROJ_TILE
    row = lambda shape: pl.BlockSpec(shape, lambda i: (i, 0))
    col = lambda shape: pl.BlockSpec(shape, lambda i: (0, i))
    head_rows = lambda nh, r, w: pl.BlockSpec((nh, r, w), lambda i: (0, i, 0))
    vt_spec = pl.BlockSpec((KV_HEADS, tm // KV_CHUNK, V_ROWS, KV_CHUNK), lambda i: (0, i, 0, 0))
    w_specs = [_const_spec(w.shape) for w in w_parts]
    in_specs = ([row((tm, D_MODEL)), row((tm, 1)), _const_spec(ln_g.shape), _const_spec(ln_b.shape)] + w_specs
                + [_const_spec(gln_g.shape), _const_spec(gln_b.shape), _const_spec(ws.shape), _const_spec(bst.shape)]
                + [_const_spec(c.shape) for c in rope_consts])
    out_shape = (
        jax.ShapeDtypeStruct((n, D_MODEL), _F32),
        jax.ShapeDtypeStruct((n, D_MODEL), _F32),
        jax.ShapeDtypeStruct((n, D_MODEL), _F32),
        jax.ShapeDtypeStruct((D_MODEL, n), _BF16),
        jax.ShapeDtypeStruct((2 * KV_HEADS, n // CMP_STRIDE, CMP_STRIDE * HEAD_DIM), _F32),
        jax.ShapeDtypeStruct((KV_HEADS, n, 2 * LANES), _BF16),
        jax.ShapeDtypeStruct((KV_HEADS, n // KV_CHUNK, V_ROWS, KV_CHUNK), _BF16),
        jax.ShapeDtypeStruct((KV_HEADS, n, HEAD_DIM), _BF16),
        jax.ShapeDtypeStruct((KV_HEADS, n // KV_CHUNK, V_ROWS, KV_CHUNK), _BF16),
        jax.ShapeDtypeStruct((KV_HEADS * LANES, n), _F32),
    )
    out_specs = (row((tm, D_MODEL)), row((tm, D_MODEL)), row((tm, D_MODEL)), col((D_MODEL, tm)),
                 head_rows(2 * KV_HEADS, tm // CMP_STRIDE, CMP_STRIDE * HEAD_DIM),
                 head_rows(KV_HEADS, tm, 2 * LANES), vt_spec, head_rows(KV_HEADS, tm, HEAD_DIM), vt_spec,
                 col((KV_HEADS * LANES, tm)))
    return pl.pallas_call(
        functools.partial(_proj_kernel, seq=seq), grid=(n // tm,), in_specs=in_specs,
        out_specs=out_specs, out_shape=out_shape,
        scratch_shapes=[pltpu.VMEM((2 * KV_DIM // LANES, tm, LANES), _F32)],
        compiler_params=pltpu.CompilerParams(dimension_semantics=("arbitrary",), vmem_limit_bytes=VMEM_LIMIT_BYTES),
        name="nsa_proj",
    )(x2, pos2, ln_g, ln_b, *w_parts, gln_g, gln_b, ws, bst, *rope_consts)


def _compress_kernel(x_ref, pos_ref, pe_ref, w1_ref, b1_ref, w2_ref, b2_ref, inv_ref, m1_ref, m2_ref,
                     o_ref, ot_ref, perm_ref):
    rows = x_ref.shape[0]
    is_key = (pl.program_id(1) == 0).astype(_F32)
    x = x_ref[...]
    half = CMP_STRIDE * HEAD_DIM
    ya = _dot((x + pe_ref[0:1, :]).astype(_BF16), w1_ref[0:half, :])
    yb = _dot((x + pe_ref[1:2, :]).astype(_BF16), w1_ref[half:2 * half, :])
    hid = jax.nn.gelu(ya + pltpu.roll(yb, rows - 1, axis=0) + b1_ref[...])
    out = _dot(hid.astype(_BF16), w2_ref[...]) + b2_ref[...]
    tables = _rope_tables(pos_ref[...].astype(_F32) * is_key, inv_ref[...], m1_ref[...], m2_ref[...])
    perm_ref[...] = _rope_block(out, tables)
    groups = rows // 4
    for r in range(4):
        part = perm_ref[pl.ds(r, groups, stride=4), :]
        o_ref[r * groups:(r + 1) * groups, :] = part[:, :HEAD_DIM].astype(_BF16)
        ot_ref[:, r * groups:(r + 1) * groups] = part.T[:HEAD_DIM, :].astype(_BF16)


def _compress(kvc, pos_end, pe, w1, b1, w2, b2, rope_consts, batch):
    rows = kvc.shape[2]
    flat = CMP_STRIDE * HEAD_DIM
    in_specs = [
        pl.BlockSpec((None, None, rows, flat), lambda b, kv, hh: (kv * KV_HEADS + hh, b, 0, 0)),
        pl.BlockSpec((None, rows, 1), lambda b, kv, hh: (b, 0, 0)),
        pl.BlockSpec((None, 2, flat), lambda b, kv, hh: (kv, 0, 0)),
        pl.BlockSpec((None, 2 * flat, CMP_HIDDEN), lambda b, kv, hh: (kv, 0, 0)),
        pl.BlockSpec((None, 1, CMP_HIDDEN), lambda b, kv, hh: (kv, 0, 0)),
        pl.BlockSpec((None, CMP_HIDDEN, LANES), lambda b, kv, hh: (kv, 0, 0)),
        pl.BlockSpec((None, 1, LANES), lambda b, kv, hh: (kv, 0, 0)),
    ] + [pl.BlockSpec(c.shape, lambda b, kv, hh: (0, 0)) for c in rope_consts]
    return pl.pallas_call(
        _compress_kernel, grid=(batch, 2, KV_HEADS), in_specs=in_specs,
        out_specs=(pl.BlockSpec((None, None, None, rows, HEAD_DIM), lambda b, kv, hh: (kv, b, hh, 0, 0)),
                   pl.BlockSpec((None, None, None, HEAD_DIM, rows), lambda b, kv, hh: (kv, b, hh, 0, 0))),
        out_shape=(jax.ShapeDtypeStruct((2, batch, KV_HEADS, rows, HEAD_DIM), _BF16),
                   jax.ShapeDtypeStruct((2, batch, KV_HEADS, HEAD_DIM, rows), _BF16)),
        scratch_shapes=[pltpu.VMEM((rows, LANES), _F32)],
        compiler_params=pltpu.CompilerParams(dimension_semantics=("arbitrary",) * 3, vmem_limit_bytes=VMEM_LIMIT_BYTES),
        name="nsa_compress",
    )(kvc, pos_end, pe, w1, b1, w2, b2, *rope_consts)


def _per_head(fn, s, *shared):
    return jnp.concatenate([fn(s[:, g * Q_BLOCK:(g + 1) * Q_BLOCK], *shared) for g in range(GROUP)], axis=1)


def _nsa_kernel(qt_ref, gnt_ref, kc_ref, vct_ref, ks_ref, vst_ref, kw_ref, vwt_ref, cend_ref, wbias_ref, *rest,
                n_sel, qb_start, n_blocks, chained):
    o_ref, sa_ref, sb_ref, qx_ref = rest[1:] if chained else rest
    rows = kc_ref.shape[1]
    n_slc = rows // 4
    seq = ks_ref.shape[1]
    heads = range(HEADS_PER_STEP)
    qb = pl.program_id(2) + qb_start
    q0 = qb * Q_BLOCK
    t = q0 + lax.broadcasted_iota(jnp.int32, (1, Q_BLOCK), 1)
    w0 = pl.multiple_of(jnp.maximum(q0 - WINDOW, 0), KV_CHUNK)
    wc = lax.shift_right_logical(w0, KV_CHUNK.bit_length() - 1)
    n_seq_chunks = seq // SEL_CHUNK
    sub = SEL_CHUNK // KV_CHUNK
    piece = WIN_SPAN // WIN_PIECES

    valid_c = cend_ref[...] <= t
    has_c = t >= CMP_BLOCK - 1
    blk = lax.broadcasted_iota(jnp.int32, (n_slc, 1), 0)
    cur = lax.shift_right_logical(t, SLC_BLOCK.bit_length() - 1)
    forced = (blk == 0) | (blk == cur) | (blk == cur - 1)
    bi = lax.broadcasted_iota(jnp.int32, (n_slc, Q_BLOCK), 0)
    bf = bi.astype(_F32)
    past = bi < lax.shift_right_logical(q0, SLC_BLOCK.bit_length() - 1)
    bias_w = wbias_ref[jnp.minimum(qb, WINDOW // Q_BLOCK)]
    causal = jnp.where(lax.broadcasted_iota(jnp.int32, (Q_BLOCK, Q_BLOCK), 0)
                       <= lax.broadcasted_iota(jnp.int32, (Q_BLOCK, Q_BLOCK), 1), 0.0, MASK_VALUE)
    out = [None] * HEADS_PER_STEP

    def before_loop(h):
        qt = qt_ref[h * GROUP * HEAD_DIM:(h + 1) * GROUP * HEAD_DIM, :]
        q_t = jnp.concatenate([qt[g * HEAD_DIM:(g + 1) * HEAD_DIM, :] for g in range(GROUP)], axis=1)
        raw_c = _dot(kc_ref[h], q_t)
        raw_w = _dot(kw_ref[h, pl.ds(w0, WIN_SPAN), :], q_t)
        raw_d = _dot(ks_ref[h, pl.ds(q0, Q_BLOCK), :HEAD_DIM], q_t)
        raw_0 = _dot(ks_ref[h, 0:SEL_CHUNK, :HEAD_DIM], q_t)
        yield

        s_w = _per_head(lambda s, b: s + b, raw_w, bias_w)
        m_w = jnp.max(s_w, axis=0, keepdims=True)
        s_d = _per_head(lambda s, b: s + b, raw_d, causal)
        m_d = jnp.max(s_d, axis=0, keepdims=True)

        def select(nb):
            take = lambda x: x if nb == n_slc else jnp.concatenate(
                [x[r * n_slc:r * n_slc + nb] for r in range(4)], axis=0)

            s_c = _per_head(lambda s, v: jnp.where(v, s, MASK_VALUE), take(raw_c), take(valid_c))
            e_c = jnp.exp2(s_c - jnp.max(s_c, axis=0, keepdims=True))
            inv_c = _per_head(lambda l, ok: jnp.where(ok, 1.0 / l, 0.0), jnp.sum(e_c, axis=0, keepdims=True), has_c)
            p_c = e_c * inv_c
            p_mm = p_c.astype(_BF16)
            if nb < n_slc:
                gap = jnp.zeros((n_slc - nb, GROUP * Q_BLOCK), _BF16)
                p_mm = jnp.concatenate([x for r in range(4) for x in (p_mm[r * nb:(r + 1) * nb], gap)], axis=0)
            o_c = _dot(vct_ref[h], p_mm)

            p_sum = p_c[:, :Q_BLOCK]
            for g in range(1, GROUP):
                p_sum = p_sum + p_c[:, g * Q_BLOCK:(g + 1) * Q_BLOCK]
            parts = [p_sum[r * nb:(r + 1) * nb, :] for r in range(4)]
            prev3 = jnp.where(blk[:nb] == 0, 0.0, pltpu.roll(parts[3], 1, axis=0))
            p_slc = parts[0] + parts[1] + parts[2] + parts[3] + prev3

            st = jnp.where(forced[:nb], -jnp.inf, jnp.where(blk[:nb] <= cur, p_slc, -1.0))
            p_w_parts = []
            for r in range(n_sel - 3):
                mx = jnp.max(st, axis=0, keepdims=True)
                first = jnp.min(jnp.where(st == mx, bf[:nb], float(n_slc)), axis=0, keepdims=True)
                if r < WIN_PIECES:
                    part = jnp.exp2(s_w[r * piece:(r + 1) * piece, :] - m_w)
                    p_w_parts.append(part.astype(_BF16))
                    col_max = jnp.max(part, axis=0, keepdims=True)
                    tie = col_max[:, :Q_BLOCK]
                    for g in range(1, GROUP):
                        tie = jnp.maximum(tie, col_max[:, g * Q_BLOCK:(g + 1) * Q_BLOCK])
                    first = first + tie * 0.0
                st = jnp.where(bf[:nb] == first, -jnp.inf, st)
            sel = st == -jnp.inf
            pen = jnp.where(sel & past[:nb], 0.0, MASK_VALUE)
            if nb < n_slc:
                pen = jnp.concatenate([pen, jnp.full((n_slc - nb, Q_BLOCK), MASK_VALUE, _F32)], axis=0)
            return (o_c, pen) + tuple(p_w_parts)

        o_c, pen, *p_w_parts = select(n_blocks)
        yield
        p_d = jnp.exp2(s_d - m_d).astype(_BF16)
        if n_slc < LANES:
            pen = jnp.concatenate([pen, jnp.zeros((LANES - n_slc, Q_BLOCK), _F32)], axis=0)
        pen_heads = jnp.concatenate([pen] * GROUP, axis=1)
        for b in range(SEL_CHUNK // SLC_BLOCK):
            ks_rows = slice(b * SLC_BLOCK, (b + 1) * SLC_BLOCK)
            sa_ref[h, ks_rows, :] = raw_0[ks_rows, :] + pen_heads[b:b + 1, :]
        qx_ref[h, 0:HEAD_DIM, :] = q_t
        qx_ref[h, HEAD_DIM:LANES, :] = jnp.zeros((LANES - HEAD_DIM, GROUP * Q_BLOCK), _BF16)
        qx_ref[h, LANES:2 * LANES, :] = jnp.concatenate([pen.astype(_BF16)] * GROUP, axis=1)
        yield

        v_w = jnp.concatenate([vwt_ref[h, wc + i] for i in range(WIN_SPAN // KV_CHUNK)], axis=1)
        acc_w = _dot(v_w, jnp.concatenate(p_w_parts, axis=0))
        acc_d = _dot(vst_ref[h, qb], p_d)
        yield
        o_w = acc_w[:HEAD_DIM] * (1.0 / acc_w[HEAD_DIM:HEAD_DIM + 1])
        out[h] = (o_c, o_w, m_d, acc_d)
        yield

    for _ in zip(*[before_loop(h) for h in heads]):
        pass

    def sel_scores(h, c, buf):
        k0 = pl.multiple_of(jnp.minimum(c, n_seq_chunks - 1) * SEL_CHUNK, SEL_CHUNK)
        buf[h] = _dot(ks_ref[h, pl.ds(k0, SEL_CHUNK), :], qx_ref[h])

    def sel_consume(h, c, buf, m, acc):
        c0 = jnp.minimum(c, n_seq_chunks - 1) * sub
        s = buf[h]
        m_new = jnp.maximum(m, jnp.max(s, axis=0, keepdims=True))
        p = jnp.exp2(s - m_new)
        v_t = jnp.concatenate([vst_ref[h, c0 + i] for i in range(sub)], axis=1)
        acc = jnp.exp2(m - m_new) * acc + _dot(v_t, p.astype(_BF16))
        return m_new, acc

    bufs = (sa_ref, sb_ref)

    def sel_trip(per_trip, base):
        def body(i, carry):
            carry = list(carry)
            for j in range(per_trip):
                c = base + per_trip * i + j
                for h in heads:
                    sel_scores(h, c + 1, bufs[(j + 1) % 2])
                for h in heads:
                    carry[h] = sel_consume(h, c, bufs[j % 2], *carry[h])
            return tuple(carry)
        return body

    n_chunks = (q0 + SEL_CHUNK - 1) // SEL_CHUNK
    carry, done = tuple((out[h][2], out[h][3]) for h in heads), 0
    for per_trip in TRIP_CHUNKS:
        left = n_chunks - done
        trips = (left + 1) // 2 if per_trip == TRIP_CHUNKS[-1] else left // per_trip
        carry = lax.fori_loop(0, trips, sel_trip(per_trip, done), carry)
        done = done + trips * per_trip

    for h in heads:
        o_c, o_w = out[h][0], out[h][1]
        acc_s = carry[h][1]
        o_s = acc_s[:HEAD_DIM] * (1.0 / acc_s[HEAD_DIM:HEAD_DIM + 1])
        gates = gnt_ref[h * LANES:(h + 1) * LANES, :]
        outs = []
        for g in range(GROUP):
            ls = slice(g * Q_BLOCK, (g + 1) * Q_BLOCK)
            outs.append(gates[3 * g:3 * g + 1] * o_c[:, ls] + gates[3 * g + 1:3 * g + 2] * o_s[:, ls]
                        + gates[3 * g + 2:3 * g + 3] * o_w[:, ls])
        o_ref[:, h * GROUP * HEAD_DIM:(h + 1) * GROUP * HEAD_DIM] = jnp.concatenate(outs, axis=0).T


def _nsa(qt, gnt, kc, vct, ks, vst, kw, vwt, batch, seq):
    n = qt.shape[1]
    n_qb = seq // Q_BLOCK
    rows = kc.shape[3]
    n_slc = seq // SLC_BLOCK
    hps = HEADS_PER_STEP
    seq_rows = lambda w: pl.BlockSpec((hps, seq, w), lambda b, hp, i: (hp, b, 0))
    vt_spec = pl.BlockSpec((hps, seq // KV_CHUNK, V_ROWS, KV_CHUNK), lambda b, hp, i: (hp, b, 0, 0))
    slot = jnp.arange(rows)
    cmp_idx = 4 * (slot % n_slc) + slot // n_slc
    cmp_end = jnp.where(cmp_idx < rows - 1, CMP_STRIDE * cmp_idx + CMP_BLOCK - 1, jnp.iinfo(jnp.int32).max)
    cmp_end = jnp.broadcast_to(cmp_end.astype(jnp.int32)[:, None], (rows, Q_BLOCK))
    off = Q_BLOCK * jnp.arange(WINDOW // Q_BLOCK + 1)[:, None, None]
    diff = off + jnp.arange(Q_BLOCK)[None, None, :] - jnp.arange(WIN_SPAN)[None, :, None]
    win_bias = jnp.where((diff >= 0) & (diff < WINDOW), 0.0, MASK_VALUE).astype(_F32)
    counts = sorted({min(nb, n_slc) for nb in SELECT_BLOCKS})
    starts = [0] + [nb // 2 for nb in counts[:-1]]
    stops = starts[1:] + [n_qb]
    yb = None
    for qb_start, qb_stop, n_blocks in zip(starts, stops, counts):
        qcol = lambda r, s=qb_start: pl.BlockSpec((hps * r, Q_BLOCK), lambda b, hp, i: (hp, b * n_qb + s + i))
        in_specs = [
            qcol(GROUP * HEAD_DIM), qcol(LANES),
            pl.BlockSpec((None, None, hps, rows, HEAD_DIM), lambda b, hp, i: (0, b, hp, 0, 0)),
            pl.BlockSpec((None, None, hps, HEAD_DIM, rows), lambda b, hp, i: (1, b, hp, 0, 0)),
            seq_rows(2 * LANES), vt_spec, seq_rows(HEAD_DIM), vt_spec,
            _const_spec(cmp_end.shape), _const_spec(win_bias.shape),
        ]
        operands = [qt, gnt, kc, vct, ks, vst, kw, vwt, cmp_end, win_bias]
        chained = yb is not None
        if chained:
            in_specs.append(pl.BlockSpec(memory_space=pl.ANY))
            operands.append(yb)
        yb = pl.pallas_call(
            functools.partial(_nsa_kernel, n_sel=min(N_SELECT, n_slc), qb_start=qb_start, n_blocks=n_blocks,
                              chained=chained),
            grid=(batch, KV_HEADS // hps, qb_stop - qb_start), in_specs=in_specs,
            out_specs=pl.BlockSpec((Q_BLOCK, hps * GROUP * HEAD_DIM),
                                   lambda b, hp, i, s=qb_start: (b * n_qb + s + i, hp)),
            out_shape=jax.ShapeDtypeStruct((n, D_MODEL), _F32),
            input_output_aliases={len(operands) - 1: 0} if chained else {},
            scratch_shapes=[pltpu.VMEM((hps, SEL_CHUNK, GROUP * Q_BLOCK), _F32)] * 2
            + [pltpu.VMEM((hps, 2 * LANES, GROUP * Q_BLOCK), _BF16)],
            compiler_params=pltpu.CompilerParams(dimension_semantics=("arbitrary",) * 3,
                                                 vmem_limit_bytes=VMEM_LIMIT_BYTES),
            name=f"nsa_attention_{n_blocks}",
        )(*operands)
    return yb


def _memkv_kernel(mem_ref, w_ref, o_ref):
    o_ref[...] = _dot(mem_ref[...].astype(_BF16), w_ref[...]).astype(_BF16)


def _memkv(mem2, w_xkv):
    m = mem2.shape[0]
    return pl.pallas_call(
        _memkv_kernel, grid=(1,),
        in_specs=[pl.BlockSpec(mem2.shape, lambda i: (0, 0)), pl.BlockSpec(w_xkv.shape, lambda i: (0, 0))],
        out_specs=pl.BlockSpec((m, 2 * D_MODEL), lambda i: (0, 0)),
        out_shape=jax.ShapeDtypeStruct((m, 2 * D_MODEL), _BF16),
        compiler_params=pltpu.CompilerParams(vmem_limit_bytes=VMEM_LIMIT_BYTES),
        name="mem_kv",
    )(mem2, w_xkv)


def _trunk_kernel(a_ref, gb_ref, yb_ref, h_ref, kvm_ref, wo_ref, wxq_ref, wxo_ref, wf1_ref, wf2_ref,
                  g1_ref, b1_ref, g2_ref, b2_ref, g3_ref, b3_ref, o_ref):
    tm = a_ref.shape[0]
    halves = [slice(i * (tm // TRUNK_SPLIT), (i + 1) * (tm // TRUNK_SPLIT)) for i in range(TRUNK_SPLIT)]
    both = lambda fn, *xs: [fn(*(x[i] for x in xs)) for i in range(TRUNK_SPLIT)]

    mix = [(a_ref[r, :] + gb_ref[r, :] * yb_ref[r, :]).astype(_BF16) for r in halves]
    y1 = both(lambda m: _dot(m, wo_ref[...]), mix)
    h1 = [_layer_norm(ALPHA * h_ref[r, :] + y, g1_ref[...], b1_ref[...]) for r, y in zip(halves, y1)]

    qx = both(lambda x: _dot(x.astype(_BF16), wxq_ref[...]).astype(_BF16), h1)
    heads = [[] for _ in range(TRUNK_SPLIT)]
    for hh in range(XATTN_HEADS):
        cs = slice(hh * XATTN_HEAD_DIM, (hh + 1) * XATTN_HEAD_DIM)
        vcs = slice(D_MODEL + hh * XATTN_HEAD_DIM, D_MODEL + (hh + 1) * XATTN_HEAD_DIM)
        s = both(lambda q: _dot_nt(q[:, cs], kvm_ref[:, cs]) * (XATTN_HEAD_DIM ** -0.5), qx)
        e = both(lambda x: jnp.exp(x - jnp.max(x, axis=-1, keepdims=True)), s)
        p = both(lambda x: (x * (1.0 / jnp.sum(x, axis=-1, keepdims=True))).astype(_BF16), e)
        for i, o in enumerate(both(lambda x: _dot(x, kvm_ref[:, vcs]), p)):
            heads[i].append(o)
    xo = both(lambda hs: jnp.concatenate(hs, axis=1).astype(_BF16), heads)
    y2 = both(lambda x: _dot(x, wxo_ref[...]), xo)
    h2 = both(lambda x, y: _layer_norm(ALPHA * x + y, g2_ref[...], b2_ref[...]), h1, y2)

    h2b = both(lambda x: x.astype(_BF16), h2)
    ff = [None] * TRUNK_SPLIT
    for c in range(D_FF // D_MODEL):
        cs = slice(c * D_MODEL, (c + 1) * D_MODEL)
        act = both(lambda x: jnp.square(jnp.maximum(_dot(x, wf1_ref[:, cs]), 0.0)).astype(_BF16), h2b)
        part = both(lambda x: _dot(x, wf2_ref[cs, :]), act)
        ff = part if c == 0 else both(lambda x, y: x + y, ff, part)
    for r, x, y in zip(halves, h2, ff):
        o_ref[r, :] = _layer_norm(ALPHA * x + y, g3_ref[...], b3_ref[...])


def _trunk(a, gb, yb, h, kvm, weights, lns, seq, mem_len):
    n = a.shape[0]
    tm = TRUNK_TILE
    tiles_per_batch = seq // tm
    row = pl.BlockSpec((tm, D_MODEL), lambda i: (i, 0))
    in_specs = ([row, row, row, row, pl.BlockSpec((mem_len, 2 * D_MODEL), lambda i: (i // tiles_per_batch, 0))]
                + [_const_spec(w.shape) for w in weights] + [_const_spec(p.shape) for p in lns])
    return pl.pallas_call(
        _trunk_kernel, grid=(n // tm,), in_specs=in_specs, out_specs=row,
        out_shape=jax.ShapeDtypeStruct((n, D_MODEL), _F32),
        compiler_params=pltpu.CompilerParams(dimension_semantics=("arbitrary",), vmem_limit_bytes=VMEM_LIMIT_BYTES),
        name="trunk",
    )(a, gb, yb, h, kvm, *weights, *lns)


def _rope_constants():
    half = ROT_DIM // 2
    inv = ROPE_THETA ** (-jnp.arange(half, dtype=_F32) / half)
    d = jnp.arange(LANES) % HEAD_DIM
    inv_lane = jnp.where(d < ROT_DIM, inv[d % half], 0.0).astype(_F32)[None, :]
    neg_first = jnp.where(d < half, -1.0, 0.0).astype(_F32)[None, :]
    pos_second = jnp.where((d >= half) & (d < ROT_DIM), 1.0, 0.0).astype(_F32)[None, :]
    return inv_lane, neg_first, pos_second


def kernel(x, mem, positions, ln_in_g, ln_in_b, w_in, gmlp_ln_g, gmlp_ln_b, gmlp_ws, gmlp_bs, cmp_k_pe, cmp_k_w1, cmp_k_b1, cmp_k_w2, cmp_k_b2, cmp_v_pe, cmp_v_w1, cmp_v_b1, cmp_v_w2, cmp_v_b2, w_out, ln1_g, ln1_b, w_xq, w_xkv, w_xo, ln2_g, ln2_b, w_ff1, w_ff2, ln3_g, ln3_b):
    batch, seq, _ = x.shape
    mem_len = mem.shape[1]
    n = batch * seq
    n_slc = seq // SLC_BLOCK
    assert w_in.shape[0] == 1, "one layer"
    assert seq % SEL_CHUNK == 0 and TRIP_CHUNKS[-1] == 2 and seq >= WIN_SPAN and seq & (seq - 1) == 0 and n_slc <= LANES

    rope_consts = _rope_constants()
    vec = lambda p: p.reshape(1, -1)

    wi = w_in[0]
    o_u, o_v, o_q, o_kv, o_gn, o_ga, o_gb = (0, 1024, 2048, 3072, 3072 + 6 * KV_DIM, 3120 + 6 * KV_DIM, 4144 + 6 * KV_DIM)
    w_gn = wi[:, o_gn:o_ga].reshape(D_MODEL, KV_HEADS, 3 * GROUP)
    w_gn = jnp.pad(w_gn, ((0, 0), (0, 0), (0, LANES - 3 * GROUP))).reshape(D_MODEL, KV_HEADS * LANES)
    w_parts = [wi[:, o_u:o_v], wi[:, o_v:o_q], wi[:, o_ga:o_gb], wi[:, o_q:o_kv], wi[:, o_gb:], wi[:, o_kv:o_gn], w_gn]
    w_parts = [w.astype(_BF16) for w in w_parts]

    h, a, gb, qt, kvc, ks, vst, kw, vwt, gnt = _project(
        x.reshape(n, D_MODEL), positions.reshape(n, 1), vec(ln_in_g), vec(ln_in_b), w_parts,
        vec(gmlp_ln_g[0]), vec(gmlp_ln_b[0]), gmlp_ws[0], gmlp_bs[0].T, rope_consts, seq)

    rows = seq // CMP_STRIDE
    flat = CMP_STRIDE * HEAD_DIM
    pos_end = jnp.pad(positions[:, CMP_BLOCK - 1::CMP_STRIDE], ((0, 0), (0, 1)))[:, :, None]
    pad_lanes = lambda w: jnp.pad(w, ((0, 0), (0, LANES - HEAD_DIM)))
    kvcmp, kvcmp_t = _compress(
        kvc.reshape(2 * KV_HEADS, batch, rows, flat), pos_end,
        jnp.stack([cmp_k_pe[0].reshape(2, flat), cmp_v_pe[0].reshape(2, flat)]),
        jnp.stack([cmp_k_w1[0], cmp_v_w1[0]]).astype(_BF16),
        jnp.stack([vec(cmp_k_b1[0]), vec(cmp_v_b1[0])]),
        jnp.stack([pad_lanes(cmp_k_w2[0]), pad_lanes(cmp_v_w2[0])]).astype(_BF16),
        jnp.stack([pad_lanes(vec(cmp_k_b2[0])), pad_lanes(vec(cmp_v_b2[0]))]),
        rope_consts, batch)

    yb = _nsa(qt, gnt, kvcmp, kvcmp_t, ks, vst, kw, vwt, batch, seq)

    kvm = _memkv(mem.reshape(batch * mem_len, D_MODEL), w_xkv[0].astype(_BF16))
    weights = [w.astype(_BF16) for w in (w_out[0], w_xq[0], w_xo[0], w_ff1[0], w_ff2[0])]
    lns = [vec(p[0]) for p in (ln1_g, ln1_b, ln2_g, ln2_b, ln3_g, ln3_b)]
    out = _trunk(a, gb, yb, h, kvm, weights, lns, seq, mem_len)
    return out.reshape(batch, seq, D_MODEL)
```

```python
import functools
import math

import jax
import jax.numpy as jnp
from jax import lax
from jax.experimental import pallas as pl
from jax.experimental.pallas import tpu as pltpu

D_MODEL = 1024
LN_EPS = 1e-5
ALPHA = 2.0 ** 0.25
ROPE_THETA = 500000.0
GMLP_GROUPS = 8
GMLP_CHUNK = 128
NSA_HEADS = 16
HEAD_DIM = 64
KV_HEADS = 4
GROUP = NSA_HEADS // KV_HEADS
KV_DIM = KV_HEADS * HEAD_DIM
ROT_DIM = HEAD_DIM // 4
CMP_BLOCK = 32
CMP_STRIDE = 16
CMP_HIDDEN = 4 * HEAD_DIM
SLC_BLOCK = 64
N_SELECT = 16
WINDOW = 512
Q_BLOCK = 128
KV_CHUNK = 128
V_ROWS = 80
XATTN_HEADS = 4
XATTN_HEAD_DIM = D_MODEL // XATTN_HEADS
D_FF = 4 * D_MODEL

LANES = 128
PROJ_TILE = 512
PROJ_SPLIT = 2
TRUNK_TILE = 512
TRUNK_SPLIT = 2
SEL_CHUNK = 512
TRIP_CHUNKS = (8, 4, 2)
HEADS_PER_STEP = 1
WIN_SPAN = WINDOW + Q_BLOCK
SELECT_BLOCKS = (32, 64, 128)
WIN_PIECES = 5
MASK_VALUE = -1e30
LOG2_E = math.log2(math.e)
VMEM_LIMIT_BYTES = 56 * 1024 * 1024

_F32 = jnp.float32
_BF16 = jnp.bfloat16


def _layer_norm(x, g, b):
    mu = jnp.mean(x, axis=-1, keepdims=True)
    xc = x - mu
    var = jnp.mean(xc * xc, axis=-1, keepdims=True)
    return xc * lax.rsqrt(var + LN_EPS) * g + b


def _dot(a, b):
    return jnp.dot(a, b, preferred_element_type=_F32)


def _dot_nt(a, b):
    return lax.dot_general(a, b, (((1,), (1,)), ((), ())), preferred_element_type=_F32)


def _rope_tables(pos_f32, inv_lane, neg_first, pos_second):
    ang = pos_f32 * inv_lane
    c = jnp.cos(ang)
    s = jnp.sin(ang)
    return c, s * neg_first, s * pos_second


def _rope_block(x, tables):
    c, s_first, s_second = tables
    half = ROT_DIM // 2
    up = pltpu.roll(x, LANES - half, axis=1)
    down = pltpu.roll(x, half, axis=1)
    return x * c + up * s_first + down * s_second


def _const_spec(shape):
    nd = len(shape)
    return pl.BlockSpec(shape, lambda *_: (0,) * nd, pipeline_mode=pl.Buffered(1))


def _proj_kernel(x_ref, pos_ref, lng_ref, lnb_ref, wu_ref, wv_ref, wga_ref, wq_ref, wgb_ref, wkv_ref, wgn_ref,
                 glng_ref, glnb_ref, ws_ref, bst_ref, inv_ref, m1_ref, m2_ref,
                 h_ref, a_ref, gb_ref, qt_ref, kvc_ref, ks_ref, vst_ref, kw_ref, vwt_ref, gnt_ref, cmp_ref, *, seq):
    tm = x_ref.shape[0]
    part = tm // PROJ_SPLIT
    ti = lax.broadcasted_iota(jnp.int32, (GMLP_CHUNK, GMLP_CHUNK), 0)
    si = lax.broadcasted_iota(jnp.int32, (GMLP_CHUNK, GMLP_CHUNK), 1)
    w_tril = [jnp.where(si <= ti, ws_ref[g], 0.0).astype(_BF16) for g in range(GMLP_GROUPS)]
    bst = bst_ref[...]
    q_scale = HEAD_DIM ** -0.5 * LOG2_E
    heads_per_slab = LANES // HEAD_DIM
    blk_lane = lax.broadcasted_iota(jnp.int32, (part, LANES), 1)
    ones_col = (lax.broadcasted_iota(jnp.int32, (part, HEAD_DIM), 1) == 0).astype(_F32)

    def stages(r0):
        rs = slice(r0, r0 + part)
        h = _layer_norm(x_ref[rs, :], lng_ref[...], lnb_ref[...])
        h_ref[rs, :] = h
        hb = h.astype(_BF16)
        zu = _dot(hb, wu_ref[...])
        yield
        u_act = jax.nn.gelu(zu)
        zv = _dot(hb, wv_ref[...])
        yield
        vn = _layer_norm(jax.nn.gelu(zv), glng_ref[...], glnb_ref[...]).astype(_BF16)
        zga = _dot(hb, wga_ref[...])
        yield
        ua = u_act * jax.nn.sigmoid(zga)
        for g in range(GMLP_GROUPS):
            cs = slice(g * LANES, (g + 1) * LANES)
            for c in range(part // GMLP_CHUNK):
                cr = slice(c * GMLP_CHUNK, (c + 1) * GMLP_CHUNK)
                mixed = _dot(w_tril[g], vn[cr, cs]) + bst[:, g:g + 1]
                a_ref[r0 + c * GMLP_CHUNK:r0 + (c + 1) * GMLP_CHUNK, cs] = ua[cr, cs] * mixed
        zgb = _dot(hb, wgb_ref[...])
        yield
        gb_ref[rs, :] = jax.nn.sigmoid(zgb)
        zgn = _dot(hb, wgn_ref[...])
        yield
        gn = jax.nn.sigmoid(zgn)
        for hh in range(KV_HEADS):
            gnt_ref[hh * LANES:(hh + 1) * LANES, rs] = gn[:, hh * LANES:(hh + 1) * LANES].T
        zq = _dot(hb, wq_ref[...])
        yield
        tables = _rope_tables(pos_ref[rs, :].astype(_F32), inv_ref[...], m1_ref[...], m2_ref[...])
        for i in range(D_MODEL // LANES):
            cs = slice(i * LANES, (i + 1) * LANES)
            qt_ref[cs, rs] = (_rope_block(zq[:, cs], tables) * q_scale).T.astype(_BF16)
        zkv = _dot(hb, wkv_ref[...])
        yield
        for sl in range(2 * KV_DIM // LANES):
            cmp_ref[sl, rs, :] = zkv[:, sl * LANES:(sl + 1) * LANES]
        out_rows = slice(r0 // CMP_STRIDE, (r0 + part) // CMP_STRIDE)
        for sl in range(2 * KV_DIM // LANES):
            for l in range(CMP_STRIDE):
                pair = cmp_ref[sl, pl.ds(r0 + l, part // CMP_STRIDE, stride=CMP_STRIDE), :]
                for hh in range(heads_per_slab):
                    kvc_ref[sl * heads_per_slab + hh, out_rows, l * HEAD_DIM:(l + 1) * HEAD_DIM] = (
                        pair[:, hh * HEAD_DIM:(hh + 1) * HEAD_DIM])
        seq_pos = (pl.program_id(0) * tm + r0 + lax.broadcasted_iota(jnp.int32, (part, LANES), 0)) & (seq - 1)
        blk_onehot = jnp.where(lax.shift_right_logical(seq_pos, SLC_BLOCK.bit_length() - 1) == blk_lane, 1.0, 0.0)
        k_tail = jnp.concatenate([jnp.zeros((part, HEAD_DIM), _BF16), blk_onehot.astype(_BF16)], axis=1)
        for base, k_out, vt_out in ((2 * KV_DIM, ks_ref, vst_ref), (4 * KV_DIM, kw_ref, vwt_ref)):
            for i in range(KV_DIM // LANES):
                kr = _rope_block(zkv[:, base + i * LANES: base + (i + 1) * LANES], tables).astype(_BF16)
                for hh in range(heads_per_slab):
                    k_head = kr[:, hh * HEAD_DIM:(hh + 1) * HEAD_DIM]
                    if k_out is ks_ref:
                        k_out[2 * i + hh, rs, :] = jnp.concatenate([k_head, k_tail], axis=1)
                    else:
                        k_out[2 * i + hh, rs, :] = k_head
            for hh in range(KV_HEADS):
                v = zkv[:, base + KV_DIM + hh * HEAD_DIM: base + KV_DIM + (hh + 1) * HEAD_DIM]
                v_ext = jnp.concatenate([v, ones_col], axis=1)
                for c in range(part // KV_CHUNK):
                    vt_out[hh, r0 // KV_CHUNK + c] = v_ext[c * KV_CHUNK:(c + 1) * KV_CHUNK, :].T[:V_ROWS].astype(_BF16)
        yield

    for _ in zip(*[stages(i * part) for i in range(PROJ_SPLIT)]):
        pass


def _project(x2, pos2, ln_g, ln_b, w_parts, gln_g, gln_b, ws, bst, rope_consts, seq):
    n = x2.shape[0]
    tm = PROJ_TILE
    row = lambda shape: pl.BlockSpec(shape, lambda i: (i, 0))
    col = lambda shape: pl.BlockSpec(shape, lambda i: (0, i))
    head_rows = lambda nh, r, w: pl.BlockSpec((nh, r, w), lambda i: (0, i, 0))
    vt_spec = pl.BlockSpec((KV_HEADS, tm // KV_CHUNK, V_ROWS, KV_CHUNK), lambda i: (0, i, 0, 0))
    w_specs = [_const_spec(w.shape) for w in w_parts]
    in_specs = ([row((tm, D_MODEL)), row((tm, 1)), _const_spec(ln_g.shape), _const_spec(ln_b.shape)] + w_specs
                + [_const_spec(gln_g.shape), _const_spec(gln_b.shape), _const_spec(ws.shape), _const_spec(bst.shape)]
                + [_const_spec(c.shape) for c in rope_consts])
    out_shape = (
        jax.ShapeDtypeStruct((n, D_MODEL), _F32),
        jax.ShapeDtypeStruct((n, D_MODEL), _F32),
        jax.ShapeDtypeStruct((n, D_MODEL), _F32),
        jax.ShapeDtypeStruct((D_MODEL, n), _BF16),
        jax.ShapeDtypeStruct((2 * KV_HEADS, n // CMP_STRIDE, CMP_STRIDE * HEAD_DIM), _F32),
        jax.ShapeDtypeStruct((KV_HEADS, n, 2 * LANES), _BF16),
        jax.ShapeDtypeStruct((KV_HEADS, n // KV_CHUNK, V_ROWS, KV_CHUNK), _BF16),
        jax.ShapeDtypeStruct((KV_HEADS, n, HEAD_DIM), _BF16),
        jax.ShapeDtypeStruct((KV_HEADS, n // KV_CHUNK, V_ROWS, KV_CHUNK), _BF16),
        jax.ShapeDtypeStruct((KV_HEADS * LANES, n), _F32),
    )
    out_specs = (row((tm, D_MODEL)), row((tm, D_MODEL)), row((tm, D_MODEL)), col((D_MODEL, tm)),
                 head_rows(2 * KV_HEADS, tm // CMP_STRIDE, CMP_STRIDE * HEAD_DIM),
                 head_rows(KV_HEADS, tm, 2 * LANES), vt_spec, head_rows(KV_HEADS, tm, HEAD_DIM), vt_spec,
                 col((KV_HEADS * LANES, tm)))
    return pl.pallas_call(
        functools.partial(_proj_kernel, seq=seq), grid=(n // tm,), in_specs=in_specs,
        out_specs=out_specs, out_shape=out_shape,
        scratch_shapes=[pltpu.VMEM((2 * KV_DIM // LANES, tm, LANES), _F32)],
        compiler_params=pltpu.CompilerParams(dimension_semantics=("arbitrary",), vmem_limit_bytes=VMEM_LIMIT_BYTES),
        name="nsa_proj",
    )(x2, pos2, ln_g, ln_b, *w_parts, gln_g, gln_b, ws, bst, *rope_consts)


def _compress_kernel(x_ref, pos_ref, pe_ref, w1_ref, b1_ref, w2_ref, b2_ref, inv_ref, m1_ref, m2_ref,
                     o_ref, ot_ref, perm_ref):
    rows = x_ref.shape[0]
    is_key = (pl.program_id(1) == 0).astype(_F32)
    x = x_ref[...]
    half = CMP_STRIDE * HEAD_DIM
    ya = _dot((x + pe_ref[0:1, :]).astype(_BF16), w1_ref[0:half, :])
    yb = _dot((x + pe_ref[1:2, :]).astype(_BF16), w1_ref[half:2 * half, :])
    hid = jax.nn.gelu(ya + pltpu.roll(yb, rows - 1, axis=0) + b1_ref[...])
    out = _dot(hid.astype(_BF16), w2_ref[...]) + b2_ref[...]
    tables = _rope_tables(pos_ref[...].astype(_F32) * is_key, inv_ref[...], m1_ref[...], m2_ref[...])
    perm_ref[...] = _rope_block(out, tables)
    groups = rows // 4
    for r in range(4):
        part = perm_ref[pl.ds(r, groups, stride=4), :]
        o_ref[r * groups:(r + 1) * groups, :] = part[:, :HEAD_DIM].astype(_BF16)
        ot_ref[:, r * groups:(r + 1) * groups] = part.T[:HEAD_DIM, :].astype(_BF16)


def _compress(kvc, pos_end, pe, w1, b1, w2, b2, rope_consts, batch):
    rows = kvc.shape[2]
    flat = CMP_STRIDE * HEAD_DIM
    in_specs = [
        pl.BlockSpec((None, None, rows, flat), lambda b, kv, hh: (kv * KV_HEADS + hh, b, 0, 0)),
        pl.BlockSpec((None, rows, 1), lambda b, kv, hh: (b, 0, 0)),
        pl.BlockSpec((None, 2, flat), lambda b, kv, hh: (kv, 0, 0)),
        pl.BlockSpec((None, 2 * flat, CMP_HIDDEN), lambda b, kv, hh: (kv, 0, 0)),
        pl.BlockSpec((None, 1, CMP_HIDDEN), lambda b, kv, hh: (kv, 0, 0)),
        pl.BlockSpec((None, CMP_HIDDEN, LANES), lambda b, kv, hh: (kv, 0, 0)),
        pl.BlockSpec((None, 1, LANES), lambda b, kv, hh: (kv, 0, 0)),
    ] + [pl.BlockSpec(c.shape, lambda b, kv, hh: (0, 0)) for c in rope_consts]
    return pl.pallas_call(
        _compress_kernel, grid=(batch, 2, KV_HEADS), in_specs=in_specs,
        out_specs=(pl.BlockSpec((None, None, None, rows, HEAD_DIM), lambda b, kv, hh: (kv, b, hh, 0, 0)),
                   pl.BlockSpec((None, None, None, HEAD_DIM, rows), lambda b, kv, hh: (kv, b, hh, 0, 0))),
        out_shape=(jax.ShapeDtypeStruct((2, batch, KV_HEADS, rows, HEAD_DIM), _BF16),
                   jax.ShapeDtypeStruct((2, batch, KV_HEADS, HEAD_DIM, rows), _BF16)),
        scratch_shapes=[pltpu.VMEM((rows, LANES), _F32)],
        compiler_params=pltpu.CompilerParams(dimension_semantics=("arbitrary",) * 3, vmem_limit_bytes=VMEM_LIMIT_BYTES),
        name="nsa_compress",
    )(kvc, pos_end, pe, w1, b1, w2, b2, *rope_consts)


def _per_head(fn, s, *shared):
    return jnp.concatenate([fn(s[:, g * Q_BLOCK:(g + 1) * Q_BLOCK], *shared) for g in range(GROUP)], axis=1)


def _nsa_step(qt_ref, gnt_ref, kc_ref, vct_ref, ks_ref, vst_ref, kw_ref, vwt_ref, cend_ref, wbias_ref,
              o_ref, sa_ref, sb_ref, qx_ref, *, n_sel, n_blocks):
    rows = kc_ref.shape[1]
    n_slc = rows // 4
    seq = ks_ref.shape[1]
    heads = range(HEADS_PER_STEP)
    qb = pl.program_id(2)
    q0 = qb * Q_BLOCK
    t = q0 + lax.broadcasted_iota(jnp.int32, (1, Q_BLOCK), 1)
    w0 = pl.multiple_of(jnp.maximum(q0 - WINDOW, 0), KV_CHUNK)
    wc = lax.shift_right_logical(w0, KV_CHUNK.bit_length() - 1)
    n_seq_chunks = seq // SEL_CHUNK
    sub = SEL_CHUNK // KV_CHUNK
    piece = WIN_SPAN // WIN_PIECES

    valid_c = cend_ref[...] <= t
    has_c = t >= CMP_BLOCK - 1
    blk = lax.broadcasted_iota(jnp.int32, (n_slc, 1), 0)
    cur = lax.shift_right_logical(t, SLC_BLOCK.bit_length() - 1)
    forced = (blk == 0) | (blk == cur) | (blk == cur - 1)
    bi = lax.broadcasted_iota(jnp.int32, (n_slc, Q_BLOCK), 0)
    bf = bi.astype(_F32)
    past = bi < lax.shift_right_logical(q0, SLC_BLOCK.bit_length() - 1)
    bias_w = wbias_ref[jnp.minimum(qb, WINDOW // Q_BLOCK)]
    causal = jnp.where(lax.broadcasted_iota(jnp.int32, (Q_BLOCK, Q_BLOCK), 0)
                       <= lax.broadcasted_iota(jnp.int32, (Q_BLOCK, Q_BLOCK), 1), 0.0, MASK_VALUE)
    out = [None] * HEADS_PER_STEP

    def before_loop(h):
        qt = qt_ref[h * GROUP * HEAD_DIM:(h + 1) * GROUP * HEAD_DIM, :]
        q_t = jnp.concatenate([qt[g * HEAD_DIM:(g + 1) * HEAD_DIM, :] for g in range(GROUP)], axis=1)
        raw_c = _dot(kc_ref[h], q_t)
        raw_w = _dot(kw_ref[h, pl.ds(w0, WIN_SPAN), :], q_t)
        raw_d = _dot(ks_ref[h, pl.ds(q0, Q_BLOCK), :HEAD_DIM], q_t)
        raw_0 = _dot(ks_ref[h, 0:SEL_CHUNK, :HEAD_DIM], q_t)
        yield

        s_w = _per_head(lambda s, b: s + b, raw_w, bias_w)
        m_w = jnp.max(s_w, axis=0, keepdims=True)
        s_d = _per_head(lambda s, b: s + b, raw_d, causal)
        m_d = jnp.max(s_d, axis=0, keepdims=True)

        def select(nb):
            take = lambda x: x if nb == n_slc else jnp.concatenate(
                [x[r * n_slc:r * n_slc + nb] for r in range(4)], axis=0)

            s_c = _per_head(lambda s, v: jnp.where(v, s, MASK_VALUE), take(raw_c), take(valid_c))
            e_c = jnp.exp2(s_c - jnp.max(s_c, axis=0, keepdims=True))
            inv_c = _per_head(lambda l, ok: jnp.where(ok, 1.0 / l, 0.0), jnp.sum(e_c, axis=0, keepdims=True), has_c)
            p_c = e_c * inv_c
            p_mm = p_c.astype(_BF16)
            if nb < n_slc:
                gap = jnp.zeros((n_slc - nb, GROUP * Q_BLOCK), _BF16)
                p_mm = jnp.concatenate([x for r in range(4) for x in (p_mm[r * nb:(r + 1) * nb], gap)], axis=0)
            o_c = _dot(vct_ref[h], p_mm)

            p_sum = p_c[:, :Q_BLOCK]
            for g in range(1, GROUP):
                p_sum = p_sum + p_c[:, g * Q_BLOCK:(g + 1) * Q_BLOCK]
            parts = [p_sum[r * nb:(r + 1) * nb, :] for r in range(4)]
            prev3 = jnp.where(blk[:nb] == 0, 0.0, pltpu.roll(parts[3], 1, axis=0))
            p_slc = parts[0] + parts[1] + parts[2] + parts[3] + prev3

            st = jnp.where(forced[:nb], -jnp.inf, jnp.where(blk[:nb] <= cur, p_slc, -1.0))
            p_w_parts = []
            for r in range(n_sel - 3):
                mx = jnp.max(st, axis=0, keepdims=True)
                first = jnp.min(jnp.where(st == mx, bf[:nb], float(n_slc)), axis=0, keepdims=True)
                if r < WIN_PIECES:
                    part = jnp.exp2(s_w[r * piece:(r + 1) * piece, :] - m_w)
                    p_w_parts.append(part.astype(_BF16))
                    col_max = jnp.max(part, axis=0, keepdims=True)
                    tie = col_max[:, :Q_BLOCK]
                    for g in range(1, GROUP):
                        tie = jnp.maximum(tie, col_max[:, g * Q_BLOCK:(g + 1) * Q_BLOCK])
                    first = first + tie * 0.0
                st = jnp.where(bf[:nb] == first, -jnp.inf, st)
            sel = st == -jnp.inf
            pen = jnp.where(sel & past[:nb], 0.0, MASK_VALUE)
            if nb < n_slc:
                pen = jnp.concatenate([pen, jnp.full((n_slc - nb, Q_BLOCK), MASK_VALUE, _F32)], axis=0)
            return (o_c, pen) + tuple(p_w_parts)

        o_c, pen, *p_w_parts = select(n_blocks)
        yield
        p_d = jnp.exp2(s_d - m_d).astype(_BF16)
        if n_slc < LANES:
            pen = jnp.concatenate([pen, jnp.zeros((LANES - n_slc, Q_BLOCK), _F32)], axis=0)
        pen_heads = jnp.concatenate([pen] * GROUP, axis=1)
        for b in range(SEL_CHUNK // SLC_BLOCK):
            ks_rows = slice(b * SLC_BLOCK, (b + 1) * SLC_BLOCK)
            sa_ref[h, ks_rows, :] = raw_0[ks_rows, :] + pen_heads[b:b + 1, :]
        qx_ref[h, 0:HEAD_DIM, :] = q_t
        qx_ref[h, HEAD_DIM:LANES, :] = jnp.zeros((LANES - HEAD_DIM, GROUP * Q_BLOCK), _BF16)
        qx_ref[h, LANES:2 * LANES, :] = jnp.concatenate([pen.astype(_BF16)] * GROUP, axis=1)
        yield

        v_w = jnp.concatenate([vwt_ref[h, wc + i] for i in range(WIN_SPAN // KV_CHUNK)], axis=1)
        acc_w = _dot(v_w, jnp.concatenate(p_w_parts, axis=0))
        acc_d = _dot(vst_ref[h, qb], p_d)
        yield
        o_w = acc_w[:HEAD_DIM] * (1.0 / acc_w[HEAD_DIM:HEAD_DIM + 1])
        out[h] = (o_c, o_w, m_d, acc_d)
        yield

    for _ in zip(*[before_loop(h) for h in heads]):
        pass

    def sel_scores(h, c, buf):
        k0 = pl.multiple_of(jnp.minimum(c, n_seq_chunks - 1) * SEL_CHUNK, SEL_CHUNK)
        buf[h] = _dot(ks_ref[h, pl.ds(k0, SEL_CHUNK), :], qx_ref[h])

    def sel_consume(h, c, buf, m, acc):
        c0 = jnp.minimum(c, n_seq_chunks - 1) * sub
        s = buf[h]
        m_new = jnp.maximum(m, jnp.max(s, axis=0, keepdims=True))
        p = jnp.exp2(s - m_new)
        v_t = jnp.concatenate([vst_ref[h, c0 + i] for i in range(sub)], axis=1)
        acc = jnp.exp2(m - m_new) * acc + _dot(v_t, p.astype(_BF16))
        return m_new, acc

    bufs = (sa_ref, sb_ref)

    def sel_trip(per_trip, base):
        def body(i, carry):
            carry = list(carry)
            for j in range(per_trip):
                c = base + per_trip * i + j
                for h in heads:
                    sel_scores(h, c + 1, bufs[(j + 1) % 2])
                for h in heads:
                    carry[h] = sel_consume(h, c, bufs[j % 2], *carry[h])
            return tuple(carry)
        return body

    n_chunks = (q0 + SEL_CHUNK - 1) // SEL_CHUNK
    carry, done = tuple((out[h][2], out[h][3]) for h in heads), 0
    for per_trip in TRIP_CHUNKS:
        left = n_chunks - done
        trips = (left + 1) // 2 if per_trip == TRIP_CHUNKS[-1] else left // per_trip
        carry = lax.fori_loop(0, trips, sel_trip(per_trip, done), carry)
        done = done + trips * per_trip

    for h in heads:
        o_c, o_w = out[h][0], out[h][1]
        acc_s = carry[h][1]
        o_s = acc_s[:HEAD_DIM] * (1.0 / acc_s[HEAD_DIM:HEAD_DIM + 1])
        gates = gnt_ref[h * LANES:(h + 1) * LANES, :]
        outs = []
        for g in range(GROUP):
            ls = slice(g * Q_BLOCK, (g + 1) * Q_BLOCK)
            outs.append(gates[3 * g:3 * g + 1] * o_c[:, ls] + gates[3 * g + 1:3 * g + 2] * o_s[:, ls]
                        + gates[3 * g + 2:3 * g + 3] * o_w[:, ls])
        o_ref[:, h * GROUP * HEAD_DIM:(h + 1) * GROUP * HEAD_DIM] = jnp.concatenate(outs, axis=0).T


def _nsa_kernel(*refs, n_sel, ranges):
    qb = pl.program_id(2)
    for first, stop, n_blocks in ranges:
        @pl.when((qb >= first) & (qb < stop))
        def _():
            _nsa_step(*refs, n_sel=n_sel, n_blocks=n_blocks)


def _nsa(qt, gnt, kc, vct, ks, vst, kw, vwt, batch, seq):
    n = qt.shape[1]
    n_qb = seq // Q_BLOCK
    rows = kc.shape[3]
    n_slc = seq // SLC_BLOCK
    hps = HEADS_PER_STEP
    seq_rows = lambda w: pl.BlockSpec((hps, seq, w), lambda b, hp, i: (hp, b, 0))
    vt_spec = pl.BlockSpec((hps, seq // KV_CHUNK, V_ROWS, KV_CHUNK), lambda b, hp, i: (hp, b, 0, 0))
    slot = jnp.arange(rows)
    cmp_idx = 4 * (slot % n_slc) + slot // n_slc
    cmp_end = jnp.where(cmp_idx < rows - 1, CMP_STRIDE * cmp_idx + CMP_BLOCK - 1, jnp.iinfo(jnp.int32).max)
    cmp_end = jnp.broadcast_to(cmp_end.astype(jnp.int32)[:, None], (rows, Q_BLOCK))
    off = Q_BLOCK * jnp.arange(WINDOW // Q_BLOCK + 1)[:, None, None]
    diff = off + jnp.arange(Q_BLOCK)[None, None, :] - jnp.arange(WIN_SPAN)[None, :, None]
    win_bias = jnp.where((diff >= 0) & (diff < WINDOW), 0.0, MASK_VALUE).astype(_F32)
    counts = sorted({min(nb, n_slc) for nb in SELECT_BLOCKS})
    starts = [0] + [nb // 2 for nb in counts[:-1]]
    ranges = tuple(zip(starts, starts[1:] + [n_qb], counts))
    qcol = lambda r: pl.BlockSpec((hps * r, Q_BLOCK), lambda b, hp, i: (hp, b * n_qb + i))
    in_specs = [
        qcol(GROUP * HEAD_DIM), qcol(LANES),
        pl.BlockSpec((None, None, hps, rows, HEAD_DIM), lambda b, hp, i: (0, b, hp, 0, 0)),
        pl.BlockSpec((None, None, hps, HEAD_DIM, rows), lambda b, hp, i: (1, b, hp, 0, 0)),
        seq_rows(2 * LANES), vt_spec, seq_rows(HEAD_DIM), vt_spec,
        _const_spec(cmp_end.shape), _const_spec(win_bias.shape),
    ]
    return pl.pallas_call(
        functools.partial(_nsa_kernel, n_sel=min(N_SELECT, n_slc), ranges=ranges),
        grid=(batch, KV_HEADS // hps, n_qb), in_specs=in_specs,
        out_specs=pl.BlockSpec((Q_BLOCK, hps * GROUP * HEAD_DIM), lambda b, hp, i: (b * n_qb + i, hp)),
        out_shape=jax.ShapeDtypeStruct((n, D_MODEL), _F32),
        scratch_shapes=[pltpu.VMEM((hps, SEL_CHUNK, GROUP * Q_BLOCK), _F32)] * 2
        + [pltpu.VMEM((hps, 2 * LANES, GROUP * Q_BLOCK), _BF16)],
        compiler_params=pltpu.CompilerParams(dimension_semantics=("arbitrary",) * 3, vmem_limit_bytes=VMEM_LIMIT_BYTES),
        name="nsa_attention",
    )(qt, gnt, kc, vct, ks, vst, kw, vwt, cmp_end, win_bias)


def _memkv_kernel(mem_ref, w_ref, o_ref):
    o_ref[...] = _dot(mem_ref[...].astype(_BF16), w_ref[...]).astype(_BF16)


def _memkv(mem2, w_xkv):
    m = mem2.shape[0]
    return pl.pallas_call(
        _memkv_kernel, grid=(1,),
        in_specs=[pl.BlockSpec(mem2.shape, lambda i: (0, 0)), pl.BlockSpec(w_xkv.shape, lambda i: (0, 0))],
        out_specs=pl.BlockSpec((m, 2 * D_MODEL), lambda i: (0, 0)),
        out_shape=jax.ShapeDtypeStruct((m, 2 * D_MODEL), _BF16),
        compiler_params=pltpu.CompilerParams(vmem_limit_bytes=VMEM_LIMIT_BYTES),
        name="mem_kv",
    )(mem2, w_xkv)


def _trunk_kernel(a_ref, gb_ref, yb_ref, h_ref, kvm_ref, wo_ref, wxq_ref, wxo_ref, wf1_ref, wf2_ref,
                  g1_ref, b1_ref, g2_ref, b2_ref, g3_ref, b3_ref, o_ref):
    tm = a_ref.shape[0]
    halves = [slice(i * (tm // TRUNK_SPLIT), (i + 1) * (tm // TRUNK_SPLIT)) for i in range(TRUNK_SPLIT)]
    both = lambda fn, *xs: [fn(*(x[i] for x in xs)) for i in range(TRUNK_SPLIT)]

    mix = [(a_ref[r, :] + gb_ref[r, :] * yb_ref[r, :]).astype(_BF16) for r in halves]
    y1 = both(lambda m: _dot(m, wo_ref[...]), mix)
    h1 = [_layer_norm(ALPHA * h_ref[r, :] + y, g1_ref[...], b1_ref[...]) for r, y in zip(halves, y1)]

    qx = both(lambda x: _dot(x.astype(_BF16), wxq_ref[...]).astype(_BF16), h1)
    heads = [[] for _ in range(TRUNK_SPLIT)]
    for hh in range(XATTN_HEADS):
        cs = slice(hh * XATTN_HEAD_DIM, (hh + 1) * XATTN_HEAD_DIM)
        vcs = slice(D_MODEL + hh * XATTN_HEAD_DIM, D_MODEL + (hh + 1) * XATTN_HEAD_DIM)
        s = both(lambda q: _dot_nt(q[:, cs], kvm_ref[:, cs]) * (XATTN_HEAD_DIM ** -0.5), qx)
        e = both(lambda x: jnp.exp(x - jnp.max(x, axis=-1, keepdims=True)), s)
        p = both(lambda x: (x * (1.0 / jnp.sum(x, axis=-1, keepdims=True))).astype(_BF16), e)
        for i, o in enumerate(both(lambda x: _dot(x, kvm_ref[:, vcs]), p)):
            heads[i].append(o)
    xo = both(lambda hs: jnp.concatenate(hs, axis=1).astype(_BF16), heads)
    y2 = both(lambda x: _dot(x, wxo_ref[...]), xo)
    h2 = both(lambda x, y: _layer_norm(ALPHA * x + y, g2_ref[...], b2_ref[...]), h1, y2)

    h2b = both(lambda x: x.astype(_BF16), h2)
    ff = [None] * TRUNK_SPLIT
    for c in range(D_FF // D_MODEL):
        cs = slice(c * D_MODEL, (c + 1) * D_MODEL)
        act = both(lambda x: jnp.square(jnp.maximum(_dot(x, wf1_ref[:, cs]), 0.0)).astype(_BF16), h2b)
        part = both(lambda x: _dot(x, wf2_ref[cs, :]), act)
        ff = part if c == 0 else both(lambda x, y: x + y, ff, part)
    for r, x, y in zip(halves, h2, ff):
        o_ref[r, :] = _layer_norm(ALPHA * x + y, g3_ref[...], b3_ref[...])


def _trunk(a, gb, yb, h, kvm, weights, lns, seq, mem_len):
    n = a.shape[0]
    tm = TRUNK_TILE
    tiles_per_batch = seq // tm
    row = pl.BlockSpec((tm, D_MODEL), lambda i: (i, 0))
    in_specs = ([row, row, row, row, pl.BlockSpec((mem_len, 2 * D_MODEL), lambda i: (i // tiles_per_batch, 0))]
                + [_const_spec(w.shape) for w in weights] + [_const_spec(p.shape) for p in lns])
    return pl.pallas_call(
        _trunk_kernel, grid=(n // tm,), in_specs=in_specs, out_specs=row,
        out_shape=jax.ShapeDtypeStruct((n, D_MODEL), _F32),
        compiler_params=pltpu.CompilerParams(dimension_semantics=("arbitrary",), vmem_limit_bytes=VMEM_LIMIT_BYTES),
        name="trunk",
    )(a, gb, yb, h, kvm, *weights, *lns)


def _rope_constants():
    half = ROT_DIM // 2
    inv = ROPE_THETA ** (-jnp.arange(half, dtype=_F32) / half)
    d = jnp.arange(LANES) % HEAD_DIM
    inv_lane = jnp.where(d < ROT_DIM, inv[d % half], 0.0).astype(_F32)[None, :]
    neg_first = jnp.where(d < half, -1.0, 0.0).astype(_F32)[None, :]
    pos_second = jnp.where((d >= half) & (d < ROT_DIM), 1.0, 0.0).astype(_F32)[None, :]
    return inv_lane, neg_first, pos_second


def kernel(x, mem, positions, ln_in_g, ln_in_b, w_in, gmlp_ln_g, gmlp_ln_b, gmlp_ws, gmlp_bs, cmp_k_pe, cmp_k_w1, cmp_k_b1, cmp_k_w2, cmp_k_b2, cmp_v_pe, cmp_v_w1, cmp_v_b1, cmp_v_w2, cmp_v_b2, w_out, ln1_g, ln1_b, w_xq, w_xkv, w_xo, ln2_g, ln2_b, w_ff1, w_ff2, ln3_g, ln3_b):
    batch, seq, _ = x.shape
    mem_len = mem.shape[1]
    n = batch * seq
    n_slc = seq // SLC_BLOCK
    assert w_in.shape[0] == 1, "one layer"
    assert seq % SEL_CHUNK == 0 and TRIP_CHUNKS[-1] == 2 and seq >= WIN_SPAN and seq & (seq - 1) == 0 and n_slc <= LANES

    rope_consts = _rope_constants()
    vec = lambda p: p.reshape(1, -1)

    wi = w_in[0]
    o_u, o_v, o_q, o_kv, o_gn, o_ga, o_gb = (0, 1024, 2048, 3072, 3072 + 6 * KV_DIM, 3120 + 6 * KV_DIM, 4144 + 6 * KV_DIM)
    w_gn = wi[:, o_gn:o_ga].reshape(D_MODEL, KV_HEADS, 3 * GROUP)
    w_gn = jnp.pad(w_gn, ((0, 0), (0, 0), (0, LANES - 3 * GROUP))).reshape(D_MODEL, KV_HEADS * LANES)
    w_parts = [wi[:, o_u:o_v], wi[:, o_v:o_q], wi[:, o_ga:o_gb], wi[:, o_q:o_kv], wi[:, o_gb:], wi[:, o_kv:o_gn], w_gn]
    w_parts = [w.astype(_BF16) for w in w_parts]

    h, a, gb, qt, kvc, ks, vst, kw, vwt, gnt = _project(
        x.reshape(n, D_MODEL), positions.reshape(n, 1), vec(ln_in_g), vec(ln_in_b), w_parts,
        vec(gmlp_ln_g[0]), vec(gmlp_ln_b[0]), gmlp_ws[0], gmlp_bs[0].T, rope_consts, seq)

    rows = seq // CMP_STRIDE
    flat = CMP_STRIDE * HEAD_DIM
    pos_end = jnp.pad(positions[:, CMP_BLOCK - 1::CMP_STRIDE], ((0, 0), (0, 1)))[:, :, None]
    pad_lanes = lambda w: jnp.pad(w, ((0, 0), (0, LANES - HEAD_DIM)))
    kvcmp, kvcmp_t = _compress(
        kvc.reshape(2 * KV_HEADS, batch, rows, flat), pos_end,
        jnp.stack([cmp_k_pe[0].reshape(2, flat), cmp_v_pe[0].reshape(2, flat)]),
        jnp.stack([cmp_k_w1[0], cmp_v_w1[0]]).astype(_BF16),
        jnp.stack([vec(cmp_k_b1[0]), vec(cmp_v_b1[0])]),
        jnp.stack([pad_lanes(cmp_k_w2[0]), pad_lanes(cmp_v_w2[0])]).astype(_BF16),
        jnp.stack([pad_lanes(vec(cmp_k_b2[0])), pad_lanes(vec(cmp_v_b2[0]))]),
        rope_consts, batch)

    yb = _nsa(qt, gnt, kvcmp, kvcmp_t, ks, vst, kw, vwt, batch, seq)

    kvm = _memkv(mem.reshape(batch * mem_len, D_MODEL), w_xkv[0].astype(_BF16))
    weights = [w.astype(_BF16) for w in (w_out[0], w_xq[0], w_xo[0], w_ff1[0], w_ff2[0])]
    lns = [vec(p[0]) for p in (ln1_g, ln1_b, ln2_g, ln2_b, ln3_g, ln3_b)]
    out = _trunk(a, gb, yb, h, kvm, weights, lns, seq, mem_len)
    return out.reshape(batch, seq, D_MODEL)
```

```python
import functools
import math

import jax
import jax.numpy as jnp
import numpy as np
from jax import lax
from jax.experimental import pallas as pl
from jax.experimental.pallas import tpu as pltpu

D_MODEL = 1024
LN_EPS = 1e-5
ALPHA = 2.0 ** 0.25
ROPE_THETA = 500000.0
GMLP_GROUPS = 8
GMLP_CHUNK = 128
NSA_HEADS = 16
HEAD_DIM = 64
KV_HEADS = 4
GROUP = NSA_HEADS // KV_HEADS
KV_DIM = KV_HEADS * HEAD_DIM
ROT_DIM = HEAD_DIM // 4
CMP_BLOCK = 32
CMP_STRIDE = 16
CMP_HIDDEN = 4 * HEAD_DIM
SLC_BLOCK = 64
N_SELECT = 16
WINDOW = 512
Q_BLOCK = 128
KV_CHUNK = 128
V_ROWS = 80
XATTN_HEADS = 4
XATTN_HEAD_DIM = D_MODEL // XATTN_HEADS
D_FF = 4 * D_MODEL

LANES = 128
PROJ_TILE = 512
PROJ_SPLIT = 2
TRUNK_TILE = 512
TRUNK_SPLIT = 2
SEL_CHUNK = 512
TRIP_CHUNKS = (8, 4, 2)
HEADS_PER_STEP = 1
WIN_SPAN = WINDOW + Q_BLOCK
SELECT_BLOCKS = (32, 64, 96, 128)
WIN_PIECES = 5
MASK_VALUE = -1e30
LOG2_E = math.log2(math.e)
VMEM_LIMIT_BYTES = 56 * 1024 * 1024

_F32 = jnp.float32
_BF16 = jnp.bfloat16


def _layer_norm(x, g, b):
    mu = jnp.mean(x, axis=-1, keepdims=True)
    xc = x - mu
    var = jnp.mean(xc * xc, axis=-1, keepdims=True)
    return xc * lax.rsqrt(var + LN_EPS) * g + b


def _dot(a, b):
    return jnp.dot(a, b, preferred_element_type=_F32)


def _dot_nt(a, b):
    return lax.dot_general(a, b, (((1,), (1,)), ((), ())), preferred_element_type=_F32)


def _rope_tables(pos_f32, inv_lane, neg_first, pos_second):
    ang = pos_f32 * inv_lane
    c = jnp.cos(ang)
    s = jnp.sin(ang)
    return c, s * neg_first, s * pos_second


def _rope_block(x, tables):
    c, s_first, s_second = tables
    half = ROT_DIM // 2
    up = pltpu.roll(x, LANES - half, axis=1)
    down = pltpu.roll(x, half, axis=1)
    return x * c + up * s_first + down * s_second


def _const_spec(shape):
    nd = len(shape)
    return pl.BlockSpec(shape, lambda *_: (0,) * nd, pipeline_mode=pl.Buffered(1))


def _proj_kernel(x_ref, pos_ref, lng_ref, lnb_ref, wu_ref, wv_ref, wga_ref, wq_ref, wgb_ref, wkv_ref, wgn_ref,
                 glng_ref, glnb_ref, ws_ref, bst_ref, inv_ref, m1_ref, m2_ref,
                 h_ref, a_ref, gb_ref, qt_ref, kvc_ref, ks_ref, vst_ref, kw_ref, vwt_ref, gnt_ref, cmp_ref, *, seq):
    tm = x_ref.shape[0]
    part = tm // PROJ_SPLIT
    ti = lax.broadcasted_iota(jnp.int32, (GMLP_CHUNK, GMLP_CHUNK), 0)
    si = lax.broadcasted_iota(jnp.int32, (GMLP_CHUNK, GMLP_CHUNK), 1)
    w_tril = [jnp.where(si <= ti, ws_ref[g], 0.0).astype(_BF16) for g in range(GMLP_GROUPS)]
    bst = bst_ref[...]
    q_scale = HEAD_DIM ** -0.5 * LOG2_E
    heads_per_slab = LANES // HEAD_DIM
    blk_lane = lax.broadcasted_iota(jnp.int32, (part, LANES), 1)
    ones_col = (lax.broadcasted_iota(jnp.int32, (part, HEAD_DIM), 1) == 0).astype(_F32)

    def stages(r0):
        rs = slice(r0, r0 + part)
        h = _layer_norm(x_ref[rs, :], lng_ref[...], lnb_ref[...])
        h_ref[rs, :] = h
        hb = h.astype(_BF16)
        zu = _dot(hb, wu_ref[...])
        yield
        u_act = jax.nn.gelu(zu)
        zv = _dot(hb, wv_ref[...])
        yield
        vn = _layer_norm(jax.nn.gelu(zv), glng_ref[...], glnb_ref[...]).astype(_BF16)
        zga = _dot(hb, wga_ref[...])
        yield
        ua = u_act * jax.nn.sigmoid(zga)
        for g in range(GMLP_GROUPS):
            cs = slice(g * LANES, (g + 1) * LANES)
            for c in range(part // GMLP_CHUNK):
                cr = slice(c * GMLP_CHUNK, (c + 1) * GMLP_CHUNK)
                mixed = _dot(w_tril[g], vn[cr, cs]) + bst[:, g:g + 1]
                a_ref[r0 + c * GMLP_CHUNK:r0 + (c + 1) * GMLP_CHUNK, cs] = ua[cr, cs] * mixed
        zgb = _dot(hb, wgb_ref[...])
        yield
        gb_ref[rs, :] = jax.nn.sigmoid(zgb)
        zgn = _dot(hb, wgn_ref[...])
        yield
        gnt_ref[:, rs] = jax.nn.sigmoid(zgn).T
        zq = _dot(hb, wq_ref[...])
        yield
        tables = _rope_tables(pos_ref[rs, :].astype(_F32), inv_ref[...], m1_ref[...], m2_ref[...])
        for i in range(D_MODEL // LANES):
            cs = slice(i * LANES, (i + 1) * LANES)
            qt_ref[cs, rs] = (_rope_block(zq[:, cs], tables) * q_scale).T.astype(_BF16)
        zkv = _dot(hb, wkv_ref[...])
        yield
        for sl in range(2 * KV_DIM // LANES):
            cmp_ref[sl, rs, :] = zkv[:, sl * LANES:(sl + 1) * LANES]
        out_rows = slice(r0 // CMP_STRIDE, (r0 + part) // CMP_STRIDE)
        for sl in range(2 * KV_DIM // LANES):
            for l in range(CMP_STRIDE):
                pair = cmp_ref[sl, pl.ds(r0 + l, part // CMP_STRIDE, stride=CMP_STRIDE), :]
                for hh in range(heads_per_slab):
                    kvc_ref[sl * heads_per_slab + hh, out_rows, l * HEAD_DIM:(l + 1) * HEAD_DIM] = (
                        pair[:, hh * HEAD_DIM:(hh + 1) * HEAD_DIM])
        seq_pos = (pl.program_id(0) * tm + r0 + lax.broadcasted_iota(jnp.int32, (part, LANES), 0)) & (seq - 1)
        blk_onehot = jnp.where(lax.shift_right_logical(seq_pos, SLC_BLOCK.bit_length() - 1) == blk_lane, 1.0, 0.0)
        k_tail = jnp.concatenate([jnp.zeros((part, HEAD_DIM), _BF16), blk_onehot.astype(_BF16)], axis=1)
        for base, k_out, vt_out in ((2 * KV_DIM, ks_ref, vst_ref), (4 * KV_DIM, kw_ref, vwt_ref)):
            for i in range(KV_DIM // LANES):
                kr = _rope_block(zkv[:, base + i * LANES: base + (i + 1) * LANES], tables).astype(_BF16)
                for hh in range(heads_per_slab):
                    k_head = kr[:, hh * HEAD_DIM:(hh + 1) * HEAD_DIM]
                    if k_out is ks_ref:
                        k_out[2 * i + hh, rs, :] = jnp.concatenate([k_head, k_tail], axis=1)
                    else:
                        k_out[2 * i + hh, rs, :] = k_head
            for hh in range(KV_HEADS):
                v = zkv[:, base + KV_DIM + hh * HEAD_DIM: base + KV_DIM + (hh + 1) * HEAD_DIM]
                v_ext = jnp.concatenate([v, ones_col], axis=1)
                for c in range(part // KV_CHUNK):
                    vt_out[hh, r0 // KV_CHUNK + c] = v_ext[c * KV_CHUNK:(c + 1) * KV_CHUNK, :].T[:V_ROWS].astype(_BF16)
        yield

    for _ in zip(*[stages(i * part) for i in range(PROJ_SPLIT)]):
        pass


def _project(x2, pos2, ln_g, ln_b, w_parts, gln_g, gln_b, ws, bst, rope_consts, seq):
    n = x2.shape[0]
    tm = PROJ_TILE
    row = lambda shape: pl.BlockSpec(shape, lambda i: (i, 0))
    col = lambda shape: pl.BlockSpec(shape, lambda i: (0, i))
    head_rows = lambda nh, r, w: pl.BlockSpec((nh, r, w), lambda i: (0, i, 0))
    vt_spec = pl.BlockSpec((KV_HEADS, tm // KV_CHUNK, V_ROWS, KV_CHUNK), lambda i: (0, i, 0, 0))
    w_specs = [_const_spec(w.shape) for w in w_parts]
    in_specs = ([row((tm, D_MODEL)), row((tm, 1)), _const_spec(ln_g.shape), _const_spec(ln_b.shape)] + w_specs
                + [_const_spec(gln_g.shape), _const_spec(gln_b.shape), _const_spec(ws.shape), _const_spec(bst.shape)]
                + [_const_spec(c.shape) for c in rope_consts])
    out_shape = (
        jax.ShapeDtypeStruct((n, D_MODEL), _F32),
        jax.ShapeDtypeStruct((n, D_MODEL), _F32),
        jax.ShapeDtypeStruct((n, D_MODEL), _F32),
        jax.ShapeDtypeStruct((D_MODEL, n), _BF16),
        jax.ShapeDtypeStruct((2 * KV_HEADS, n // CMP_STRIDE, CMP_STRIDE * HEAD_DIM), _F32),
        jax.ShapeDtypeStruct((KV_HEADS, n, 2 * LANES), _BF16),
        jax.ShapeDtypeStruct((KV_HEADS, n // KV_CHUNK, V_ROWS, KV_CHUNK), _BF16),
        jax.ShapeDtypeStruct((KV_HEADS, n, HEAD_DIM), _BF16),
        jax.ShapeDtypeStruct((KV_HEADS, n // KV_CHUNK, V_ROWS, KV_CHUNK), _BF16),
        jax.ShapeDtypeStruct((LANES, n), _F32),
    )
    out_specs = (row((tm, D_MODEL)), row((tm, D_MODEL)), row((tm, D_MODEL)), col((D_MODEL, tm)),
                 head_rows(2 * KV_HEADS, tm // CMP_STRIDE, CMP_STRIDE * HEAD_DIM),
                 head_rows(KV_HEADS, tm, 2 * LANES), vt_spec, head_rows(KV_HEADS, tm, HEAD_DIM), vt_spec,
                 col((LANES, tm)))
    return pl.pallas_call(
        functools.partial(_proj_kernel, seq=seq), grid=(n // tm,), in_specs=in_specs,
        out_specs=out_specs, out_shape=out_shape,
        scratch_shapes=[pltpu.VMEM((2 * KV_DIM // LANES, tm, LANES), _F32)],
        compiler_params=pltpu.CompilerParams(dimension_semantics=("arbitrary",), vmem_limit_bytes=VMEM_LIMIT_BYTES),
        name="nsa_proj",
    )(x2, pos2, ln_g, ln_b, *w_parts, gln_g, gln_b, ws, bst, *rope_consts)


def _compress_kernel(x_ref, pos_ref, pe_ref, w1_ref, b1_ref, w2_ref, b2_ref, inv_ref, m1_ref, m2_ref,
                     o_ref, ot_ref, perm_ref):
    rows = x_ref.shape[0]
    is_key = (pl.program_id(1) == 0).astype(_F32)
    x = x_ref[...]
    half = CMP_STRIDE * HEAD_DIM
    ya = _dot((x + pe_ref[0:1, :]).astype(_BF16), w1_ref[0:half, :])
    yb = _dot((x + pe_ref[1:2, :]).astype(_BF16), w1_ref[half:2 * half, :])
    hid = jax.nn.gelu(ya + pltpu.roll(yb, rows - 1, axis=0) + b1_ref[...])
    out = _dot(hid.astype(_BF16), w2_ref[...]) + b2_ref[...]
    tables = _rope_tables(pos_ref[...].astype(_F32) * is_key, inv_ref[...], m1_ref[...], m2_ref[...])
    perm_ref[...] = _rope_block(out, tables)
    groups = rows // 4
    for r in range(4):
        part = perm_ref[pl.ds(r, groups, stride=4), :]
        o_ref[r * groups:(r + 1) * groups, :] = part[:, :HEAD_DIM].astype(_BF16)
        ot_ref[:, r * groups:(r + 1) * groups] = part.T[:HEAD_DIM, :].astype(_BF16)


def _compress(kvc, pos_end, pe, w1, b1, w2, b2, rope_consts, batch):
    rows = kvc.shape[2]
    flat = CMP_STRIDE * HEAD_DIM
    in_specs = [
        pl.BlockSpec((None, None, rows, flat), lambda b, kv, hh: (kv * KV_HEADS + hh, b, 0, 0)),
        pl.BlockSpec((None, rows, 1), lambda b, kv, hh: (b, 0, 0)),
        pl.BlockSpec((None, 2, flat), lambda b, kv, hh: (kv, 0, 0)),
        pl.BlockSpec((None, 2 * flat, CMP_HIDDEN), lambda b, kv, hh: (kv, 0, 0)),
        pl.BlockSpec((None, 1, CMP_HIDDEN), lambda b, kv, hh: (kv, 0, 0)),
        pl.BlockSpec((None, CMP_HIDDEN, LANES), lambda b, kv, hh: (kv, 0, 0)),
        pl.BlockSpec((None, 1, LANES), lambda b, kv, hh: (kv, 0, 0)),
    ] + [pl.BlockSpec(c.shape, lambda b, kv, hh: (0, 0)) for c in rope_consts]
    return pl.pallas_call(
        _compress_kernel, grid=(batch, 2, KV_HEADS), in_specs=in_specs,
        out_specs=(pl.BlockSpec((None, None, None, rows, HEAD_DIM), lambda b, kv, hh: (kv, b, hh, 0, 0)),
                   pl.BlockSpec((None, None, None, HEAD_DIM, rows), lambda b, kv, hh: (kv, b, hh, 0, 0))),
        out_shape=(jax.ShapeDtypeStruct((2, batch, KV_HEADS, rows, HEAD_DIM), _BF16),
                   jax.ShapeDtypeStruct((2, batch, KV_HEADS, HEAD_DIM, rows), _BF16)),
        scratch_shapes=[pltpu.VMEM((rows, LANES), _F32)],
        compiler_params=pltpu.CompilerParams(dimension_semantics=("arbitrary",) * 3, vmem_limit_bytes=VMEM_LIMIT_BYTES),
        name="nsa_compress",
    )(kvc, pos_end, pe, w1, b1, w2, b2, *rope_consts)


def _per_head(fn, s, *shared):
    return jnp.concatenate([fn(s[:, g * Q_BLOCK:(g + 1) * Q_BLOCK], *shared) for g in range(GROUP)], axis=1)


def _nsa_step(qt_ref, gnt_ref, kc_ref, vct_ref, ks_ref, vst_ref, kw_ref, vwt_ref, cend_ref, wbias_ref,
              o_ref, sa_ref, sb_ref, qx_ref, *, n_sel, n_blocks):
    rows = kc_ref.shape[1]
    n_slc = rows // 4
    seq = ks_ref.shape[1]
    heads = range(HEADS_PER_STEP)
    qb = pl.program_id(2)
    q0 = qb * Q_BLOCK
    t = q0 + lax.broadcasted_iota(jnp.int32, (1, Q_BLOCK), 1)
    w0 = pl.multiple_of(jnp.maximum(q0 - WINDOW, 0), KV_CHUNK)
    wc = lax.shift_right_logical(w0, KV_CHUNK.bit_length() - 1)
    n_seq_chunks = seq // SEL_CHUNK
    sub = SEL_CHUNK // KV_CHUNK
    piece = WIN_SPAN // WIN_PIECES

    valid_c = cend_ref[...] <= t
    has_c = t >= CMP_BLOCK - 1
    blk = lax.broadcasted_iota(jnp.int32, (n_slc, 1), 0)
    cur = lax.shift_right_logical(t, SLC_BLOCK.bit_length() - 1)
    forced = (blk == 0) | (blk == cur) | (blk == cur - 1)
    bi = lax.broadcasted_iota(jnp.int32, (n_slc, Q_BLOCK), 0)
    bf = bi.astype(_F32)
    past = bi < lax.shift_right_logical(q0, SLC_BLOCK.bit_length() - 1)
    bias_w = wbias_ref[jnp.minimum(qb, WINDOW // Q_BLOCK)]
    causal = jnp.where(lax.broadcasted_iota(jnp.int32, (Q_BLOCK, Q_BLOCK), 0)
                       <= lax.broadcasted_iota(jnp.int32, (Q_BLOCK, Q_BLOCK), 1), 0.0, MASK_VALUE)
    out = [None] * HEADS_PER_STEP

    def before_loop(h):
        qt = qt_ref[h * GROUP * HEAD_DIM:(h + 1) * GROUP * HEAD_DIM, :]
        q_t = jnp.concatenate([qt[g * HEAD_DIM:(g + 1) * HEAD_DIM, :] for g in range(GROUP)], axis=1)
        raw_c = _dot(kc_ref[h], q_t)
        raw_w = _dot(kw_ref[h, pl.ds(w0, WIN_SPAN), :], q_t)
        raw_d = _dot(ks_ref[h, pl.ds(q0, Q_BLOCK), :HEAD_DIM], q_t)
        raw_0 = _dot(ks_ref[h, 0:SEL_CHUNK, :HEAD_DIM], q_t)
        yield

        s_w = _per_head(lambda s, b: s + b, raw_w, bias_w)
        m_w = jnp.max(s_w, axis=0, keepdims=True)
        s_d = _per_head(lambda s, b: s + b, raw_d, causal)
        m_d = jnp.max(s_d, axis=0, keepdims=True)

        def select(nb):
            take = lambda x: x if nb == n_slc else jnp.concatenate(
                [x[r * n_slc:r * n_slc + nb] for r in range(4)], axis=0)

            s_c = _per_head(lambda s, v: jnp.where(v, s, MASK_VALUE), take(raw_c), take(valid_c))
            e_c = jnp.exp2(s_c - jnp.max(s_c, axis=0, keepdims=True))
            inv_c = _per_head(lambda l, ok: jnp.where(ok, 1.0 / l, 0.0), jnp.sum(e_c, axis=0, keepdims=True), has_c)
            p_c = e_c * inv_c
            p_mm = p_c.astype(_BF16)
            if nb < n_slc:
                gap = jnp.zeros((n_slc - nb, GROUP * Q_BLOCK), _BF16)
                p_mm = jnp.concatenate([x for r in range(4) for x in (p_mm[r * nb:(r + 1) * nb], gap)], axis=0)
            o_c = _dot(vct_ref[h], p_mm)

            p_sum = p_c[:, :Q_BLOCK]
            for g in range(1, GROUP):
                p_sum = p_sum + p_c[:, g * Q_BLOCK:(g + 1) * Q_BLOCK]
            parts = [p_sum[r * nb:(r + 1) * nb, :] for r in range(4)]
            prev3 = jnp.where(blk[:nb] == 0, 0.0, pltpu.roll(parts[3], 1, axis=0))
            p_slc = parts[0] + parts[1] + parts[2] + parts[3] + prev3

            st = jnp.where(forced[:nb], -jnp.inf, jnp.where(blk[:nb] <= cur, p_slc, -1.0))
            p_w_parts = []
            for r in range(n_sel - 3):
                mx = jnp.max(st, axis=0, keepdims=True)
                first = jnp.min(jnp.where(st == mx, bf[:nb], float(n_slc)), axis=0, keepdims=True)
                if r < WIN_PIECES:
                    part = jnp.exp2(s_w[r * piece:(r + 1) * piece, :] - m_w)
                    p_w_parts.append(part.astype(_BF16))
                    col_max = jnp.max(part, axis=0, keepdims=True)
                    tie = col_max[:, :Q_BLOCK]
                    for g in range(1, GROUP):
                        tie = jnp.maximum(tie, col_max[:, g * Q_BLOCK:(g + 1) * Q_BLOCK])
                    first = first + tie * 0.0
                st = jnp.where(bf[:nb] == first, -jnp.inf, st)
            sel = st == -jnp.inf
            pen = jnp.where(sel & past[:nb], 0.0, MASK_VALUE)
            if nb < n_slc:
                pen = jnp.concatenate([pen, jnp.full((n_slc - nb, Q_BLOCK), MASK_VALUE, _F32)], axis=0)
            return (o_c, pen) + tuple(p_w_parts)

        o_c, pen, *p_w_parts = select(n_blocks)
        yield
        p_d = jnp.exp2(s_d - m_d).astype(_BF16)
        if n_slc < LANES:
            pen = jnp.concatenate([pen, jnp.zeros((LANES - n_slc, Q_BLOCK), _F32)], axis=0)
        pen_heads = jnp.concatenate([pen] * GROUP, axis=1)
        for b in range(SEL_CHUNK // SLC_BLOCK):
            ks_rows = slice(b * SLC_BLOCK, (b + 1) * SLC_BLOCK)
            sa_ref[h, ks_rows, :] = raw_0[ks_rows, :] + pen_heads[b:b + 1, :]
        qx_ref[h, 0:HEAD_DIM, :] = q_t
        qx_ref[h, HEAD_DIM:LANES, :] = jnp.zeros((LANES - HEAD_DIM, GROUP * Q_BLOCK), _BF16)
        qx_ref[h, LANES:2 * LANES, :] = jnp.concatenate([pen.astype(_BF16)] * GROUP, axis=1)
        yield

        v_w = jnp.concatenate([vwt_ref[h, wc + i] for i in range(WIN_SPAN // KV_CHUNK)], axis=1)
        acc_w = _dot(v_w, jnp.concatenate(p_w_parts, axis=0))
        acc_d = _dot(vst_ref[h, qb], p_d)
        yield
        o_w = acc_w[:HEAD_DIM] * (1.0 / acc_w[HEAD_DIM:HEAD_DIM + 1])
        out[h] = (o_c, o_w, m_d, acc_d)
        yield

    for _ in zip(*[before_loop(h) for h in heads]):
        pass

    def sel_scores(h, c, buf):
        k0 = pl.multiple_of(jnp.minimum(c, n_seq_chunks - 1) * SEL_CHUNK, SEL_CHUNK)
        buf[h] = _dot(ks_ref[h, pl.ds(k0, SEL_CHUNK), :], qx_ref[h])

    def sel_consume(h, c, buf, m, acc):
        c0 = jnp.minimum(c, n_seq_chunks - 1) * sub
        s = buf[h]
        m_new = jnp.maximum(m, jnp.max(s, axis=0, keepdims=True))
        p = jnp.exp2(s - m_new)
        v_t = jnp.concatenate([vst_ref[h, c0 + i] for i in range(sub)], axis=1)
        acc = jnp.exp2(m - m_new) * acc + _dot(v_t, p.astype(_BF16))
        return m_new, acc

    bufs = (sa_ref, sb_ref)

    def sel_trip(per_trip, base):
        def body(i, carry):
            carry = list(carry)
            for j in range(per_trip):
                c = base + per_trip * i + j
                for h in heads:
                    sel_scores(h, c + 1, bufs[(j + 1) % 2])
                for h in heads:
                    carry[h] = sel_consume(h, c, bufs[j % 2], *carry[h])
            return tuple(carry)
        return body

    n_chunks = (q0 + SEL_CHUNK - 1) // SEL_CHUNK
    carry, done = tuple((out[h][2], out[h][3]) for h in heads), 0
    for per_trip in TRIP_CHUNKS:
        left = n_chunks - done
        trips = (left + 1) // 2 if per_trip == TRIP_CHUNKS[-1] else left // per_trip
        carry = lax.fori_loop(0, trips, sel_trip(per_trip, done), carry)
        done = done + trips * per_trip

    for h in heads:
        o_c, o_w = out[h][0], out[h][1]
        acc_s = carry[h][1]
        o_s = acc_s[:HEAD_DIM] * (1.0 / acc_s[HEAD_DIM:HEAD_DIM + 1])
        first_gate = 3 * GROUP * (pl.program_id(1) * HEADS_PER_STEP + h)
        gate = lambda g, k: gnt_ref[pl.ds(first_gate + 3 * g + k, 1), :]
        outs = []
        for g in range(GROUP):
            ls = slice(g * Q_BLOCK, (g + 1) * Q_BLOCK)
            outs.append(gate(g, 0) * o_c[:, ls] + gate(g, 1) * o_s[:, ls] + gate(g, 2) * o_w[:, ls])
        o_ref[:, h * GROUP * HEAD_DIM:(h + 1) * GROUP * HEAD_DIM] = jnp.concatenate(outs, axis=0).T


def _nsa_kernel(*refs, n_sel, ranges):
    qb = pl.program_id(2)
    for first, stop, n_blocks in ranges:
        @pl.when((qb >= first) & (qb < stop))
        def _():
            _nsa_step(*refs, n_sel=n_sel, n_blocks=n_blocks)


def _nsa(qt, gnt, kc, vct, ks, vst, kw, vwt, batch, seq):
    n = qt.shape[1]
    n_qb = seq // Q_BLOCK
    rows = kc.shape[3]
    n_slc = seq // SLC_BLOCK
    hps = HEADS_PER_STEP
    seq_rows = lambda w: pl.BlockSpec((hps, seq, w), lambda b, hp, i: (hp, b, 0))
    vt_spec = pl.BlockSpec((hps, seq // KV_CHUNK, V_ROWS, KV_CHUNK), lambda b, hp, i: (hp, b, 0, 0))
    slot = np.arange(rows)
    cmp_idx = 4 * (slot % n_slc) + slot // n_slc
    cmp_end = np.where(cmp_idx < rows - 1, CMP_STRIDE * cmp_idx + CMP_BLOCK - 1, np.iinfo(np.int32).max)
    cmp_end = np.broadcast_to(cmp_end.astype(np.int32)[:, None], (rows, Q_BLOCK))
    off = Q_BLOCK * np.arange(WINDOW // Q_BLOCK + 1)[:, None, None]
    diff = off + np.arange(Q_BLOCK)[None, None, :] - np.arange(WIN_SPAN)[None, :, None]
    win_bias = np.where((diff >= 0) & (diff < WINDOW), 0.0, MASK_VALUE).astype(np.float32)
    counts = sorted({min(nb, n_slc) for nb in SELECT_BLOCKS})
    starts = [0] + [nb // 2 for nb in counts[:-1]]
    ranges = tuple(zip(starts, starts[1:] + [n_qb], counts))
    qcol = lambda r: pl.BlockSpec((hps * r, Q_BLOCK), lambda b, hp, i: (hp, b * n_qb + i))
    in_specs = [
        qcol(GROUP * HEAD_DIM), pl.BlockSpec((LANES, Q_BLOCK), lambda b, hp, i: (0, b * n_qb + i)),
        pl.BlockSpec((None, None, hps, rows, HEAD_DIM), lambda b, hp, i: (0, b, hp, 0, 0)),
        pl.BlockSpec((None, None, hps, HEAD_DIM, rows), lambda b, hp, i: (1, b, hp, 0, 0)),
        seq_rows(2 * LANES), vt_spec, seq_rows(HEAD_DIM), vt_spec,
        _const_spec(cmp_end.shape), _const_spec(win_bias.shape),
    ]
    return pl.pallas_call(
        functools.partial(_nsa_kernel, n_sel=min(N_SELECT, n_slc), ranges=ranges),
        grid=(batch, KV_HEADS // hps, n_qb), in_specs=in_specs,
        out_specs=pl.BlockSpec((Q_BLOCK, hps * GROUP * HEAD_DIM), lambda b, hp, i: (b * n_qb + i, hp)),
        out_shape=jax.ShapeDtypeStruct((n, D_MODEL), _F32),
        scratch_shapes=[pltpu.VMEM((hps, SEL_CHUNK, GROUP * Q_BLOCK), _F32)] * 2
        + [pltpu.VMEM((hps, 2 * LANES, GROUP * Q_BLOCK), _BF16)],
        compiler_params=pltpu.CompilerParams(dimension_semantics=("arbitrary",) * 3, vmem_limit_bytes=VMEM_LIMIT_BYTES),
        name="nsa_attention",
    )(qt, gnt, kc, vct, ks, vst, kw, vwt, cmp_end, win_bias)


def _memkv_kernel(mem_ref, w_ref, o_ref):
    o_ref[...] = _dot(mem_ref[...].astype(_BF16), w_ref[...]).astype(_BF16)


def _memkv(mem2, w_xkv):
    m = mem2.shape[0]
    return pl.pallas_call(
        _memkv_kernel, grid=(1,),
        in_specs=[pl.BlockSpec(mem2.shape, lambda i: (0, 0)), pl.BlockSpec(w_xkv.shape, lambda i: (0, 0))],
        out_specs=pl.BlockSpec((m, 2 * D_MODEL), lambda i: (0, 0)),
        out_shape=jax.ShapeDtypeStruct((m, 2 * D_MODEL), _BF16),
        compiler_params=pltpu.CompilerParams(vmem_limit_bytes=VMEM_LIMIT_BYTES),
        name="mem_kv",
    )(mem2, w_xkv)


def _trunk_kernel(a_ref, gb_ref, yb_ref, h_ref, kvm_ref, wo_ref, wxq_ref, wxo_ref, wf1_ref, wf2_ref,
                  g1_ref, b1_ref, g2_ref, b2_ref, g3_ref, b3_ref, o_ref):
    tm = a_ref.shape[0]
    halves = [slice(i * (tm // TRUNK_SPLIT), (i + 1) * (tm // TRUNK_SPLIT)) for i in range(TRUNK_SPLIT)]
    both = lambda fn, *xs: [fn(*(x[i] for x in xs)) for i in range(TRUNK_SPLIT)]

    mix = [(a_ref[r, :] + gb_ref[r, :] * yb_ref[r, :]).astype(_BF16) for r in halves]
    y1 = both(lambda m: _dot(m, wo_ref[...]), mix)
    h1 = [_layer_norm(ALPHA * h_ref[r, :] + y, g1_ref[...], b1_ref[...]) for r, y in zip(halves, y1)]

    qx = both(lambda x: _dot(x.astype(_BF16), wxq_ref[...]).astype(_BF16), h1)
    heads = [[] for _ in range(TRUNK_SPLIT)]
    for hh in range(XATTN_HEADS):
        cs = slice(hh * XATTN_HEAD_DIM, (hh + 1) * XATTN_HEAD_DIM)
        vcs = slice(D_MODEL + hh * XATTN_HEAD_DIM, D_MODEL + (hh + 1) * XATTN_HEAD_DIM)
        s = both(lambda q: _dot_nt(q[:, cs], kvm_ref[:, cs]) * (XATTN_HEAD_DIM ** -0.5), qx)
        e = both(lambda x: jnp.exp(x - jnp.max(x, axis=-1, keepdims=True)), s)
        p = both(lambda x: (x * (1.0 / jnp.sum(x, axis=-1, keepdims=True))).astype(_BF16), e)
        for i, o in enumerate(both(lambda x: _dot(x, kvm_ref[:, vcs]), p)):
            heads[i].append(o)
    xo = both(lambda hs: jnp.concatenate(hs, axis=1).astype(_BF16), heads)
    y2 = both(lambda x: _dot(x, wxo_ref[...]), xo)
    h2 = both(lambda x, y: _layer_norm(ALPHA * x + y, g2_ref[...], b2_ref[...]), h1, y2)

    h2b = both(lambda x: x.astype(_BF16), h2)
    ff = [None] * TRUNK_SPLIT
    for c in range(D_FF // D_MODEL):
        cs = slice(c * D_MODEL, (c + 1) * D_MODEL)
        act = both(lambda x: jnp.square(jnp.maximum(_dot(x, wf1_ref[:, cs]), 0.0)).astype(_BF16), h2b)
        part = both(lambda x: _dot(x, wf2_ref[cs, :]), act)
        ff = part if c == 0 else both(lambda x, y: x + y, ff, part)
    for r, x, y in zip(halves, h2, ff):
        o_ref[r, :] = _layer_norm(ALPHA * x + y, g3_ref[...], b3_ref[...])


def _trunk(a, gb, yb, h, kvm, weights, lns, seq, mem_len):
    n = a.shape[0]
    tm = TRUNK_TILE
    tiles_per_batch = seq // tm
    row = pl.BlockSpec((tm, D_MODEL), lambda i: (i, 0))
    in_specs = ([row, row, row, row, pl.BlockSpec((mem_len, 2 * D_MODEL), lambda i: (i // tiles_per_batch, 0))]
                + [_const_spec(w.shape) for w in weights] + [_const_spec(p.shape) for p in lns])
    return pl.pallas_call(
        _trunk_kernel, grid=(n // tm,), in_specs=in_specs, out_specs=row,
        out_shape=jax.ShapeDtypeStruct((n, D_MODEL), _F32),
        compiler_params=pltpu.CompilerParams(dimension_semantics=("arbitrary",), vmem_limit_bytes=VMEM_LIMIT_BYTES),
        name="trunk",
    )(a, gb, yb, h, kvm, *weights, *lns)


def _rope_constants():
    half = ROT_DIM // 2
    inv = ROPE_THETA ** (-jnp.arange(half, dtype=_F32) / half)
    d = jnp.arange(LANES) % HEAD_DIM
    inv_lane = jnp.where(d < ROT_DIM, inv[d % half], 0.0).astype(_F32)[None, :]
    neg_first = jnp.where(d < half, -1.0, 0.0).astype(_F32)[None, :]
    pos_second = jnp.where((d >= half) & (d < ROT_DIM), 1.0, 0.0).astype(_F32)[None, :]
    return inv_lane, neg_first, pos_second


def kernel(x, mem, positions, ln_in_g, ln_in_b, w_in, gmlp_ln_g, gmlp_ln_b, gmlp_ws, gmlp_bs, cmp_k_pe, cmp_k_w1, cmp_k_b1, cmp_k_w2, cmp_k_b2, cmp_v_pe, cmp_v_w1, cmp_v_b1, cmp_v_w2, cmp_v_b2, w_out, ln1_g, ln1_b, w_xq, w_xkv, w_xo, ln2_g, ln2_b, w_ff1, w_ff2, ln3_g, ln3_b):
    batch, seq, _ = x.shape
    mem_len = mem.shape[1]
    n = batch * seq
    n_slc = seq // SLC_BLOCK
    assert w_in.shape[0] == 1, "one layer"
    assert seq % SEL_CHUNK == 0 and TRIP_CHUNKS[-1] == 2 and seq >= WIN_SPAN and seq & (seq - 1) == 0 and n_slc <= LANES

    rope_consts = _rope_constants()
    vec = lambda p: p.reshape(1, -1)

    wi = w_in[0]
    o_u, o_v, o_q, o_kv, o_gn, o_ga, o_gb = (0, 1024, 2048, 3072, 3072 + 6 * KV_DIM, 3120 + 6 * KV_DIM, 4144 + 6 * KV_DIM)
    w_gn = jnp.pad(wi[:, o_gn:o_ga], ((0, 0), (0, LANES - 3 * NSA_HEADS)))
    w_parts = [wi[:, o_u:o_v], wi[:, o_v:o_q], wi[:, o_ga:o_gb], wi[:, o_q:o_kv], wi[:, o_gb:], wi[:, o_kv:o_gn], w_gn]
    w_parts = [w.astype(_BF16) for w in w_parts]

    h, a, gb, qt, kvc, ks, vst, kw, vwt, gnt = _project(
        x.reshape(n, D_MODEL), positions.reshape(n, 1), vec(ln_in_g), vec(ln_in_b), w_parts,
        vec(gmlp_ln_g[0]), vec(gmlp_ln_b[0]), gmlp_ws[0], gmlp_bs[0].T, rope_consts, seq)

    rows = seq // CMP_STRIDE
    flat = CMP_STRIDE * HEAD_DIM
    pos_end = jnp.pad(positions[:, CMP_BLOCK - 1::CMP_STRIDE], ((0, 0), (0, 1)))[:, :, None]
    pad_lanes = lambda w: jnp.pad(w, ((0, 0), (0, LANES - HEAD_DIM)))
    kvcmp, kvcmp_t = _compress(
        kvc.reshape(2 * KV_HEADS, batch, rows, flat), pos_end,
        jnp.stack([cmp_k_pe[0].reshape(2, flat), cmp_v_pe[0].reshape(2, flat)]),
        jnp.stack([cmp_k_w1[0], cmp_v_w1[0]]).astype(_BF16),
        jnp.stack([vec(cmp_k_b1[0]), vec(cmp_v_b1[0])]),
        jnp.stack([pad_lanes(cmp_k_w2[0]), pad_lanes(cmp_v_w2[0])]).astype(_BF16),
        jnp.stack([pad_lanes(vec(cmp_k_b2[0])), pad_lanes(vec(cmp_v_b2[0]))]),
        rope_consts, batch)

    yb = _nsa(qt, gnt, kvcmp, kvcmp_t, ks, vst, kw, vwt, batch, seq)

    kvm = _memkv(mem.reshape(batch * mem_len, D_MODEL), w_xkv[0].astype(_BF16))
    weights = [w.astype(_BF16) for w in (w_out[0], w_xq[0], w_xo[0], w_ff1[0], w_ff2[0])]
    lns = [vec(p[0]) for p in (ln1_g, ln1_b, ln2_g, ln2_b, ln3_g, ln3_b)]
    out = _trunk(a, gb, yb, h, kvm, weights, lns, seq, mem_len)
    return out.reshape(batch, seq, D_MODEL)
```

```python
import functools
import math

import jax
import jax.numpy as jnp
import numpy as np
from jax import lax
from jax.experimental import pallas as pl
from jax.experimental.pallas import tpu as pltpu

D_MODEL = 1024
LN_EPS = 1e-5
ALPHA = 2.0 ** 0.25
ROPE_THETA = 500000.0
GMLP_GROUPS = 8
GMLP_CHUNK = 128
NSA_HEADS = 16
HEAD_DIM = 64
KV_HEADS = 4
GROUP = NSA_HEADS // KV_HEADS
KV_DIM = KV_HEADS * HEAD_DIM
ROT_DIM = HEAD_DIM // 4
CMP_BLOCK = 32
CMP_STRIDE = 16
CMP_HIDDEN = 4 * HEAD_DIM
SLC_BLOCK = 64
N_SELECT = 16
WINDOW = 512
Q_BLOCK = 128
KV_CHUNK = 128
V_ROWS = 80
XATTN_HEADS = 4
XATTN_HEAD_DIM = D_MODEL // XATTN_HEADS
D_FF = 4 * D_MODEL

LANES = 128
PROJ_TILE = 512
PROJ_SPLIT = 2
TRUNK_TILE = 512
TRUNK_SPLIT = 2
SEL_CHUNK = 512
TRIP_CHUNKS = (8, 4, 2)
HEADS_PER_STEP = 1
WIN_SPAN = WINDOW + Q_BLOCK
SELECT_BLOCKS = (32, 64, 96, 128)
WIN_PIECES = 5
MASK_VALUE = -1e30
LOG2_E = math.log2(math.e)
VMEM_LIMIT_BYTES = 56 * 1024 * 1024

_F32 = jnp.float32
_BF16 = jnp.bfloat16


def _layer_norm(x, g, b):
    mu = jnp.mean(x, axis=-1, keepdims=True)
    xc = x - mu
    var = jnp.mean(xc * xc, axis=-1, keepdims=True)
    return xc * lax.rsqrt(var + LN_EPS) * g + b


def _dot(a, b):
    return jnp.dot(a, b, preferred_element_type=_F32)


def _dot_nt(a, b):
    return lax.dot_general(a, b, (((1,), (1,)), ((), ())), preferred_element_type=_F32)


def _rope_tables(pos_f32, inv_lane, neg_first, pos_second):
    ang = pos_f32 * inv_lane
    c = jnp.cos(ang)
    s = jnp.sin(ang)
    return c, s * neg_first, s * pos_second


def _rope_block(x, tables):
    c, s_first, s_second = tables
    half = ROT_DIM // 2
    up = pltpu.roll(x, LANES - half, axis=1)
    down = pltpu.roll(x, half, axis=1)
    return x * c + up * s_first + down * s_second


def _const_spec(shape):
    nd = len(shape)
    return pl.BlockSpec(shape, lambda *_: (0,) * nd, pipeline_mode=pl.Buffered(1))


def _proj_kernel(x_ref, pos_ref, lng_ref, lnb_ref, wu_ref, wv_ref, wga_ref, wq_ref, wgb_ref, wkv_ref, wgn_ref,
                 glng_ref, glnb_ref, ws_ref, bst_ref, inv_ref, m1_ref, m2_ref,
                 h_ref, a_ref, gb_ref, qt_ref, kvc_ref, ks_ref, vst_ref, kw_ref, vwt_ref, gnt_ref, cmp_ref, *, seq):
    tm = x_ref.shape[0]
    part = tm // PROJ_SPLIT
    ti = lax.broadcasted_iota(jnp.int32, (GMLP_CHUNK, GMLP_CHUNK), 0)
    si = lax.broadcasted_iota(jnp.int32, (GMLP_CHUNK, GMLP_CHUNK), 1)
    w_tril = [jnp.where(si <= ti, ws_ref[g], 0.0).astype(_BF16) for g in range(GMLP_GROUPS)]
    bst = bst_ref[...]
    q_scale = HEAD_DIM ** -0.5 * LOG2_E
    heads_per_slab = LANES // HEAD_DIM
    blk_lane = lax.broadcasted_iota(jnp.int32, (part, LANES), 1)
    ones_col = (lax.broadcasted_iota(jnp.int32, (part, HEAD_DIM), 1) == 0).astype(_F32)

    def stages(r0):
        rs = slice(r0, r0 + part)
        h = _layer_norm(x_ref[rs, :], lng_ref[...], lnb_ref[...])
        h_ref[rs, :] = h
        hb = h.astype(_BF16)
        zkv = _dot(hb, wkv_ref[...])
        yield
        tables = _rope_tables(pos_ref[rs, :].astype(_F32), inv_ref[...], m1_ref[...], m2_ref[...])
        for sl in range(2 * KV_DIM // LANES):
            cmp_ref[sl, rs, :] = zkv[:, sl * LANES:(sl + 1) * LANES]
        out_rows = slice(r0 // CMP_STRIDE, (r0 + part) // CMP_STRIDE)
        for sl in range(2 * KV_DIM // LANES):
            for l in range(CMP_STRIDE):
                pair = cmp_ref[sl, pl.ds(r0 + l, part // CMP_STRIDE, stride=CMP_STRIDE), :]
                for hh in range(heads_per_slab):
                    kvc_ref[sl * heads_per_slab + hh, out_rows, l * HEAD_DIM:(l + 1) * HEAD_DIM] = (
                        pair[:, hh * HEAD_DIM:(hh + 1) * HEAD_DIM])
        seq_pos = (pl.program_id(0) * tm + r0 + lax.broadcasted_iota(jnp.int32, (part, LANES), 0)) & (seq - 1)
        blk_onehot = jnp.where(lax.shift_right_logical(seq_pos, SLC_BLOCK.bit_length() - 1) == blk_lane, 1.0, 0.0)
        k_tail = jnp.concatenate([jnp.zeros((part, HEAD_DIM), _BF16), blk_onehot.astype(_BF16)], axis=1)
        for base, k_out, vt_out in ((2 * KV_DIM, ks_ref, vst_ref), (4 * KV_DIM, kw_ref, vwt_ref)):
            for i in range(KV_DIM // LANES):
                kr = _rope_block(zkv[:, base + i * LANES: base + (i + 1) * LANES], tables).astype(_BF16)
                for hh in range(heads_per_slab):
                    k_head = kr[:, hh * HEAD_DIM:(hh + 1) * HEAD_DIM]
                    if k_out is ks_ref:
                        k_out[2 * i + hh, rs, :] = jnp.concatenate([k_head, k_tail], axis=1)
                    else:
                        k_out[2 * i + hh, rs, :] = k_head
            for hh in range(KV_HEADS):
                v = zkv[:, base + KV_DIM + hh * HEAD_DIM: base + KV_DIM + (hh + 1) * HEAD_DIM]
                v_ext = jnp.concatenate([v, ones_col], axis=1)
                for c in range(part // KV_CHUNK):
                    vt_out[hh, r0 // KV_CHUNK + c] = v_ext[c * KV_CHUNK:(c + 1) * KV_CHUNK, :].T[:V_ROWS].astype(_BF16)
        zq = _dot(hb, wq_ref[...])
        yield
        for i in range(D_MODEL // LANES):
            cs = slice(i * LANES, (i + 1) * LANES)
            qt_ref[cs, rs] = (_rope_block(zq[:, cs], tables) * q_scale).T.astype(_BF16)
        zu = _dot(hb, wu_ref[...])
        yield
        u_act = jax.nn.gelu(zu)
        zv = _dot(hb, wv_ref[...])
        yield
        vn = _layer_norm(jax.nn.gelu(zv), glng_ref[...], glnb_ref[...]).astype(_BF16)
        zga = _dot(hb, wga_ref[...])
        yield
        ua = u_act * jax.nn.sigmoid(zga)
        for g in range(GMLP_GROUPS):
            cs = slice(g * LANES, (g + 1) * LANES)
            for c in range(part // GMLP_CHUNK):
                cr = slice(c * GMLP_CHUNK, (c + 1) * GMLP_CHUNK)
                mixed = _dot(w_tril[g], vn[cr, cs]) + bst[:, g:g + 1]
                a_ref[r0 + c * GMLP_CHUNK:r0 + (c + 1) * GMLP_CHUNK, cs] = ua[cr, cs] * mixed
        zgn = _dot(hb, wgn_ref[...])
        yield
        gnt_ref[:, rs] = jax.nn.sigmoid(zgn).T
        zgb = _dot(hb, wgb_ref[...])
        yield
        gb_ref[rs, :] = jax.nn.sigmoid(zgb)
        yield

    for _ in zip(*[stages(i * part) for i in range(PROJ_SPLIT)]):
        pass


def _project(x2, pos2, ln_g, ln_b, w_parts, gln_g, gln_b, ws, bst, rope_consts, seq):
    n = x2.shape[0]
    tm = PROJ_TILE
    row = lambda shape: pl.BlockSpec(shape, lambda i: (i, 0))
    col = lambda shape: pl.BlockSpec(shape, lambda i: (0, i))
    head_rows = lambda nh, r, w: pl.BlockSpec((nh, r, w), lambda i: (0, i, 0))
    vt_spec = pl.BlockSpec((KV_HEADS, tm // KV_CHUNK, V_ROWS, KV_CHUNK), lambda i: (0, i, 0, 0))
    w_specs = [_const_spec(w.shape) for w in w_parts]
    in_specs = ([row((tm, D_MODEL)), row((tm, 1)), _const_spec(ln_g.shape), _const_spec(ln_b.shape)] + w_specs
                + [_const_spec(gln_g.shape), _const_spec(gln_b.shape), _const_spec(ws.shape), _const_spec(bst.shape)]
                + [_const_spec(c.shape) for c in rope_consts])
    out_shape = (
        jax.ShapeDtypeStruct((n, D_MODEL), _F32),
        jax.ShapeDtypeStruct((n, D_MODEL), _F32),
        jax.ShapeDtypeStruct((n, D_MODEL), _F32),
        jax.ShapeDtypeStruct((D_MODEL, n), _BF16),
        jax.ShapeDtypeStruct((2 * KV_HEADS, n // CMP_STRIDE, CMP_STRIDE * HEAD_DIM), _F32),
        jax.ShapeDtypeStruct((KV_HEADS, n, 2 * LANES), _BF16),
        jax.ShapeDtypeStruct((KV_HEADS, n // KV_CHUNK, V_ROWS, KV_CHUNK), _BF16),
        jax.ShapeDtypeStruct((KV_HEADS, n, HEAD_DIM), _BF16),
        jax.ShapeDtypeStruct((KV_HEADS, n // KV_CHUNK, V_ROWS, KV_CHUNK), _BF16),
        jax.ShapeDtypeStruct((LANES, n), _F32),
    )
    out_specs = (row((tm, D_MODEL)), row((tm, D_MODEL)), row((tm, D_MODEL)), col((D_MODEL, tm)),
                 head_rows(2 * KV_HEADS, tm // CMP_STRIDE, CMP_STRIDE * HEAD_DIM),
                 head_rows(KV_HEADS, tm, 2 * LANES), vt_spec, head_rows(KV_HEADS, tm, HEAD_DIM), vt_spec,
                 col((LANES, tm)))
    return pl.pallas_call(
        functools.partial(_proj_kernel, seq=seq), grid=(n // tm,), in_specs=in_specs,
        out_specs=out_specs, out_shape=out_shape,
        scratch_shapes=[pltpu.VMEM((2 * KV_DIM // LANES, tm, LANES), _F32)],
        compiler_params=pltpu.CompilerParams(dimension_semantics=("arbitrary",), vmem_limit_bytes=VMEM_LIMIT_BYTES),
        name="nsa_proj",
    )(x2, pos2, ln_g, ln_b, *w_parts, gln_g, gln_b, ws, bst, *rope_consts)


def _compress_kernel(x_ref, pos_ref, pe_ref, w1_ref, b1_ref, w2_ref, b2_ref, inv_ref, m1_ref, m2_ref,
                     o_ref, ot_ref, perm_ref):
    rows = x_ref.shape[0]
    is_key = (pl.program_id(1) == 0).astype(_F32)
    x = x_ref[...]
    half = CMP_STRIDE * HEAD_DIM
    ya = _dot((x + pe_ref[0:1, :]).astype(_BF16), w1_ref[0:half, :])
    yb = _dot((x + pe_ref[1:2, :]).astype(_BF16), w1_ref[half:2 * half, :])
    hid = jax.nn.gelu(ya + pltpu.roll(yb, rows - 1, axis=0) + b1_ref[...])
    out = _dot(hid.astype(_BF16), w2_ref[...]) + b2_ref[...]
    tables = _rope_tables(pos_ref[...].astype(_F32) * is_key, inv_ref[...], m1_ref[...], m2_ref[...])
    perm_ref[...] = _rope_block(out, tables)
    groups = rows // 4
    for r in range(4):
        part = perm_ref[pl.ds(r, groups, stride=4), :]
        o_ref[r * groups:(r + 1) * groups, :] = part[:, :HEAD_DIM].astype(_BF16)
        ot_ref[:, r * groups:(r + 1) * groups] = part.T[:HEAD_DIM, :].astype(_BF16)


def _compress(kvc, pos_end, pe, w1, b1, w2, b2, rope_consts, batch):
    rows = kvc.shape[2]
    flat = CMP_STRIDE * HEAD_DIM
    in_specs = [
        pl.BlockSpec((None, None, rows, flat), lambda b, kv, hh: (kv * KV_HEADS + hh, b, 0, 0)),
        pl.BlockSpec((None, rows, 1), lambda b, kv, hh: (b, 0, 0)),
        pl.BlockSpec((None, 2, flat), lambda b, kv, hh: (kv, 0, 0)),
        pl.BlockSpec((None, 2 * flat, CMP_HIDDEN), lambda b, kv, hh: (kv, 0, 0)),
        pl.BlockSpec((None, 1, CMP_HIDDEN), lambda b, kv, hh: (kv, 0, 0)),
        pl.BlockSpec((None, CMP_HIDDEN, LANES), lambda b, kv, hh: (kv, 0, 0)),
        pl.BlockSpec((None, 1, LANES), lambda b, kv, hh: (kv, 0, 0)),
    ] + [pl.BlockSpec(c.shape, lambda b, kv, hh: (0, 0)) for c in rope_consts]
    return pl.pallas_call(
        _compress_kernel, grid=(batch, 2, KV_HEADS), in_specs=in_specs,
        out_specs=(pl.BlockSpec((None, None, None, rows, HEAD_DIM), lambda b, kv, hh: (kv, b, hh, 0, 0)),
                   pl.BlockSpec((None, None, None, HEAD_DIM, rows), lambda b, kv, hh: (kv, b, hh, 0, 0))),
        out_shape=(jax.ShapeDtypeStruct((2, batch, KV_HEADS, rows, HEAD_DIM), _BF16),
                   jax.ShapeDtypeStruct((2, batch, KV_HEADS, HEAD_DIM, rows), _BF16)),
        scratch_shapes=[pltpu.VMEM((rows, LANES), _F32)],
        compiler_params=pltpu.CompilerParams(dimension_semantics=("arbitrary",) * 3, vmem_limit_bytes=VMEM_LIMIT_BYTES),
        name="nsa_compress",
    )(kvc, pos_end, pe, w1, b1, w2, b2, *rope_consts)


def _per_head(fn, s, *shared):
    return jnp.concatenate([fn(s[:, g * Q_BLOCK:(g + 1) * Q_BLOCK], *shared) for g in range(GROUP)], axis=1)


def _nsa_step(qt_ref, gnt_ref, kc_ref, vct_ref, ks_ref, vst_ref, kw_ref, vwt_ref, cend_ref, wbias_ref,
              o_ref, sa_ref, sb_ref, qx_ref, *, n_sel, n_blocks):
    rows = kc_ref.shape[1]
    n_slc = rows // 4
    seq = ks_ref.shape[1]
    heads = range(HEADS_PER_STEP)
    qb = pl.program_id(2)
    q0 = qb * Q_BLOCK
    t = q0 + lax.broadcasted_iota(jnp.int32, (1, Q_BLOCK), 1)
    w0 = pl.multiple_of(jnp.maximum(q0 - WINDOW, 0), KV_CHUNK)
    wc = lax.shift_right_logical(w0, KV_CHUNK.bit_length() - 1)
    n_seq_chunks = seq // SEL_CHUNK
    sub = SEL_CHUNK // KV_CHUNK
    piece = WIN_SPAN // WIN_PIECES

    valid_c = cend_ref[...] <= t
    has_c = t >= CMP_BLOCK - 1
    blk = lax.broadcasted_iota(jnp.int32, (n_slc, 1), 0)
    cur = lax.shift_right_logical(t, SLC_BLOCK.bit_length() - 1)
    forced = (blk == 0) | (blk == cur) | (blk == cur - 1)
    bi = lax.broadcasted_iota(jnp.int32, (n_slc, Q_BLOCK), 0)
    bf = bi.astype(_F32)
    past = bi < lax.shift_right_logical(q0, SLC_BLOCK.bit_length() - 1)
    bias_w = wbias_ref[jnp.minimum(qb, WINDOW // Q_BLOCK)]
    causal = jnp.where(lax.broadcasted_iota(jnp.int32, (Q_BLOCK, Q_BLOCK), 0)
                       <= lax.broadcasted_iota(jnp.int32, (Q_BLOCK, Q_BLOCK), 1), 0.0, MASK_VALUE)
    out = [None] * HEADS_PER_STEP

    def before_loop(h):
        qt = qt_ref[h * GROUP * HEAD_DIM:(h + 1) * GROUP * HEAD_DIM, :]
        q_t = jnp.concatenate([qt[g * HEAD_DIM:(g + 1) * HEAD_DIM, :] for g in range(GROUP)], axis=1)
        raw_c = _dot(kc_ref[h], q_t)
        raw_w = _dot(kw_ref[h, pl.ds(w0, WIN_SPAN), :], q_t)
        raw_d = _dot(ks_ref[h, pl.ds(q0, Q_BLOCK), :HEAD_DIM], q_t)
        raw_0 = _dot(ks_ref[h, 0:SEL_CHUNK, :HEAD_DIM], q_t)
        yield

        s_w = _per_head(lambda s, b: s + b, raw_w, bias_w)
        m_w = jnp.max(s_w, axis=0, keepdims=True)
        s_d = _per_head(lambda s, b: s + b, raw_d, causal)
        m_d = jnp.max(s_d, axis=0, keepdims=True)

        def select(nb):
            take = lambda x: x if nb == n_slc else jnp.concatenate(
                [x[r * n_slc:r * n_slc + nb] for r in range(4)], axis=0)

            s_c = _per_head(lambda s, v: jnp.where(v, s, MASK_VALUE), take(raw_c), take(valid_c))
            e_c = jnp.exp2(s_c - jnp.max(s_c, axis=0, keepdims=True))
            inv_c = _per_head(lambda l, ok: jnp.where(ok, 1.0 / l, 0.0), jnp.sum(e_c, axis=0, keepdims=True), has_c)
            p_c = e_c * inv_c
            p_mm = p_c.astype(_BF16)
            if nb < n_slc:
                gap = jnp.zeros((n_slc - nb, GROUP * Q_BLOCK), _BF16)
                p_mm = jnp.concatenate([x for r in range(4) for x in (p_mm[r * nb:(r + 1) * nb], gap)], axis=0)
            o_c = _dot(vct_ref[h], p_mm)

            p_sum = p_c[:, :Q_BLOCK]
            for g in range(1, GROUP):
                p_sum = p_sum + p_c[:, g * Q_BLOCK:(g + 1) * Q_BLOCK]
            parts = [p_sum[r * nb:(r + 1) * nb, :] for r in range(4)]
            prev3 = jnp.where(blk[:nb] == 0, 0.0, pltpu.roll(parts[3], 1, axis=0))
            p_slc = parts[0] + parts[1] + parts[2] + parts[3] + prev3

            st = jnp.where(forced[:nb], -jnp.inf, jnp.where(blk[:nb] <= cur, p_slc, -1.0))
            p_w_parts = []
            for r in range(n_sel - 3):
                mx = jnp.max(st, axis=0, keepdims=True)
                first = jnp.min(jnp.where(st == mx, bf[:nb], float(n_slc)), axis=0, keepdims=True)
                if r < WIN_PIECES:
                    part = jnp.exp2(s_w[r * piece:(r + 1) * piece, :] - m_w)
                    p_w_parts.append(part.astype(_BF16))
                    col_max = jnp.max(part, axis=0, keepdims=True)
                    tie = col_max[:, :Q_BLOCK]
                    for g in range(1, GROUP):
                        tie = jnp.maximum(tie, col_max[:, g * Q_BLOCK:(g + 1) * Q_BLOCK])
                    first = first + tie * 0.0
                st = jnp.where(bf[:nb] == first, -jnp.inf, st)
            sel = st == -jnp.inf
            pen = jnp.where(sel & past[:nb], 0.0, MASK_VALUE)
            if nb < n_slc:
                pen = jnp.concatenate([pen, jnp.full((n_slc - nb, Q_BLOCK), MASK_VALUE, _F32)], axis=0)
            return (o_c, pen) + tuple(p_w_parts)

        o_c, pen, *p_w_parts = select(n_blocks)
        yield
        p_d = jnp.exp2(s_d - m_d).astype(_BF16)
        if n_slc < LANES:
            pen = jnp.concatenate([pen, jnp.zeros((LANES - n_slc, Q_BLOCK), _F32)], axis=0)
        pen_heads = jnp.concatenate([pen] * GROUP, axis=1)
        for b in range(SEL_CHUNK // SLC_BLOCK):
            ks_rows = slice(b * SLC_BLOCK, (b + 1) * SLC_BLOCK)
            sa_ref[h, ks_rows, :] = raw_0[ks_rows, :] + pen_heads[b:b + 1, :]
        qx_ref[h, 0:HEAD_DIM, :] = q_t
        qx_ref[h, HEAD_DIM:LANES, :] = jnp.zeros((LANES - HEAD_DIM, GROUP * Q_BLOCK), _BF16)
        qx_ref[h, LANES:2 * LANES, :] = jnp.concatenate([pen.astype(_BF16)] * GROUP, axis=1)
        yield

        v_w = jnp.concatenate([vwt_ref[h, wc + i] for i in range(WIN_SPAN // KV_CHUNK)], axis=1)
        acc_w = _dot(v_w, jnp.concatenate(p_w_parts, axis=0))
        acc_d = _dot(vst_ref[h, qb], p_d)
        yield
        o_w = acc_w[:HEAD_DIM] * (1.0 / acc_w[HEAD_DIM:HEAD_DIM + 1])
        out[h] = (o_c, o_w, m_d, acc_d)
        yield

    for _ in zip(*[before_loop(h) for h in heads]):
        pass

    def sel_scores(h, c, buf):
        k0 = pl.multiple_of(jnp.minimum(c, n_seq_chunks - 1) * SEL_CHUNK, SEL_CHUNK)
        buf[h] = _dot(ks_ref[h, pl.ds(k0, SEL_CHUNK), :], qx_ref[h])

    def sel_consume(h, c, buf, m, acc):
        c0 = jnp.minimum(c, n_seq_chunks - 1) * sub
        s = buf[h]
        m_new = jnp.maximum(m, jnp.max(s, axis=0, keepdims=True))
        p = jnp.exp2(s - m_new)
        v_t = jnp.concatenate([vst_ref[h, c0 + i] for i in range(sub)], axis=1)
        acc = jnp.exp2(m - m_new) * acc + _dot(v_t, p.astype(_BF16))
        return m_new, acc

    bufs = (sa_ref, sb_ref)

    def sel_trip(per_trip, base):
        def body(i, carry):
            carry = list(carry)
            for j in range(per_trip):
                c = base + per_trip * i + j
                for h in heads:
                    sel_scores(h, c + 1, bufs[(j + 1) % 2])
                for h in heads:
                    carry[h] = sel_consume(h, c, bufs[j % 2], *carry[h])
            return tuple(carry)
        return body

    n_chunks = (q0 + SEL_CHUNK - 1) // SEL_CHUNK
    carry, done = tuple((out[h][2], out[h][3]) for h in heads), 0
    for per_trip in TRIP_CHUNKS:
        left = n_chunks - done
        trips = (left + 1) // 2 if per_trip == TRIP_CHUNKS[-1] else left // per_trip
        carry = lax.fori_loop(0, trips, sel_trip(per_trip, done), carry)
        done = done + trips * per_trip

    for h in heads:
        o_c, o_w = out[h][0], out[h][1]
        acc_s = carry[h][1]
        o_s = acc_s[:HEAD_DIM] * (1.0 / acc_s[HEAD_DIM:HEAD_DIM + 1])
        first_gate = 3 * GROUP * (pl.program_id(1) * HEADS_PER_STEP + h)
        gate = lambda g, k: gnt_ref[pl.ds(first_gate + 3 * g + k, 1), :]
        outs = []
        for g in range(GROUP):
            ls = slice(g * Q_BLOCK, (g + 1) * Q_BLOCK)
            outs.append(gate(g, 0) * o_c[:, ls] + gate(g, 1) * o_s[:, ls] + gate(g, 2) * o_w[:, ls])
        o_ref[:, h * GROUP * HEAD_DIM:(h + 1) * GROUP * HEAD_DIM] = jnp.concatenate(outs, axis=0).T


def _nsa_kernel(*refs, n_sel, ranges):
    qb = pl.program_id(2)
    for first, stop, n_blocks in ranges:
        @pl.when((qb >= first) & (qb < stop))
        def _():
            _nsa_step(*refs, n_sel=n_sel, n_blocks=n_blocks)


def _nsa(qt, gnt, kc, vct, ks, vst, kw, vwt, batch, seq):
    n = qt.shape[1]
    n_qb = seq // Q_BLOCK
    rows = kc.shape[3]
    n_slc = seq // SLC_BLOCK
    hps = HEADS_PER_STEP
    seq_rows = lambda w: pl.BlockSpec((hps, seq, w), lambda b, hp, i: (hp, b, 0))
    vt_spec = pl.BlockSpec((hps, seq // KV_CHUNK, V_ROWS, KV_CHUNK), lambda b, hp, i: (hp, b, 0, 0))
    slot = np.arange(rows)
    cmp_idx = 4 * (slot % n_slc) + slot // n_slc
    cmp_end = np.where(cmp_idx < rows - 1, CMP_STRIDE * cmp_idx + CMP_BLOCK - 1, np.iinfo(np.int32).max)
    cmp_end = np.broadcast_to(cmp_end.astype(np.int32)[:, None], (rows, Q_BLOCK))
    off = Q_BLOCK * np.arange(WINDOW // Q_BLOCK + 1)[:, None, None]
    diff = off + np.arange(Q_BLOCK)[None, None, :] - np.arange(WIN_SPAN)[None, :, None]
    win_bias = np.where((diff >= 0) & (diff < WINDOW), 0.0, MASK_VALUE).astype(np.float32)
    counts = sorted({min(nb, n_slc) for nb in SELECT_BLOCKS})
    starts = [0] + [nb // 2 for nb in counts[:-1]]
    ranges = tuple(zip(starts, starts[1:] + [n_qb], counts))
    qcol = lambda r: pl.BlockSpec((hps * r, Q_BLOCK), lambda b, hp, i: (hp, b * n_qb + i))
    in_specs = [
        qcol(GROUP * HEAD_DIM), pl.BlockSpec((LANES, Q_BLOCK), lambda b, hp, i: (0, b * n_qb + i)),
        pl.BlockSpec((None, None, hps, rows, HEAD_DIM), lambda b, hp, i: (0, b, hp, 0, 0)),
        pl.BlockSpec((None, None, hps, HEAD_DIM, rows), lambda b, hp, i: (1, b, hp, 0, 0)),
        seq_rows(2 * LANES), vt_spec, seq_rows(HEAD_DIM), vt_spec,
        _const_spec(cmp_end.shape), _const_spec(win_bias.shape),
    ]
    return pl.pallas_call(
        functools.partial(_nsa_kernel, n_sel=min(N_SELECT, n_slc), ranges=ranges),
        grid=(batch, KV_HEADS // hps, n_qb), in_specs=in_specs,
        out_specs=pl.BlockSpec((Q_BLOCK, hps * GROUP * HEAD_DIM), lambda b, hp, i: (b * n_qb + i, hp)),
        out_shape=jax.ShapeDtypeStruct((n, D_MODEL), _F32),
        scratch_shapes=[pltpu.VMEM((hps, SEL_CHUNK, GROUP * Q_BLOCK), _F32)] * 2
        + [pltpu.VMEM((hps, 2 * LANES, GROUP * Q_BLOCK), _BF16)],
        compiler_params=pltpu.CompilerParams(dimension_semantics=("arbitrary",) * 3, vmem_limit_bytes=VMEM_LIMIT_BYTES),
        name="nsa_attention",
    )(qt, gnt, kc, vct, ks, vst, kw, vwt, cmp_end, win_bias)


def _memkv_kernel(mem_ref, w_ref, o_ref):
    o_ref[...] = _dot(mem_ref[...].astype(_BF16), w_ref[...]).astype(_BF16)


def _memkv(mem2, w_xkv):
    m = mem2.shape[0]
    return pl.pallas_call(
        _memkv_kernel, grid=(1,),
        in_specs=[pl.BlockSpec(mem2.shape, lambda i: (0, 0)), pl.BlockSpec(w_xkv.shape, lambda i: (0, 0))],
        out_specs=pl.BlockSpec((m, 2 * D_MODEL), lambda i: (0, 0)),
        out_shape=jax.ShapeDtypeStruct((m, 2 * D_MODEL), _BF16),
        compiler_params=pltpu.CompilerParams(vmem_limit_bytes=VMEM_LIMIT_BYTES),
        name="mem_kv",
    )(mem2, w_xkv)


def _trunk_kernel(a_ref, gb_ref, yb_ref, h_ref, kvm_ref, wo_ref, wxq_ref, wxo_ref, wf1_ref, wf2_ref,
                  g1_ref, b1_ref, g2_ref, b2_ref, g3_ref, b3_ref, o_ref):
    tm = a_ref.shape[0]
    halves = [slice(i * (tm // TRUNK_SPLIT), (i + 1) * (tm // TRUNK_SPLIT)) for i in range(TRUNK_SPLIT)]
    both = lambda fn, *xs: [fn(*(x[i] for x in xs)) for i in range(TRUNK_SPLIT)]

    mix = [(a_ref[r, :] + gb_ref[r, :] * yb_ref[r, :]).astype(_BF16) for r in halves]
    y1 = both(lambda m: _dot(m, wo_ref[...]), mix)
    h1 = [_layer_norm(ALPHA * h_ref[r, :] + y, g1_ref[...], b1_ref[...]) for r, y in zip(halves, y1)]

    qx = both(lambda x: _dot(x.astype(_BF16), wxq_ref[...]).astype(_BF16), h1)
    heads = [[] for _ in range(TRUNK_SPLIT)]
    for hh in range(XATTN_HEADS):
        cs = slice(hh * XATTN_HEAD_DIM, (hh + 1) * XATTN_HEAD_DIM)
        vcs = slice(D_MODEL + hh * XATTN_HEAD_DIM, D_MODEL + (hh + 1) * XATTN_HEAD_DIM)
        s = both(lambda q: _dot_nt(q[:, cs], kvm_ref[:, cs]) * (XATTN_HEAD_DIM ** -0.5), qx)
        e = both(lambda x: jnp.exp(x - jnp.max(x, axis=-1, keepdims=True)), s)
        p = both(lambda x: (x * (1.0 / jnp.sum(x, axis=-1, keepdims=True))).astype(_BF16), e)
        for i, o in enumerate(both(lambda x: _dot(x, kvm_ref[:, vcs]), p)):
            heads[i].append(o)
    xo = both(lambda hs: jnp.concatenate(hs, axis=1).astype(_BF16), heads)
    y2 = both(lambda x: _dot(x, wxo_ref[...]), xo)
    h2 = both(lambda x, y: _layer_norm(ALPHA * x + y, g2_ref[...], b2_ref[...]), h1, y2)

    h2b = both(lambda x: x.astype(_BF16), h2)
    ff = [None] * TRUNK_SPLIT
    for c in range(D_FF // D_MODEL):
        cs = slice(c * D_MODEL, (c + 1) * D_MODEL)
        act = both(lambda x: jnp.square(jnp.maximum(_dot(x, wf1_ref[:, cs]), 0.0)).astype(_BF16), h2b)
        part = both(lambda x: _dot(x, wf2_ref[cs, :]), act)
        ff = part if c == 0 else both(lambda x, y: x + y, ff, part)
    for r, x, y in zip(halves, h2, ff):
        o_ref[r, :] = _layer_norm(ALPHA * x + y, g3_ref[...], b3_ref[...])


def _trunk(a, gb, yb, h, kvm, weights, lns, seq, mem_len):
    n = a.shape[0]
    tm = TRUNK_TILE
    tiles_per_batch = seq // tm
    row = pl.BlockSpec((tm, D_MODEL), lambda i: (i, 0))
    in_specs = ([row, row, row, row, pl.BlockSpec((mem_len, 2 * D_MODEL), lambda i: (i // tiles_per_batch, 0))]
                + [_const_spec(w.shape) for w in weights] + [_const_spec(p.shape) for p in lns])
    return pl.pallas_call(
        _trunk_kernel, grid=(n // tm,), in_specs=in_specs, out_specs=row,
        out_shape=jax.ShapeDtypeStruct((n, D_MODEL), _F32),
        compiler_params=pltpu.CompilerParams(dimension_semantics=("arbitrary",), vmem_limit_bytes=VMEM_LIMIT_BYTES),
        name="trunk",
    )(a, gb, yb, h, kvm, *weights, *lns)


def _rope_constants():
    half = ROT_DIM // 2
    inv = ROPE_THETA ** (-jnp.arange(half, dtype=_F32) / half)
    d = jnp.arange(LANES) % HEAD_DIM
    inv_lane = jnp.where(d < ROT_DIM, inv[d % half], 0.0).astype(_F32)[None, :]
    neg_first = jnp.where(d < half, -1.0, 0.0).astype(_F32)[None, :]
    pos_second = jnp.where((d >= half) & (d < ROT_DIM), 1.0, 0.0).astype(_F32)[None, :]
    return inv_lane, neg_first, pos_second


def kernel(x, mem, positions, ln_in_g, ln_in_b, w_in, gmlp_ln_g, gmlp_ln_b, gmlp_ws, gmlp_bs, cmp_k_pe, cmp_k_w1, cmp_k_b1, cmp_k_w2, cmp_k_b2, cmp_v_pe, cmp_v_w1, cmp_v_b1, cmp_v_w2, cmp_v_b2, w_out, ln1_g, ln1_b, w_xq, w_xkv, w_xo, ln2_g, ln2_b, w_ff1, w_ff2, ln3_g, ln3_b):
    batch, seq, _ = x.shape
    mem_len = mem.shape[1]
    n = batch * seq
    n_slc = seq // SLC_BLOCK
    assert w_in.shape[0] == 1, "one layer"
    assert seq % SEL_CHUNK == 0 and TRIP_CHUNKS[-1] == 2 and seq >= WIN_SPAN and seq & (seq - 1) == 0 and n_slc <= LANES

    rope_consts = _rope_constants()
    vec = lambda p: p.reshape(1, -1)

    wi = w_in[0]
    o_u, o_v, o_q, o_kv, o_gn, o_ga, o_gb = (0, 1024, 2048, 3072, 3072 + 6 * KV_DIM, 3120 + 6 * KV_DIM, 4144 + 6 * KV_DIM)
    w_gn = jnp.pad(wi[:, o_gn:o_ga], ((0, 0), (0, LANES - 3 * NSA_HEADS)))
    w_parts = [wi[:, o_u:o_v], wi[:, o_v:o_q], wi[:, o_ga:o_gb], wi[:, o_q:o_kv], wi[:, o_gb:], wi[:, o_kv:o_gn], w_gn]
    w_parts = [w.astype(_BF16) for w in w_parts]

    h, a, gb, qt, kvc, ks, vst, kw, vwt, gnt = _project(
        x.reshape(n, D_MODEL), positions.reshape(n, 1), vec(ln_in_g), vec(ln_in_b), w_parts,
        vec(gmlp_ln_g[0]), vec(gmlp_ln_b[0]), gmlp_ws[0], gmlp_bs[0].T, rope_consts, seq)

    rows = seq // CMP_STRIDE
    flat = CMP_STRIDE * HEAD_DIM
    pos_end = jnp.pad(positions[:, CMP_BLOCK - 1::CMP_STRIDE], ((0, 0), (0, 1)))[:, :, None]
    pad_lanes = lambda w: jnp.pad(w, ((0, 0), (0, LANES - HEAD_DIM)))
    kvcmp, kvcmp_t = _compress(
        kvc.reshape(2 * KV_HEADS, batch, rows, flat), pos_end,
        jnp.stack([cmp_k_pe[0].reshape(2, flat), cmp_v_pe[0].reshape(2, flat)]),
        jnp.stack([cmp_k_w1[0], cmp_v_w1[0]]).astype(_BF16),
        jnp.stack([vec(cmp_k_b1[0]), vec(cmp_v_b1[0])]),
        jnp.stack([pad_lanes(cmp_k_w2[0]), pad_lanes(cmp_v_w2[0])]).astype(_BF16),
        jnp.stack([pad_lanes(vec(cmp_k_b2[0])), pad_lanes(vec(cmp_v_b2[0]))]),
        rope_consts, batch)

    yb = _nsa(qt, gnt, kvcmp, kvcmp_t, ks, vst, kw, vwt, batch, seq)

    kvm = _memkv(mem.reshape(batch * mem_len, D_MODEL), w_xkv[0].astype(_BF16))
    weights = [w.astype(_BF16) for w in (w_out[0], w_xq[0], w_xo[0], w_ff1[0], w_ff2[0])]
    lns = [vec(p[0]) for p in (ln1_g, ln1_b, ln2_g, ln2_b, ln3_g, ln3_b)]
    out = _trunk(a, gb, yb, h, kvm, weights, lns, seq, mem_len)
    return out.reshape(batch, seq, D_MODEL)
```

```python
import functools
import math

import jax
import jax.numpy as jnp
import numpy as np
from jax import lax
from jax.experimental import pallas as pl
from jax.experimental.pallas import tpu as pltpu

D_MODEL = 1024
LN_EPS = 1e-5
ALPHA = 2.0 ** 0.25
ROPE_THETA = 500000.0
GMLP_GROUPS = 8
GMLP_CHUNK = 128
NSA_HEADS = 16
HEAD_DIM = 64
KV_HEADS = 4
GROUP = NSA_HEADS // KV_HEADS
KV_DIM = KV_HEADS * HEAD_DIM
ROT_DIM = HEAD_DIM // 4
CMP_BLOCK = 32
CMP_STRIDE = 16
CMP_HIDDEN = 4 * HEAD_DIM
SLC_BLOCK = 64
N_SELECT = 16
WINDOW = 512
Q_BLOCK = 128
KV_CHUNK = 128
V_ROWS = 80
XATTN_HEADS = 4
XATTN_HEAD_DIM = D_MODEL // XATTN_HEADS
D_FF = 4 * D_MODEL

LANES = 128
PROJ_TILE = 512
PROJ_SPLIT = 2
TRUNK_TILE = 512
TRUNK_SPLIT = 2
SEL_CHUNK = 512
TRIP_CHUNKS = (8, 4, 2)
HEADS_PER_STEP = 1
WIN_SPAN = WINDOW + Q_BLOCK
SELECT_BLOCKS = (32, 64, 96, 128)
WIN_PIECES = 5
MASK_VALUE = -1e30
LOG2_E = math.log2(math.e)
VMEM_LIMIT_BYTES = 56 * 1024 * 1024

_F32 = jnp.float32
_BF16 = jnp.bfloat16


def _layer_norm(x, g, b):
    mu = jnp.mean(x, axis=-1, keepdims=True)
    xc = x - mu
    var = jnp.mean(xc * xc, axis=-1, keepdims=True)
    return xc * lax.rsqrt(var + LN_EPS) * g + b


def _dot(a, b):
    return jnp.dot(a, b, preferred_element_type=_F32)


def _dot_nt(a, b):
    return lax.dot_general(a, b, (((1,), (1,)), ((), ())), preferred_element_type=_F32)


def _rope_tables(pos_f32, inv_lane, neg_first, pos_second):
    ang = pos_f32 * inv_lane
    c = jnp.cos(ang)
    s = jnp.sin(ang)
    return c, s * neg_first, s * pos_second


def _rope_block(x, tables):
    c, s_first, s_second = tables
    half = ROT_DIM // 2
    up = pltpu.roll(x, LANES - half, axis=1)
    down = pltpu.roll(x, half, axis=1)
    return x * c + up * s_first + down * s_second


def _const_spec(shape):
    nd = len(shape)
    return pl.BlockSpec(shape, lambda *_: (0,) * nd, pipeline_mode=pl.Buffered(1))


def _proj_kernel(x_ref, pos_ref, lng_ref, lnb_ref, wu_ref, wv_ref, wga_ref, wq_ref, wgb_ref, wkv_ref, wgn_ref,
                 glng_ref, glnb_ref, ws_ref, bst_ref, inv_ref, m1_ref, m2_ref,
                 h_ref, a_ref, gb_ref, qt_ref, kvc_ref, ks_ref, vst_ref, kw_ref, vwt_ref, gnt_ref, cmp_ref, *, seq):
    tm = x_ref.shape[0]
    part = tm // PROJ_SPLIT
    ti = lax.broadcasted_iota(jnp.int32, (GMLP_CHUNK, GMLP_CHUNK), 0)
    si = lax.broadcasted_iota(jnp.int32, (GMLP_CHUNK, GMLP_CHUNK), 1)
    w_tril = [jnp.where(si <= ti, ws_ref[g], 0.0).astype(_BF16) for g in range(GMLP_GROUPS)]
    bst = bst_ref[...]
    q_scale = HEAD_DIM ** -0.5 * LOG2_E
    heads_per_slab = LANES // HEAD_DIM
    blk_lane = lax.broadcasted_iota(jnp.int32, (part, LANES), 1)
    ones_col = (lax.broadcasted_iota(jnp.int32, (part, HEAD_DIM), 1) == 0).astype(_F32)

    def stages(r0):
        rs = slice(r0, r0 + part)
        h = _layer_norm(x_ref[rs, :], lng_ref[...], lnb_ref[...])
        h_ref[rs, :] = h
        hb = h.astype(_BF16)
        zu = _dot(hb, wu_ref[...])
        yield
        u_act = jax.nn.gelu(zu)
        zv = _dot(hb, wv_ref[...])
        yield
        vn = _layer_norm(jax.nn.gelu(zv), glng_ref[...], glnb_ref[...]).astype(_BF16)
        zga = _dot(hb, wga_ref[...])
        yield
        ua = u_act * jax.nn.sigmoid(zga)
        for g in range(GMLP_GROUPS):
            cs = slice(g * LANES, (g + 1) * LANES)
            for c in range(part // GMLP_CHUNK):
                cr = slice(c * GMLP_CHUNK, (c + 1) * GMLP_CHUNK)
                mixed = _dot(w_tril[g], vn[cr, cs]) + bst[:, g:g + 1]
                a_ref[r0 + c * GMLP_CHUNK:r0 + (c + 1) * GMLP_CHUNK, cs] = ua[cr, cs] * mixed
        zgb = _dot(hb, wgb_ref[...])
        yield
        gb_ref[rs, :] = jax.nn.sigmoid(zgb)
        zgn = _dot(hb, wgn_ref[...])
        yield
        gnt_ref[:, rs] = jax.nn.sigmoid(zgn).T
        zq = _dot(hb, wq_ref[...])
        yield
        tables = _rope_tables(pos_ref[rs, :].astype(_F32), inv_ref[...], m1_ref[...], m2_ref[...])
        for i in range(D_MODEL // LANES):
            cs = slice(i * LANES, (i + 1) * LANES)
            qt_ref[cs, rs] = (_rope_block(zq[:, cs], tables) * q_scale).T.astype(_BF16)
        zkv = _dot(hb, wkv_ref[...])
        yield
        for sl in range(2 * KV_DIM // LANES):
            cmp_ref[sl, rs, :] = zkv[:, sl * LANES:(sl + 1) * LANES]
        out_rows = slice(r0 // CMP_STRIDE, (r0 + part) // CMP_STRIDE)
        for sl in range(2 * KV_DIM // LANES):
            for l in range(CMP_STRIDE):
                pair = cmp_ref[sl, pl.ds(r0 + l, part // CMP_STRIDE, stride=CMP_STRIDE), :]
                for hh in range(heads_per_slab):
                    kvc_ref[sl * heads_per_slab + hh, out_rows, l * HEAD_DIM:(l + 1) * HEAD_DIM] = (
                        pair[:, hh * HEAD_DIM:(hh + 1) * HEAD_DIM])
        seq_pos = (pl.program_id(0) * tm + r0 + lax.broadcasted_iota(jnp.int32, (part, LANES), 0)) & (seq - 1)
        blk_onehot = jnp.where(lax.shift_right_logical(seq_pos, SLC_BLOCK.bit_length() - 1) == blk_lane, 1.0, 0.0)
        k_tail = jnp.concatenate([jnp.zeros((part, HEAD_DIM), _BF16), blk_onehot.astype(_BF16)], axis=1)
        for base, k_out, vt_out in ((2 * KV_DIM, ks_ref, vst_ref), (4 * KV_DIM, kw_ref, vwt_ref)):
            for i in range(KV_DIM // LANES):
                kr = _rope_block(zkv[:, base + i * LANES: base + (i + 1) * LANES], tables).astype(_BF16)
                for hh in range(heads_per_slab):
                    k_head = kr[:, hh * HEAD_DIM:(hh + 1) * HEAD_DIM]
                    if k_out is ks_ref:
                        k_out[2 * i + hh, rs, :] = jnp.concatenate([k_head, k_tail], axis=1)
                    else:
                        k_out[2 * i + hh, rs, :] = k_head
            for hh in range(KV_HEADS):
                v = zkv[:, base + KV_DIM + hh * HEAD_DIM: base + KV_DIM + (hh + 1) * HEAD_DIM]
                v_ext = jnp.concatenate([v, ones_col], axis=1)
                for c in range(part // KV_CHUNK):
                    vt_out[hh, r0 // KV_CHUNK + c] = v_ext[c * KV_CHUNK:(c + 1) * KV_CHUNK, :].T[:V_ROWS].astype(_BF16)
        yield

    for _ in zip(*[stages(i * part) for i in range(PROJ_SPLIT)]):
        pass


def _project(x2, pos2, ln_g, ln_b, w_parts, gln_g, gln_b, ws, bst, rope_consts, seq):
    n = x2.shape[0]
    tm = PROJ_TILE
    row = lambda shape: pl.BlockSpec(shape, lambda i: (i, 0))
    col = lambda shape: pl.BlockSpec(shape, lambda i: (0, i))
    head_rows = lambda nh, r, w: pl.BlockSpec((nh, r, w), lambda i: (0, i, 0))
    vt_spec = pl.BlockSpec((KV_HEADS, tm // KV_CHUNK, V_ROWS, KV_CHUNK), lambda i: (0, i, 0, 0))
    w_specs = [_const_spec(w.shape) for w in w_parts]
    in_specs = ([row((tm, D_MODEL)), row((tm, 1)), _const_spec(ln_g.shape), _const_spec(ln_b.shape)] + w_specs
                + [_const_spec(gln_g.shape), _const_spec(gln_b.shape), _const_spec(ws.shape), _const_spec(bst.shape)]
                + [_const_spec(c.shape) for c in rope_consts])
    out_shape = (
        jax.ShapeDtypeStruct((n, D_MODEL), _F32),
        jax.ShapeDtypeStruct((n, D_MODEL), _F32),
        jax.ShapeDtypeStruct((n, D_MODEL), _F32),
        jax.ShapeDtypeStruct((D_MODEL, n), _BF16),
        jax.ShapeDtypeStruct((2 * KV_HEADS, n // CMP_STRIDE, CMP_STRIDE * HEAD_DIM), _F32),
        jax.ShapeDtypeStruct((KV_HEADS, n, 2 * LANES), _BF16),
        jax.ShapeDtypeStruct((KV_HEADS, n // KV_CHUNK, V_ROWS, KV_CHUNK), _BF16),
        jax.ShapeDtypeStruct((KV_HEADS, n, HEAD_DIM), _BF16),
        jax.ShapeDtypeStruct((KV_HEADS, n // KV_CHUNK, V_ROWS, KV_CHUNK), _BF16),
        jax.ShapeDtypeStruct((LANES, n), _F32),
    )
    out_specs = (row((tm, D_MODEL)), row((tm, D_MODEL)), row((tm, D_MODEL)), col((D_MODEL, tm)),
                 head_rows(2 * KV_HEADS, tm // CMP_STRIDE, CMP_STRIDE * HEAD_DIM),
                 head_rows(KV_HEADS, tm, 2 * LANES), vt_spec, head_rows(KV_HEADS, tm, HEAD_DIM), vt_spec,
                 col((LANES, tm)))
    return pl.pallas_call(
        functools.partial(_proj_kernel, seq=seq), grid=(n // tm,), in_specs=in_specs,
        out_specs=out_specs, out_shape=out_shape,
        scratch_shapes=[pltpu.VMEM((2 * KV_DIM // LANES, tm, LANES), _F32)],
        compiler_params=pltpu.CompilerParams(dimension_semantics=("arbitrary",), vmem_limit_bytes=VMEM_LIMIT_BYTES),
        name="nsa_proj",
    )(x2, pos2, ln_g, ln_b, *w_parts, gln_g, gln_b, ws, bst, *rope_consts)


def _compress_kernel(x_ref, pos_ref, pe_ref, w1_ref, b1_ref, w2_ref, b2_ref, inv_ref, m1_ref, m2_ref,
                     o_ref, ot_ref, perm_ref):
    rows = x_ref.shape[0]
    is_key = (pl.program_id(1) == 0).astype(_F32)
    x = x_ref[...]
    half = CMP_STRIDE * HEAD_DIM
    ya = _dot((x + pe_ref[0:1, :]).astype(_BF16), w1_ref[0:half, :])
    yb = _dot((x + pe_ref[1:2, :]).astype(_BF16), w1_ref[half:2 * half, :])
    hid = jax.nn.gelu(ya + pltpu.roll(yb, rows - 1, axis=0) + b1_ref[...])
    out = _dot(hid.astype(_BF16), w2_ref[...]) + b2_ref[...]
    tables = _rope_tables(pos_ref[...].astype(_F32) * is_key, inv_ref[...], m1_ref[...], m2_ref[...])
    perm_ref[...] = _rope_block(out, tables)
    groups = rows // 4
    for r in range(4):
        part = perm_ref[pl.ds(r, groups, stride=4), :]
        o_ref[r * groups:(r + 1) * groups, :] = part[:, :HEAD_DIM].astype(_BF16)
        ot_ref[:, r * groups:(r + 1) * groups] = part.T[:HEAD_DIM, :].astype(_BF16)


def _compress(kvc, pos_end, pe, w1, b1, w2, b2, rope_consts, batch):
    rows = kvc.shape[2]
    flat = CMP_STRIDE * HEAD_DIM
    in_specs = [
        pl.BlockSpec((None, None, rows, flat), lambda b, kv, hh: (kv * KV_HEADS + hh, b, 0, 0)),
        pl.BlockSpec((None, rows, 1), lambda b, kv, hh: (b, 0, 0)),
        pl.BlockSpec((None, 2, flat), lambda b, kv, hh: (kv, 0, 0)),
        pl.BlockSpec((None, 2 * flat, CMP_HIDDEN), lambda b, kv, hh: (kv, 0, 0)),
        pl.BlockSpec((None, 1, CMP_HIDDEN), lambda b, kv, hh: (kv, 0, 0)),
        pl.BlockSpec((None, CMP_HIDDEN, LANES), lambda b, kv, hh: (kv, 0, 0)),
        pl.BlockSpec((None, 1, LANES), lambda b, kv, hh: (kv, 0, 0)),
    ] + [pl.BlockSpec(c.shape, lambda b, kv, hh: (0, 0)) for c in rope_consts]
    return pl.pallas_call(
        _compress_kernel, grid=(batch, 2, KV_HEADS), in_specs=in_specs,
        out_specs=(pl.BlockSpec((None, None, None, rows, HEAD_DIM), lambda b, kv, hh: (kv, b, hh, 0, 0)),
                   pl.BlockSpec((None, None, None, HEAD_DIM, rows), lambda b, kv, hh: (kv, b, hh, 0, 0))),
        out_shape=(jax.ShapeDtypeStruct((2, batch, KV_HEADS, rows, HEAD_DIM), _BF16),
                   jax.ShapeDtypeStruct((2, batch, KV_HEADS, HEAD_DIM, rows), _BF16)),
        scratch_shapes=[pltpu.VMEM((rows, LANES), _F32)],
        compiler_params=pltpu.CompilerParams(dimension_semantics=("arbitrary",) * 3, vmem_limit_bytes=VMEM_LIMIT_BYTES),
        name="nsa_compress",
    )(kvc, pos_end, pe, w1, b1, w2, b2, *rope_consts)


def _per_head(fn, s, *shared):
    return jnp.concatenate([fn(s[:, g * Q_BLOCK:(g + 1) * Q_BLOCK], *shared) for g in range(GROUP)], axis=1)


def _nsa_step(qt_ref, gnt_ref, kc_ref, vct_ref, ks_ref, vst_ref, kw_ref, vwt_ref, cend_ref, wbias_ref,
              o_ref, sa_ref, sb_ref, qx_ref, *, n_sel, n_blocks):
    rows = kc_ref.shape[1]
    n_slc = rows // 4
    seq = ks_ref.shape[1]
    heads = range(HEADS_PER_STEP)
    qb = pl.program_id(2)
    q0 = qb * Q_BLOCK
    t = q0 + lax.broadcasted_iota(jnp.int32, (1, Q_BLOCK), 1)
    w0 = pl.multiple_of(jnp.maximum(q0 - WINDOW, 0), KV_CHUNK)
    wc = lax.shift_right_logical(w0, KV_CHUNK.bit_length() - 1)
    n_seq_chunks = seq // SEL_CHUNK
    sub = SEL_CHUNK // KV_CHUNK
    piece = WIN_SPAN // WIN_PIECES

    valid_c = cend_ref[...] <= t
    has_c = t >= CMP_BLOCK - 1
    blk = lax.broadcasted_iota(jnp.int32, (n_slc, 1), 0)
    cur = lax.shift_right_logical(t, SLC_BLOCK.bit_length() - 1)
    forced = (blk == 0) | (blk == cur) | (blk == cur - 1)
    bi = lax.broadcasted_iota(jnp.int32, (n_slc, Q_BLOCK), 0)
    bf = bi.astype(_F32)
    past = bi < lax.shift_right_logical(q0, SLC_BLOCK.bit_length() - 1)
    bias_w = wbias_ref[jnp.minimum(qb, WINDOW // Q_BLOCK)]
    causal = jnp.where(lax.broadcasted_iota(jnp.int32, (Q_BLOCK, Q_BLOCK), 0)
                       <= lax.broadcasted_iota(jnp.int32, (Q_BLOCK, Q_BLOCK), 1), 0.0, MASK_VALUE)
    out = [None] * HEADS_PER_STEP

    def before_loop(h):
        qt = qt_ref[h * GROUP * HEAD_DIM:(h + 1) * GROUP * HEAD_DIM, :]
        q_t = jnp.concatenate([qt[g * HEAD_DIM:(g + 1) * HEAD_DIM, :] for g in range(GROUP)], axis=1)
        raw_c = _dot(kc_ref[h], q_t)
        raw_w = _dot(kw_ref[h, pl.ds(w0, WIN_SPAN), :], q_t)
        raw_d = _dot(ks_ref[h, pl.ds(q0, Q_BLOCK), :HEAD_DIM], q_t)
        raw_0 = _dot(ks_ref[h, 0:SEL_CHUNK, :HEAD_DIM], q_t)
        yield

        s_w = _per_head(lambda s, b: s + b, raw_w, bias_w)
        m_w = jnp.max(s_w, axis=0, keepdims=True)
        s_d = _per_head(lambda s, b: s + b, raw_d, causal)
        m_d = jnp.max(s_d, axis=0, keepdims=True)

        def select(nb):
            take = lambda x: x if nb == n_slc else jnp.concatenate(
                [x[r * n_slc:r * n_slc + nb] for r in range(4)], axis=0)

            s_c = _per_head(lambda s, v: jnp.where(v, s, MASK_VALUE), take(raw_c), take(valid_c))
            e_c = jnp.exp2(s_c - jnp.max(s_c, axis=0, keepdims=True))
            inv_c = _per_head(lambda l, ok: jnp.where(ok, 1.0 / l, 0.0), jnp.sum(e_c, axis=0, keepdims=True), has_c)
            p_c = e_c * inv_c
            p_mm = p_c.astype(_BF16)
            if nb < n_slc:
                gap = jnp.zeros((n_slc - nb, GROUP * Q_BLOCK), _BF16)
                p_mm = jnp.concatenate([x for r in range(4) for x in (p_mm[r * nb:(r + 1) * nb], gap)], axis=0)
            o_c = _dot(vct_ref[h], p_mm)

            p_sum = p_c[:, :Q_BLOCK]
            for g in range(1, GROUP):
                p_sum = p_sum + p_c[:, g * Q_BLOCK:(g + 1) * Q_BLOCK]
            parts = [p_sum[r * nb:(r + 1) * nb, :] for r in range(4)]
            prev3 = jnp.where(blk[:nb] == 0, 0.0, pltpu.roll(parts[3], 1, axis=0))
            p_slc = parts[0] + parts[1] + parts[2] + parts[3] + prev3

            st = jnp.where(forced[:nb], -jnp.inf, jnp.where(blk[:nb] <= cur, p_slc, -1.0))
            p_w_parts = []
            for r in range(n_sel - 3):
                mx = jnp.max(st, axis=0, keepdims=True)
                first = jnp.min(jnp.where(st == mx, bf[:nb], float(n_slc)), axis=0, keepdims=True)
                if r < WIN_PIECES:
                    part = jnp.exp2(s_w[r * piece:(r + 1) * piece, :] - m_w)
                    p_w_parts.append(part.astype(_BF16))
                    col_max = jnp.max(part, axis=0, keepdims=True)
                    tie = col_max[:, :Q_BLOCK]
                    for g in range(1, GROUP):
                        tie = jnp.maximum(tie, col_max[:, g * Q_BLOCK:(g + 1) * Q_BLOCK])
                    first = first + tie * 0.0
                st = jnp.where(bf[:nb] == first, -jnp.inf, st)
            sel = st == -jnp.inf
            pen = jnp.where(sel & past[:nb], 0.0, MASK_VALUE)
            if nb < n_slc:
                pen = jnp.concatenate([pen, jnp.full((n_slc - nb, Q_BLOCK), MASK_VALUE, _F32)], axis=0)
            return (o_c, pen) + tuple(p_w_parts)

        o_c, pen, *p_w_parts = select(n_blocks)
        yield
        p_d = jnp.exp2(s_d - m_d).astype(_BF16)
        if n_slc < LANES:
            pen = jnp.concatenate([pen, jnp.zeros((LANES - n_slc, Q_BLOCK), _F32)], axis=0)
        pen_heads = jnp.concatenate([pen] * GROUP, axis=1)
        for b in range(SEL_CHUNK // SLC_BLOCK):
            ks_rows = slice(b * SLC_BLOCK, (b + 1) * SLC_BLOCK)
            sa_ref[h, ks_rows, :] = raw_0[ks_rows, :] + pen_heads[b:b + 1, :]
        qx_ref[h, 0:HEAD_DIM, :] = q_t
        qx_ref[h, HEAD_DIM:LANES, :] = jnp.zeros((LANES - HEAD_DIM, GROUP * Q_BLOCK), _BF16)
        qx_ref[h, LANES:2 * LANES, :] = jnp.concatenate([pen.astype(_BF16)] * GROUP, axis=1)
        yield

        v_w = jnp.concatenate([vwt_ref[h, wc + i] for i in range(WIN_SPAN // KV_CHUNK)], axis=1)
        acc_w = _dot(v_w, jnp.concatenate(p_w_parts, axis=0))
        acc_d = _dot(vst_ref[h, qb], p_d)
        yield
        o_w = acc_w[:HEAD_DIM] * (1.0 / acc_w[HEAD_DIM:HEAD_DIM + 1])
        out[h] = (o_c, o_w, m_d, acc_d)
        yield

    for _ in zip(*[before_loop(h) for h in heads]):
        pass

    def sel_scores(h, c, buf):
        k0 = pl.multiple_of(jnp.minimum(c, n_seq_chunks - 1) * SEL_CHUNK, SEL_CHUNK)
        buf[h] = _dot(ks_ref[h, pl.ds(k0, SEL_CHUNK), :], qx_ref[h])

    def sel_consume(h, c, buf, m, acc):
        c0 = jnp.minimum(c, n_seq_chunks - 1) * sub
        s = buf[h]
        m_new = jnp.maximum(m, jnp.max(s, axis=0, keepdims=True))
        p = jnp.exp2(s - m_new)
        v_t = jnp.concatenate([vst_ref[h, c0 + i] for i in range(sub)], axis=1)
        acc = jnp.exp2(m - m_new) * acc + _dot(v_t, p.astype(_BF16))
        return m_new, acc

    bufs = (sa_ref, sb_ref)

    def sel_trip(per_trip, base):
        def body(i, carry):
            carry = list(carry)
            for j in range(per_trip):
                c = base + per_trip * i + j
                for h in heads:
                    sel_scores(h, c + 1, bufs[(j + 1) % 2])
                for h in heads:
                    carry[h] = sel_consume(h, c, bufs[j % 2], *carry[h])
            return tuple(carry)
        return body

    n_chunks = (q0 + SEL_CHUNK - 1) // SEL_CHUNK
    carry, done = tuple((out[h][2], out[h][3]) for h in heads), 0
    for per_trip in TRIP_CHUNKS:
        left = n_chunks - done
        trips = (left + 1) // 2 if per_trip == TRIP_CHUNKS[-1] else left // per_trip
        carry = lax.fori_loop(0, trips, sel_trip(per_trip, done), carry)
        done = done + trips * per_trip

    for h in heads:
        o_c, o_w = out[h][0], out[h][1]
        acc_s = carry[h][1]
        o_s = acc_s[:HEAD_DIM] * (1.0 / acc_s[HEAD_DIM:HEAD_DIM + 1])
        first_gate = 3 * GROUP * (pl.program_id(1) * HEADS_PER_STEP + h)
        gate = lambda g, k: gnt_ref[pl.ds(first_gate + 3 * g + k, 1), :]
        outs = []
        for g in range(GROUP):
            ls = slice(g * Q_BLOCK, (g + 1) * Q_BLOCK)
            outs.append(gate(g, 0) * o_c[:, ls] + gate(g, 1) * o_s[:, ls] + gate(g, 2) * o_w[:, ls])
        o_ref[:, h * GROUP * HEAD_DIM:(h + 1) * GROUP * HEAD_DIM] = jnp.concatenate(outs, axis=0).T


def _nsa_kernel(*refs, n_sel, ranges):
    qb = pl.program_id(2)
    for first, stop, n_blocks in ranges:
        @pl.when((qb >= first) & (qb < stop))
        def _():
            _nsa_step(*refs, n_sel=n_sel, n_blocks=n_blocks)


def _nsa(qt, gnt, kc, vct, ks, vst, kw, vwt, batch, seq):
    n = qt.shape[1]
    n_qb = seq // Q_BLOCK
    rows = kc.shape[3]
    n_slc = seq // SLC_BLOCK
    hps = HEADS_PER_STEP
    seq_rows = lambda w: pl.BlockSpec((hps, seq, w), lambda b, hp, i: (hp, b, 0))
    vt_spec = pl.BlockSpec((hps, seq // KV_CHUNK, V_ROWS, KV_CHUNK), lambda b, hp, i: (hp, b, 0, 0))
    slot = np.arange(rows)
    cmp_idx = 4 * (slot % n_slc) + slot // n_slc
    cmp_end = np.where(cmp_idx < rows - 1, CMP_STRIDE * cmp_idx + CMP_BLOCK - 1, np.iinfo(np.int32).max)
    cmp_end = np.broadcast_to(cmp_end.astype(np.int32)[:, None], (rows, Q_BLOCK))
    off = Q_BLOCK * np.arange(WINDOW // Q_BLOCK + 1)[:, None, None]
    diff = off + np.arange(Q_BLOCK)[None, None, :] - np.arange(WIN_SPAN)[None, :, None]
    win_bias = np.where((diff >= 0) & (diff < WINDOW), 0.0, MASK_VALUE).astype(np.float32)
    counts = sorted({min(nb, n_slc) for nb in SELECT_BLOCKS})
    starts = [0] + [nb // 2 for nb in counts[:-1]]
    ranges = tuple(zip(starts, starts[1:] + [n_qb], counts))
    qcol = lambda r: pl.BlockSpec((hps * r, Q_BLOCK), lambda b, hp, i: (hp, b * n_qb + i))
    in_specs = [
        qcol(GROUP * HEAD_DIM), pl.BlockSpec((LANES, Q_BLOCK), lambda b, hp, i: (0, b * n_qb + i)),
        pl.BlockSpec((None, None, hps, rows, HEAD_DIM), lambda b, hp, i: (0, b, hp, 0, 0)),
        pl.BlockSpec((None, None, hps, HEAD_DIM, rows), lambda b, hp, i: (1, b, hp, 0, 0)),
        seq_rows(2 * LANES), vt_spec, seq_rows(HEAD_DIM), vt_spec,
        _const_spec(cmp_end.shape), _const_spec(win_bias.shape),
    ]
    return pl.pallas_call(
        functools.partial(_nsa_kernel, n_sel=min(N_SELECT, n_slc), ranges=ranges),
        grid=(batch, KV_HEADS // hps, n_qb), in_specs=in_specs,
        out_specs=pl.BlockSpec((Q_BLOCK, hps * GROUP * HEAD_DIM), lambda b, hp, i: (b * n_qb + i, hp)),
        out_shape=jax.ShapeDtypeStruct((n, D_MODEL), _F32),
        scratch_shapes=[pltpu.VMEM((hps, SEL_CHUNK, GROUP * Q_BLOCK), _F32)] * 2
        + [pltpu.VMEM((hps, 2 * LANES, GROUP * Q_BLOCK), _BF16)],
        compiler_params=pltpu.CompilerParams(dimension_semantics=("arbitrary",) * 3, vmem_limit_bytes=VMEM_LIMIT_BYTES),
        name="nsa_attention",
    )(qt, gnt, kc, vct, ks, vst, kw, vwt, cmp_end, win_bias)


def _memkv_kernel(mem_ref, w_ref, o_ref):
    o_ref[...] = _dot(mem_ref[...].astype(_BF16), w_ref[...]).astype(_BF16)


def _memkv(mem2, w_xkv):
    m = mem2.shape[0]
    return pl.pallas_call(
        _memkv_kernel, grid=(1,),
        in_specs=[pl.BlockSpec(mem2.shape, lambda i: (0, 0)), pl.BlockSpec(w_xkv.shape, lambda i: (0, 0))],
        out_specs=pl.BlockSpec((m, 2 * D_MODEL), lambda i: (0, 0)),
        out_shape=jax.ShapeDtypeStruct((m, 2 * D_MODEL), _BF16),
        compiler_params=pltpu.CompilerParams(vmem_limit_bytes=VMEM_LIMIT_BYTES),
        name="mem_kv",
    )(mem2, w_xkv)


def _trunk_kernel(a_ref, gb_ref, yb_ref, h_ref, kvm_ref, wo_ref, wxq_ref, wxo_ref, wf1_ref, wf2_ref,
                  g1_ref, b1_ref, g2_ref, b2_ref, g3_ref, b3_ref, o_ref):
    tm = a_ref.shape[0]
    halves = [slice(i * (tm // TRUNK_SPLIT), (i + 1) * (tm // TRUNK_SPLIT)) for i in range(TRUNK_SPLIT)]
    both = lambda fn, *xs: [fn(*(x[i] for x in xs)) for i in range(TRUNK_SPLIT)]

    mix = [(a_ref[r, :] + gb_ref[r, :] * yb_ref[r, :]).astype(_BF16) for r in halves]
    y1 = both(lambda m: _dot(m, wo_ref[...]), mix)
    h1 = [_layer_norm(ALPHA * h_ref[r, :] + y, g1_ref[...], b1_ref[...]) for r, y in zip(halves, y1)]

    qx = both(lambda x: _dot(x.astype(_BF16), wxq_ref[...]).astype(_BF16), h1)
    heads = [[] for _ in range(TRUNK_SPLIT)]
    for hh in range(XATTN_HEADS):
        cs = slice(hh * XATTN_HEAD_DIM, (hh + 1) * XATTN_HEAD_DIM)
        vcs = slice(D_MODEL + hh * XATTN_HEAD_DIM, D_MODEL + (hh + 1) * XATTN_HEAD_DIM)
        s = both(lambda q: _dot_nt(q[:, cs], kvm_ref[:, cs]) * (XATTN_HEAD_DIM ** -0.5), qx)
        e = both(lambda x: jnp.exp(x - jnp.max(x, axis=-1, keepdims=True)), s)
        p = both(lambda x: (x * (1.0 / jnp.sum(x, axis=-1, keepdims=True))).astype(_BF16), e)
        for i, o in enumerate(both(lambda x: _dot(x, kvm_ref[:, vcs]), p)):
            heads[i].append(o)
    xo = both(lambda hs: jnp.concatenate(hs, axis=1).astype(_BF16), heads)
    y2 = both(lambda x: _dot(x, wxo_ref[...]), xo)
    h2 = both(lambda x, y: _layer_norm(ALPHA * x + y, g2_ref[...], b2_ref[...]), h1, y2)

    h2b = both(lambda x: x.astype(_BF16), h2)
    ff = [None] * TRUNK_SPLIT
    for c in range(D_FF // D_MODEL):
        cs = slice(c * D_MODEL, (c + 1) * D_MODEL)
        act = both(lambda x: jnp.square(jnp.maximum(_dot(x, wf1_ref[:, cs]), 0.0)).astype(_BF16), h2b)
        part = both(lambda x: _dot(x, wf2_ref[cs, :]), act)
        ff = part if c == 0 else both(lambda x, y: x + y, ff, part)
    for r, x, y in zip(halves, h2, ff):
        o_ref[r, :] = _layer_norm(ALPHA * x + y, g3_ref[...], b3_ref[...])


def _trunk(a, gb, yb, h, kvm, weights, lns, seq, mem_len):
    n = a.shape[0]
    tm = TRUNK_TILE
    tiles_per_batch = seq // tm
    row = pl.BlockSpec((tm, D_MODEL), lambda i: (i, 0))
    in_specs = ([row, row, row, row, pl.BlockSpec((mem_len, 2 * D_MODEL), lambda i: (i // tiles_per_batch, 0))]
                + [_const_spec(w.shape) for w in weights] + [_const_spec(p.shape) for p in lns])
    return pl.pallas_call(
        _trunk_kernel, grid=(n // tm,), in_specs=in_specs, out_specs=row,
        out_shape=jax.ShapeDtypeStruct((n, D_MODEL), _F32),
        compiler_params=pltpu.CompilerParams(dimension_semantics=("arbitrary",), vmem_limit_bytes=VMEM_LIMIT_BYTES),
        name="trunk",
    )(a, gb, yb, h, kvm, *weights, *lns)


def _rope_constants():
    half = ROT_DIM // 2
    inv = ROPE_THETA ** (-jnp.arange(half, dtype=_F32) / half)
    d = jnp.arange(LANES) % HEAD_DIM
    inv_lane = jnp.where(d < ROT_DIM, inv[d % half], 0.0).astype(_F32)[None, :]
    neg_first = jnp.where(d < half, -1.0, 0.0).astype(_F32)[None, :]
    pos_second = jnp.where((d >= half) & (d < ROT_DIM), 1.0, 0.0).astype(_F32)[None, :]
    return inv_lane, neg_first, pos_second


def kernel(x, mem, positions, ln_in_g, ln_in_b, w_in, gmlp_ln_g, gmlp_ln_b, gmlp_ws, gmlp_bs, cmp_k_pe, cmp_k_w1, cmp_k_b1, cmp_k_w2, cmp_k_b2, cmp_v_pe, cmp_v_w1, cmp_v_b1, cmp_v_w2, cmp_v_b2, w_out, ln1_g, ln1_b, w_xq, w_xkv, w_xo, ln2_g, ln2_b, w_ff1, w_ff2, ln3_g, ln3_b):
    batch, seq, _ = x.shape
    mem_len = mem.shape[1]
    n = batch * seq
    n_slc = seq // SLC_BLOCK
    assert w_in.shape[0] == 1, "one layer"
    assert seq % SEL_CHUNK == 0 and TRIP_CHUNKS[-1] == 2 and seq >= WIN_SPAN and seq & (seq - 1) == 0 and n_slc <= LANES

    rope_consts = _rope_constants()
    vec = lambda p: p.reshape(1, -1)

    wi = w_in[0]
    o_u, o_v, o_q, o_kv, o_gn, o_ga, o_gb = (0, 1024, 2048, 3072, 3072 + 6 * KV_DIM, 3120 + 6 * KV_DIM, 4144 + 6 * KV_DIM)
    w_gn = jnp.pad(wi[:, o_gn:o_ga], ((0, 0), (0, LANES - 3 * NSA_HEADS)))
    w_parts = [wi[:, o_u:o_v], wi[:, o_v:o_q], wi[:, o_ga:o_gb], wi[:, o_q:o_kv], wi[:, o_gb:], wi[:, o_kv:o_gn], w_gn]
    w_parts = [w.astype(_BF16) for w in w_parts]

    h, a, gb, qt, kvc, ks, vst, kw, vwt, gnt = _project(
        x.reshape(n, D_MODEL), positions.reshape(n, 1), vec(ln_in_g), vec(ln_in_b), w_parts,
        vec(gmlp_ln_g[0]), vec(gmlp_ln_b[0]), gmlp_ws[0], gmlp_bs[0].T, rope_consts, seq)

    rows = seq // CMP_STRIDE
    flat = CMP_STRIDE * HEAD_DIM
    pos_end = jnp.pad(positions[:, CMP_BLOCK - 1::CMP_STRIDE], ((0, 0), (0, 1)))[:, :, None]
    pad_lanes = lambda w: jnp.pad(w, ((0, 0), (0, LANES - HEAD_DIM)))
    kvcmp, kvcmp_t = _compress(
        kvc.reshape(2 * KV_HEADS, batch, rows, flat), pos_end,
        jnp.stack([cmp_k_pe[0].reshape(2, flat), cmp_v_pe[0].reshape(2, flat)]),
        jnp.stack([cmp_k_w1[0], cmp_v_w1[0]]).astype(_BF16),
        jnp.stack([vec(cmp_k_b1[0]), vec(cmp_v_b1[0])]),
        jnp.stack([pad_lanes(cmp_k_w2[0]), pad_lanes(cmp_v_w2[0])]).astype(_BF16),
        jnp.stack([pad_lanes(vec(cmp_k_b2[0])), pad_lanes(vec(cmp_v_b2[0]))]),
        rope_consts, batch)

    yb = _nsa(qt, gnt, kvcmp, kvcmp_t, ks, vst, kw, vwt, batch, seq)

    kvm = _memkv(mem.reshape(batch * mem_len, D_MODEL), w_xkv[0].astype(_BF16))
    weights = [w.astype(_BF16) for w in (w_out[0], w_xq[0], w_xo[0], w_ff1[0], w_ff2[0])]
    lns = [vec(p[0]) for p in (ln1_g, ln1_b, ln2_g, ln2_b, ln3_g, ln3_b)]
    out = _trunk(a, gb, yb, h, kvm, weights, lns, seq, mem_len)
    return out.reshape(batch, seq, D_MODEL)
```

```python
import functools
import math

import jax
import jax.numpy as jnp
import numpy as np
from jax import lax
from jax.experimental import pallas as pl
from jax.experimental.pallas import tpu as pltpu

D_MODEL = 1024
LN_EPS = 1e-5
ALPHA = 2.0 ** 0.25
ROPE_THETA = 500000.0
GMLP_GROUPS = 8
GMLP_CHUNK = 128
NSA_HEADS = 16
HEAD_DIM = 64
KV_HEADS = 4
GROUP = NSA_HEADS // KV_HEADS
KV_DIM = KV_HEADS * HEAD_DIM
ROT_DIM = HEAD_DIM // 4
CMP_BLOCK = 32
CMP_STRIDE = 16
CMP_HIDDEN = 4 * HEAD_DIM
SLC_BLOCK = 64
N_SELECT = 16
WINDOW = 512
Q_BLOCK = 128
KV_CHUNK = 128
V_ROWS = 80
XATTN_HEADS = 4
XATTN_HEAD_DIM = D_MODEL // XATTN_HEADS
D_FF = 4 * D_MODEL

LANES = 128
PROJ_TILE = 512
PROJ_SPLIT = 2
TRUNK_TILE = 512
TRUNK_SPLIT = 2
SEL_CHUNK = 512
TRIP_CHUNKS = (8, 4, 2)
LANE_SPLIT = 2
HEADS_PER_STEP = 1
WIN_SPAN = WINDOW + Q_BLOCK
SELECT_BLOCKS = (32, 64, 96, 128)
WIN_PIECES = 5
MASK_VALUE = -1e30
LOG2_E = math.log2(math.e)
VMEM_LIMIT_BYTES = 56 * 1024 * 1024

_F32 = jnp.float32
_BF16 = jnp.bfloat16


def _layer_norm(x, g, b):
    mu = jnp.mean(x, axis=-1, keepdims=True)
    xc = x - mu
    var = jnp.mean(xc * xc, axis=-1, keepdims=True)
    return xc * lax.rsqrt(var + LN_EPS) * g + b


def _dot(a, b):
    return jnp.dot(a, b, preferred_element_type=_F32)


def _dot_nt(a, b):
    return lax.dot_general(a, b, (((1,), (1,)), ((), ())), preferred_element_type=_F32)


def _rope_tables(pos_f32, inv_lane, neg_first, pos_second):
    ang = pos_f32 * inv_lane
    c = jnp.cos(ang)
    s = jnp.sin(ang)
    return c, s * neg_first, s * pos_second


def _rope_block(x, tables):
    c, s_first, s_second = tables
    half = ROT_DIM // 2
    up = pltpu.roll(x, LANES - half, axis=1)
    down = pltpu.roll(x, half, axis=1)
    return x * c + up * s_first + down * s_second


def _const_spec(shape):
    nd = len(shape)
    return pl.BlockSpec(shape, lambda *_: (0,) * nd, pipeline_mode=pl.Buffered(1))


def _proj_kernel(x_ref, pos_ref, lng_ref, lnb_ref, wu_ref, wv_ref, wga_ref, wq_ref, wgb_ref, wkv_ref, wgn_ref,
                 glng_ref, glnb_ref, ws_ref, bst_ref, inv_ref, m1_ref, m2_ref,
                 h_ref, a_ref, gb_ref, qt_ref, kvc_ref, ks_ref, vst_ref, kw_ref, vwt_ref, gnt_ref, cmp_ref, *, seq):
    tm = x_ref.shape[0]
    part = tm // PROJ_SPLIT
    ti = lax.broadcasted_iota(jnp.int32, (GMLP_CHUNK, GMLP_CHUNK), 0)
    si = lax.broadcasted_iota(jnp.int32, (GMLP_CHUNK, GMLP_CHUNK), 1)
    w_tril = [jnp.where(si <= ti, ws_ref[g], 0.0).astype(_BF16) for g in range(GMLP_GROUPS)]
    bst = bst_ref[...]
    q_scale = HEAD_DIM ** -0.5 * LOG2_E
    heads_per_slab = LANES // HEAD_DIM
    blk_lane = lax.broadcasted_iota(jnp.int32, (part, LANES), 1)
    ones_col = (lax.broadcasted_iota(jnp.int32, (part, HEAD_DIM), 1) == 0).astype(_F32)

    def stages(r0):
        rs = slice(r0, r0 + part)
        h = _layer_norm(x_ref[rs, :], lng_ref[...], lnb_ref[...])
        h_ref[rs, :] = h
        hb = h.astype(_BF16)
        zu = _dot(hb, wu_ref[...])
        yield
        u_act = jax.nn.gelu(zu)
        zv = _dot(hb, wv_ref[...])
        yield
        vn = _layer_norm(jax.nn.gelu(zv), glng_ref[...], glnb_ref[...]).astype(_BF16)
        zga = _dot(hb, wga_ref[...])
        yield
        ua = u_act * jax.nn.sigmoid(zga)
        for g in range(GMLP_GROUPS):
            cs = slice(g * LANES, (g + 1) * LANES)
            for c in range(part // GMLP_CHUNK):
                cr = slice(c * GMLP_CHUNK, (c + 1) * GMLP_CHUNK)
                mixed = _dot(w_tril[g], vn[cr, cs]) + bst[:, g:g + 1]
                a_ref[r0 + c * GMLP_CHUNK:r0 + (c + 1) * GMLP_CHUNK, cs] = ua[cr, cs] * mixed
        zgb = _dot(hb, wgb_ref[...])
        yield
        gb_ref[rs, :] = jax.nn.sigmoid(zgb)
        zgn = _dot(hb, wgn_ref[...])
        yield
        gnt_ref[:, rs] = jax.nn.sigmoid(zgn).T
        zq = _dot(hb, wq_ref[...])
        yield
        tables = _rope_tables(pos_ref[rs, :].astype(_F32), inv_ref[...], m1_ref[...], m2_ref[...])
        for i in range(D_MODEL // LANES):
            cs = slice(i * LANES, (i + 1) * LANES)
            qt_ref[cs, rs] = (_rope_block(zq[:, cs], tables) * q_scale).T.astype(_BF16)
        zkv = _dot(hb, wkv_ref[...])
        yield
        for sl in range(2 * KV_DIM // LANES):
            cmp_ref[sl, rs, :] = zkv[:, sl * LANES:(sl + 1) * LANES]
        out_rows = slice(r0 // CMP_STRIDE, (r0 + part) // CMP_STRIDE)
        for sl in range(2 * KV_DIM // LANES):
            for l in range(CMP_STRIDE):
                pair = cmp_ref[sl, pl.ds(r0 + l, part // CMP_STRIDE, stride=CMP_STRIDE), :]
                for hh in range(heads_per_slab):
                    kvc_ref[sl * heads_per_slab + hh, out_rows, l * HEAD_DIM:(l + 1) * HEAD_DIM] = (
                        pair[:, hh * HEAD_DIM:(hh + 1) * HEAD_DIM])
        seq_pos = (pl.program_id(0) * tm + r0 + lax.broadcasted_iota(jnp.int32, (part, LANES), 0)) & (seq - 1)
        blk_onehot = jnp.where(lax.shift_right_logical(seq_pos, SLC_BLOCK.bit_length() - 1) == blk_lane, 1.0, 0.0)
        k_tail = jnp.concatenate([jnp.zeros((part, HEAD_DIM), _BF16), blk_onehot.astype(_BF16)], axis=1)
        for base, k_out, vt_out in ((2 * KV_DIM, ks_ref, vst_ref), (4 * KV_DIM, kw_ref, vwt_ref)):
            for i in range(KV_DIM // LANES):
                kr = _rope_block(zkv[:, base + i * LANES: base + (i + 1) * LANES], tables).astype(_BF16)
                for hh in range(heads_per_slab):
                    k_head = kr[:, hh * HEAD_DIM:(hh + 1) * HEAD_DIM]
                    if k_out is ks_ref:
                        k_out[2 * i + hh, rs, :] = jnp.concatenate([k_head, k_tail], axis=1)
                    else:
                        k_out[2 * i + hh, rs, :] = k_head
            for hh in range(KV_HEADS):
                v = zkv[:, base + KV_DIM + hh * HEAD_DIM: base + KV_DIM + (hh + 1) * HEAD_DIM]
                v_ext = jnp.concatenate([v, ones_col], axis=1)
                for c in range(part // KV_CHUNK):
                    vt_out[hh, r0 // KV_CHUNK + c] = v_ext[c * KV_CHUNK:(c + 1) * KV_CHUNK, :].T[:V_ROWS].astype(_BF16)
        yield

    for _ in zip(*[stages(i * part) for i in range(PROJ_SPLIT)]):
        pass


def _project(x2, pos2, ln_g, ln_b, w_parts, gln_g, gln_b, ws, bst, rope_consts, seq):
    n = x2.shape[0]
    tm = PROJ_TILE
    row = lambda shape: pl.BlockSpec(shape, lambda i: (i, 0))
    col = lambda shape: pl.BlockSpec(shape, lambda i: (0, i))
    head_rows = lambda nh, r, w: pl.BlockSpec((nh, r, w), lambda i: (0, i, 0))
    vt_spec = pl.BlockSpec((KV_HEADS, tm // KV_CHUNK, V_ROWS, KV_CHUNK), lambda i: (0, i, 0, 0))
    w_specs = [_const_spec(w.shape) for w in w_parts]
    in_specs = ([row((tm, D_MODEL)), row((tm, 1)), _const_spec(ln_g.shape), _const_spec(ln_b.shape)] + w_specs
                + [_const_spec(gln_g.shape), _const_spec(gln_b.shape), _const_spec(ws.shape), _const_spec(bst.shape)]
                + [_const_spec(c.shape) for c in rope_consts])
    out_shape = (
        jax.ShapeDtypeStruct((n, D_MODEL), _F32),
        jax.ShapeDtypeStruct((n, D_MODEL), _F32),
        jax.ShapeDtypeStruct((n, D_MODEL), _F32),
        jax.ShapeDtypeStruct((D_MODEL, n), _BF16),
        jax.ShapeDtypeStruct((2 * KV_HEADS, n // CMP_STRIDE, CMP_STRIDE * HEAD_DIM), _F32),
        jax.ShapeDtypeStruct((KV_HEADS, n, 2 * LANES), _BF16),
        jax.ShapeDtypeStruct((KV_HEADS, n // KV_CHUNK, V_ROWS, KV_CHUNK), _BF16),
        jax.ShapeDtypeStruct((KV_HEADS, n, HEAD_DIM), _BF16),
        jax.ShapeDtypeStruct((KV_HEADS, n // KV_CHUNK, V_ROWS, KV_CHUNK), _BF16),
        jax.ShapeDtypeStruct((LANES, n), _F32),
    )
    out_specs = (row((tm, D_MODEL)), row((tm, D_MODEL)), row((tm, D_MODEL)), col((D_MODEL, tm)),
                 head_rows(2 * KV_HEADS, tm // CMP_STRIDE, CMP_STRIDE * HEAD_DIM),
                 head_rows(KV_HEADS, tm, 2 * LANES), vt_spec, head_rows(KV_HEADS, tm, HEAD_DIM), vt_spec,
                 col((LANES, tm)))
    return pl.pallas_call(
        functools.partial(_proj_kernel, seq=seq), grid=(n // tm,), in_specs=in_specs,
        out_specs=out_specs, out_shape=out_shape,
        scratch_shapes=[pltpu.VMEM((2 * KV_DIM // LANES, tm, LANES), _F32)],
        compiler_params=pltpu.CompilerParams(dimension_semantics=("arbitrary",), vmem_limit_bytes=VMEM_LIMIT_BYTES),
        name="nsa_proj",
    )(x2, pos2, ln_g, ln_b, *w_parts, gln_g, gln_b, ws, bst, *rope_consts)


def _compress_kernel(x_ref, pos_ref, pe_ref, w1_ref, b1_ref, w2_ref, b2_ref, inv_ref, m1_ref, m2_ref,
                     o_ref, ot_ref, perm_ref):
    rows = x_ref.shape[0]
    is_key = (pl.program_id(1) == 0).astype(_F32)
    x = x_ref[...]
    half = CMP_STRIDE * HEAD_DIM
    ya = _dot((x + pe_ref[0:1, :]).astype(_BF16), w1_ref[0:half, :])
    yb = _dot((x + pe_ref[1:2, :]).astype(_BF16), w1_ref[half:2 * half, :])
    hid = jax.nn.gelu(ya + pltpu.roll(yb, rows - 1, axis=0) + b1_ref[...])
    out = _dot(hid.astype(_BF16), w2_ref[...]) + b2_ref[...]
    tables = _rope_tables(pos_ref[...].astype(_F32) * is_key, inv_ref[...], m1_ref[...], m2_ref[...])
    perm_ref[...] = _rope_block(out, tables)
    groups = rows // 4
    for r in range(4):
        part = perm_ref[pl.ds(r, groups, stride=4), :]
        o_ref[r * groups:(r + 1) * groups, :] = part[:, :HEAD_DIM].astype(_BF16)
        ot_ref[:, r * groups:(r + 1) * groups] = part.T[:HEAD_DIM, :].astype(_BF16)


def _compress(kvc, pos_end, pe, w1, b1, w2, b2, rope_consts, batch):
    rows = kvc.shape[2]
    flat = CMP_STRIDE * HEAD_DIM
    in_specs = [
        pl.BlockSpec((None, None, rows, flat), lambda b, kv, hh: (kv * KV_HEADS + hh, b, 0, 0)),
        pl.BlockSpec((None, rows, 1), lambda b, kv, hh: (b, 0, 0)),
        pl.BlockSpec((None, 2, flat), lambda b, kv, hh: (kv, 0, 0)),
        pl.BlockSpec((None, 2 * flat, CMP_HIDDEN), lambda b, kv, hh: (kv, 0, 0)),
        pl.BlockSpec((None, 1, CMP_HIDDEN), lambda b, kv, hh: (kv, 0, 0)),
        pl.BlockSpec((None, CMP_HIDDEN, LANES), lambda b, kv, hh: (kv, 0, 0)),
        pl.BlockSpec((None, 1, LANES), lambda b, kv, hh: (kv, 0, 0)),
    ] + [pl.BlockSpec(c.shape, lambda b, kv, hh: (0, 0)) for c in rope_consts]
    return pl.pallas_call(
        _compress_kernel, grid=(batch, 2, KV_HEADS), in_specs=in_specs,
        out_specs=(pl.BlockSpec((None, None, None, rows, HEAD_DIM), lambda b, kv, hh: (kv, b, hh, 0, 0)),
                   pl.BlockSpec((None, None, None, HEAD_DIM, rows), lambda b, kv, hh: (kv, b, hh, 0, 0))),
        out_shape=(jax.ShapeDtypeStruct((2, batch, KV_HEADS, rows, HEAD_DIM), _BF16),
                   jax.ShapeDtypeStruct((2, batch, KV_HEADS, HEAD_DIM, rows), _BF16)),
        scratch_shapes=[pltpu.VMEM((rows, LANES), _F32)],
        compiler_params=pltpu.CompilerParams(dimension_semantics=("arbitrary",) * 3, vmem_limit_bytes=VMEM_LIMIT_BYTES),
        name="nsa_compress",
    )(kvc, pos_end, pe, w1, b1, w2, b2, *rope_consts)


def _per_head(fn, s, *shared):
    return jnp.concatenate([fn(s[:, g * Q_BLOCK:(g + 1) * Q_BLOCK], *shared) for g in range(GROUP)], axis=1)


def _nsa_step(qt_ref, gnt_ref, kc_ref, vct_ref, ks_ref, vst_ref, kw_ref, vwt_ref, cend_ref, wbias_ref,
              o_ref, sa_ref, sb_ref, qx_ref, *, n_sel, n_blocks):
    rows = kc_ref.shape[1]
    n_slc = rows // 4
    seq = ks_ref.shape[1]
    heads = range(HEADS_PER_STEP)
    qb = pl.program_id(2)
    q0 = qb * Q_BLOCK
    t = q0 + lax.broadcasted_iota(jnp.int32, (1, Q_BLOCK), 1)
    w0 = pl.multiple_of(jnp.maximum(q0 - WINDOW, 0), KV_CHUNK)
    wc = lax.shift_right_logical(w0, KV_CHUNK.bit_length() - 1)
    n_seq_chunks = seq // SEL_CHUNK
    sub = SEL_CHUNK // KV_CHUNK
    piece = WIN_SPAN // WIN_PIECES

    valid_c = cend_ref[...] <= t
    has_c = t >= CMP_BLOCK - 1
    blk = lax.broadcasted_iota(jnp.int32, (n_slc, 1), 0)
    cur = lax.shift_right_logical(t, SLC_BLOCK.bit_length() - 1)
    forced = (blk == 0) | (blk == cur) | (blk == cur - 1)
    bi = lax.broadcasted_iota(jnp.int32, (n_slc, Q_BLOCK), 0)
    bf = bi.astype(_F32)
    past = bi < lax.shift_right_logical(q0, SLC_BLOCK.bit_length() - 1)
    bias_w = wbias_ref[jnp.minimum(qb, WINDOW // Q_BLOCK)]
    causal = jnp.where(lax.broadcasted_iota(jnp.int32, (Q_BLOCK, Q_BLOCK), 0)
                       <= lax.broadcasted_iota(jnp.int32, (Q_BLOCK, Q_BLOCK), 1), 0.0, MASK_VALUE)
    out = [None] * HEADS_PER_STEP

    def before_loop(h):
        qt = qt_ref[h * GROUP * HEAD_DIM:(h + 1) * GROUP * HEAD_DIM, :]
        q_t = jnp.concatenate([qt[g * HEAD_DIM:(g + 1) * HEAD_DIM, :] for g in range(GROUP)], axis=1)
        raw_c = _dot(kc_ref[h], q_t)
        raw_w = _dot(kw_ref[h, pl.ds(w0, WIN_SPAN), :], q_t)
        raw_d = _dot(ks_ref[h, pl.ds(q0, Q_BLOCK), :HEAD_DIM], q_t)
        raw_0 = _dot(ks_ref[h, 0:SEL_CHUNK, :HEAD_DIM], q_t)
        yield

        s_w = _per_head(lambda s, b: s + b, raw_w, bias_w)
        m_w = jnp.max(s_w, axis=0, keepdims=True)
        s_d = _per_head(lambda s, b: s + b, raw_d, causal)
        m_d = jnp.max(s_d, axis=0, keepdims=True)

        def select(nb):
            take = lambda x: x if nb == n_slc else jnp.concatenate(
                [x[r * n_slc:r * n_slc + nb] for r in range(4)], axis=0)

            s_c = _per_head(lambda s, v: jnp.where(v, s, MASK_VALUE), take(raw_c), take(valid_c))
            e_c = jnp.exp2(s_c - jnp.max(s_c, axis=0, keepdims=True))
            inv_c = _per_head(lambda l, ok: jnp.where(ok, 1.0 / l, 0.0), jnp.sum(e_c, axis=0, keepdims=True), has_c)
            p_c = e_c * inv_c
            p_mm = p_c.astype(_BF16)
            if nb < n_slc:
                gap = jnp.zeros((n_slc - nb, GROUP * Q_BLOCK), _BF16)
                p_mm = jnp.concatenate([x for r in range(4) for x in (p_mm[r * nb:(r + 1) * nb], gap)], axis=0)
            o_c = _dot(vct_ref[h], p_mm)

            p_sum = p_c[:, :Q_BLOCK]
            for g in range(1, GROUP):
                p_sum = p_sum + p_c[:, g * Q_BLOCK:(g + 1) * Q_BLOCK]
            parts = [p_sum[r * nb:(r + 1) * nb, :] for r in range(4)]
            prev3 = jnp.where(blk[:nb] == 0, 0.0, pltpu.roll(parts[3], 1, axis=0))
            p_slc = parts[0] + parts[1] + parts[2] + parts[3] + prev3

            st = jnp.where(forced[:nb], -jnp.inf, jnp.where(blk[:nb] <= cur, p_slc, -1.0))
            p_w_parts = []
            for r in range(n_sel - 3):
                mx = jnp.max(st, axis=0, keepdims=True)
                first = jnp.min(jnp.where(st == mx, bf[:nb], float(n_slc)), axis=0, keepdims=True)
                if r < WIN_PIECES:
                    part = jnp.exp2(s_w[r * piece:(r + 1) * piece, :] - m_w)
                    p_w_parts.append(part.astype(_BF16))
                    col_max = jnp.max(part, axis=0, keepdims=True)
                    tie = col_max[:, :Q_BLOCK]
                    for g in range(1, GROUP):
                        tie = jnp.maximum(tie, col_max[:, g * Q_BLOCK:(g + 1) * Q_BLOCK])
                    first = first + tie * 0.0
                st = jnp.where(bf[:nb] == first, -jnp.inf, st)
            sel = st == -jnp.inf
            pen = jnp.where(sel & past[:nb], 0.0, MASK_VALUE)
            if nb < n_slc:
                pen = jnp.concatenate([pen, jnp.full((n_slc - nb, Q_BLOCK), MASK_VALUE, _F32)], axis=0)
            return (o_c, pen) + tuple(p_w_parts)

        o_c, pen, *p_w_parts = select(n_blocks)
        yield
        p_d = jnp.exp2(s_d - m_d).astype(_BF16)
        if n_slc < LANES:
            pen = jnp.concatenate([pen, jnp.zeros((LANES - n_slc, Q_BLOCK), _F32)], axis=0)
        pen_heads = jnp.concatenate([pen] * GROUP, axis=1)
        for b in range(SEL_CHUNK // SLC_BLOCK):
            ks_rows = slice(b * SLC_BLOCK, (b + 1) * SLC_BLOCK)
            sa_ref[h, ks_rows, :] = raw_0[ks_rows, :] + pen_heads[b:b + 1, :]
        qx_ref[h, 0:HEAD_DIM, :] = q_t
        qx_ref[h, HEAD_DIM:LANES, :] = jnp.zeros((LANES - HEAD_DIM, GROUP * Q_BLOCK), _BF16)
        qx_ref[h, LANES:2 * LANES, :] = jnp.concatenate([pen.astype(_BF16)] * GROUP, axis=1)
        yield

        v_w = jnp.concatenate([vwt_ref[h, wc + i] for i in range(WIN_SPAN // KV_CHUNK)], axis=1)
        acc_w = _dot(v_w, jnp.concatenate(p_w_parts, axis=0))
        acc_d = _dot(vst_ref[h, qb], p_d)
        yield
        o_w = acc_w[:HEAD_DIM] * (1.0 / acc_w[HEAD_DIM:HEAD_DIM + 1])
        out[h] = (o_c, o_w, m_d, acc_d)
        yield

    for _ in zip(*[before_loop(h) for h in heads]):
        pass

    def sel_scores(h, c, buf):
        k0 = pl.multiple_of(jnp.minimum(c, n_seq_chunks - 1) * SEL_CHUNK, SEL_CHUNK)
        buf[h] = _dot(ks_ref[h, pl.ds(k0, SEL_CHUNK), :], qx_ref[h])

    def sel_consume(h, c, buf, m, acc):
        c0 = jnp.minimum(c, n_seq_chunks - 1) * sub
        v_t = jnp.concatenate([vst_ref[h, c0 + i] for i in range(sub)], axis=1)
        width = GROUP * Q_BLOCK // LANE_SPLIT
        ms, accs = [], []
        for part in range(LANE_SPLIT):
            ls = slice(part * width, (part + 1) * width)
            s = buf[h, :, ls]
            m_new = jnp.maximum(m[:, ls], jnp.max(s, axis=0, keepdims=True))
            p = jnp.exp2(s - m_new)
            accs.append(jnp.exp2(m[:, ls] - m_new) * acc[:, ls] + _dot(v_t, p.astype(_BF16)))
            ms.append(m_new)
        return jnp.concatenate(ms, axis=1), jnp.concatenate(accs, axis=1)

    bufs = (sa_ref, sb_ref)

    def sel_trip(per_trip, base):
        def body(i, carry):
            carry = list(carry)
            for j in range(per_trip):
                c = base + per_trip * i + j
                for h in heads:
                    sel_scores(h, c + 1, bufs[(j + 1) % 2])
                for h in heads:
                    carry[h] = sel_consume(h, c, bufs[j % 2], *carry[h])
            return tuple(carry)
        return body

    n_chunks = (q0 + SEL_CHUNK - 1) // SEL_CHUNK
    carry, done = tuple((out[h][2], out[h][3]) for h in heads), 0
    for per_trip in TRIP_CHUNKS:
        left = n_chunks - done
        trips = (left + 1) // 2 if per_trip == TRIP_CHUNKS[-1] else left // per_trip
        carry = lax.fori_loop(0, trips, sel_trip(per_trip, done), carry)
        done = done + trips * per_trip

    for h in heads:
        o_c, o_w = out[h][0], out[h][1]
        acc_s = carry[h][1]
        o_s = acc_s[:HEAD_DIM] * (1.0 / acc_s[HEAD_DIM:HEAD_DIM + 1])
        first_gate = 3 * GROUP * (pl.program_id(1) * HEADS_PER_STEP + h)
        gate = lambda g, k: gnt_ref[pl.ds(first_gate + 3 * g + k, 1), :]
        outs = []
        for g in range(GROUP):
            ls = slice(g * Q_BLOCK, (g + 1) * Q_BLOCK)
            outs.append(gate(g, 0) * o_c[:, ls] + gate(g, 1) * o_s[:, ls] + gate(g, 2) * o_w[:, ls])
        o_ref[:, h * GROUP * HEAD_DIM:(h + 1) * GROUP * HEAD_DIM] = jnp.concatenate(outs, axis=0).T


def _nsa_kernel(*refs, n_sel, ranges):
    qb = pl.program_id(2)
    for first, stop, n_blocks in ranges:
        @pl.when((qb >= first) & (qb < stop))
        def _():
            _nsa_step(*refs, n_sel=n_sel, n_blocks=n_blocks)


def _nsa(qt, gnt, kc, vct, ks, vst, kw, vwt, batch, seq):
    n = qt.shape[1]
    n_qb = seq // Q_BLOCK
    rows = kc.shape[3]
    n_slc = seq // SLC_BLOCK
    hps = HEADS_PER_STEP
    seq_rows = lambda w: pl.BlockSpec((hps, seq, w), lambda b, hp, i: (hp, b, 0))
    vt_spec = pl.BlockSpec((hps, seq // KV_CHUNK, V_ROWS, KV_CHUNK), lambda b, hp, i: (hp, b, 0, 0))
    slot = np.arange(rows)
    cmp_idx = 4 * (slot % n_slc) + slot // n_slc
    cmp_end = np.where(cmp_idx < rows - 1, CMP_STRIDE * cmp_idx + CMP_BLOCK - 1, np.iinfo(np.int32).max)
    cmp_end = np.broadcast_to(cmp_end.astype(np.int32)[:, None], (rows, Q_BLOCK))
    off = Q_BLOCK * np.arange(WINDOW // Q_BLOCK + 1)[:, None, None]
    diff = off + np.arange(Q_BLOCK)[None, None, :] - np.arange(WIN_SPAN)[None, :, None]
    win_bias = np.where((diff >= 0) & (diff < WINDOW), 0.0, MASK_VALUE).astype(np.float32)
    counts = sorted({min(nb, n_slc) for nb in SELECT_BLOCKS})
    starts = [0] + [nb // 2 for nb in counts[:-1]]
    ranges = tuple(zip(starts, starts[1:] + [n_qb], counts))
    qcol = lambda r: pl.BlockSpec((hps * r, Q_BLOCK), lambda b, hp, i: (hp, b * n_qb + i))
    in_specs = [
        qcol(GROUP * HEAD_DIM), pl.BlockSpec((LANES, Q_BLOCK), lambda b, hp, i: (0, b * n_qb + i)),
        pl.BlockSpec((None, None, hps, rows, HEAD_DIM), lambda b, hp, i: (0, b, hp, 0, 0)),
        pl.BlockSpec((None, None, hps, HEAD_DIM, rows), lambda b, hp, i: (1, b, hp, 0, 0)),
        seq_rows(2 * LANES), vt_spec, seq_rows(HEAD_DIM), vt_spec,
        _const_spec(cmp_end.shape), _const_spec(win_bias.shape),
    ]
    return pl.pallas_call(
        functools.partial(_nsa_kernel, n_sel=min(N_SELECT, n_slc), ranges=ranges),
        grid=(batch, KV_HEADS // hps, n_qb), in_specs=in_specs,
        out_specs=pl.BlockSpec((Q_BLOCK, hps * GROUP * HEAD_DIM), lambda b, hp, i: (b * n_qb + i, hp)),
        out_shape=jax.ShapeDtypeStruct((n, D_MODEL), _F32),
        scratch_shapes=[pltpu.VMEM((hps, SEL_CHUNK, GROUP * Q_BLOCK), _F32)] * 2
        + [pltpu.VMEM((hps, 2 * LANES, GROUP * Q_BLOCK), _BF16)],
        compiler_params=pltpu.CompilerParams(dimension_semantics=("arbitrary",) * 3, vmem_limit_bytes=VMEM_LIMIT_BYTES),
        name="nsa_attention",
    )(qt, gnt, kc, vct, ks, vst, kw, vwt, cmp_end, win_bias)


def _memkv_kernel(mem_ref, w_ref, o_ref):
    o_ref[...] = _dot(mem_ref[...].astype(_BF16), w_ref[...]).astype(_BF16)


def _memkv(mem2, w_xkv):
    m = mem2.shape[0]
    return pl.pallas_call(
        _memkv_kernel, grid=(1,),
        in_specs=[pl.BlockSpec(mem2.shape, lambda i: (0, 0)), pl.BlockSpec(w_xkv.shape, lambda i: (0, 0))],
        out_specs=pl.BlockSpec((m, 2 * D_MODEL), lambda i: (0, 0)),
        out_shape=jax.ShapeDtypeStruct((m, 2 * D_MODEL), _BF16),
        compiler_params=pltpu.CompilerParams(vmem_limit_bytes=VMEM_LIMIT_BYTES),
        name="mem_kv",
    )(mem2, w_xkv)


def _trunk_kernel(a_ref, gb_ref, yb_ref, h_ref, kvm_ref, wo_ref, wxq_ref, wxo_ref, wf1_ref, wf2_ref,
                  g1_ref, b1_ref, g2_ref, b2_ref, g3_ref, b3_ref, o_ref):
    tm = a_ref.shape[0]
    halves = [slice(i * (tm // TRUNK_SPLIT), (i + 1) * (tm // TRUNK_SPLIT)) for i in range(TRUNK_SPLIT)]
    both = lambda fn, *xs: [fn(*(x[i] for x in xs)) for i in range(TRUNK_SPLIT)]

    mix = [(a_ref[r, :] + gb_ref[r, :] * yb_ref[r, :]).astype(_BF16) for r in halves]
    y1 = both(lambda m: _dot(m, wo_ref[...]), mix)
    h1 = [_layer_norm(ALPHA * h_ref[r, :] + y, g1_ref[...], b1_ref[...]) for r, y in zip(halves, y1)]

    qx = both(lambda x: _dot(x.astype(_BF16), wxq_ref[...]).astype(_BF16), h1)
    heads = [[] for _ in range(TRUNK_SPLIT)]
    for hh in range(XATTN_HEADS):
        cs = slice(hh * XATTN_HEAD_DIM, (hh + 1) * XATTN_HEAD_DIM)
        vcs = slice(D_MODEL + hh * XATTN_HEAD_DIM, D_MODEL + (hh + 1) * XATTN_HEAD_DIM)
        s = both(lambda q: _dot_nt(q[:, cs], kvm_ref[:, cs]) * (XATTN_HEAD_DIM ** -0.5), qx)
        e = both(lambda x: jnp.exp(x - jnp.max(x, axis=-1, keepdims=True)), s)
        p = both(lambda x: (x * (1.0 / jnp.sum(x, axis=-1, keepdims=True))).astype(_BF16), e)
        for i, o in enumerate(both(lambda x: _dot(x, kvm_ref[:, vcs]), p)):
            heads[i].append(o)
    xo = both(lambda hs: jnp.concatenate(hs, axis=1).astype(_BF16), heads)
    y2 = both(lambda x: _dot(x, wxo_ref[...]), xo)
    h2 = both(lambda x, y: _layer_norm(ALPHA * x + y, g2_ref[...], b2_ref[...]), h1, y2)

    h2b = both(lambda x: x.astype(_BF16), h2)
    ff = [None] * TRUNK_SPLIT
    for c in range(D_FF // D_MODEL):
        cs = slice(c * D_MODEL, (c + 1) * D_MODEL)
        act = both(lambda x: jnp.square(jnp.maximum(_dot(x, wf1_ref[:, cs]), 0.0)).astype(_BF16), h2b)
        part = both(lambda x: _dot(x, wf2_ref[cs, :]), act)
        ff = part if c == 0 else both(lambda x, y: x + y, ff, part)
    for r, x, y in zip(halves, h2, ff):
        o_ref[r, :] = _layer_norm(ALPHA * x + y, g3_ref[...], b3_ref[...])


def _trunk(a, gb, yb, h, kvm, weights, lns, seq, mem_len):
    n = a.shape[0]
    tm = TRUNK_TILE
    tiles_per_batch = seq // tm
    row = pl.BlockSpec((tm, D_MODEL), lambda i: (i, 0))
    in_specs = ([row, row, row, row, pl.BlockSpec((mem_len, 2 * D_MODEL), lambda i: (i // tiles_per_batch, 0))]
                + [_const_spec(w.shape) for w in weights] + [_const_spec(p.shape) for p in lns])
    return pl.pallas_call(
        _trunk_kernel, grid=(n // tm,), in_specs=in_specs, out_specs=row,
        out_shape=jax.ShapeDtypeStruct((n, D_MODEL), _F32),
        compiler_params=pltpu.CompilerParams(dimension_semantics=("arbitrary",), vmem_limit_bytes=VMEM_LIMIT_BYTES),
        name="trunk",
    )(a, gb, yb, h, kvm, *weights, *lns)


def _rope_constants():
    half = ROT_DIM // 2
    inv = ROPE_THETA ** (-jnp.arange(half, dtype=_F32) / half)
    d = jnp.arange(LANES) % HEAD_DIM
    inv_lane = jnp.where(d < ROT_DIM, inv[d % half], 0.0).astype(_F32)[None, :]
    neg_first = jnp.where(d < half, -1.0, 0.0).astype(_F32)[None, :]
    pos_second = jnp.where((d >= half) & (d < ROT_DIM), 1.0, 0.0).astype(_F32)[None, :]
    return inv_lane, neg_first, pos_second


def kernel(x, mem, positions, ln_in_g, ln_in_b, w_in, gmlp_ln_g, gmlp_ln_b, gmlp_ws, gmlp_bs, cmp_k_pe, cmp_k_w1, cmp_k_b1, cmp_k_w2, cmp_k_b2, cmp_v_pe, cmp_v_w1, cmp_v_b1, cmp_v_w2, cmp_v_b2, w_out, ln1_g, ln1_b, w_xq, w_xkv, w_xo, ln2_g, ln2_b, w_ff1, w_ff2, ln3_g, ln3_b):
    batch, seq, _ = x.shape
    mem_len = mem.shape[1]
    n = batch * seq
    n_slc = seq // SLC_BLOCK
    assert w_in.shape[0] == 1, "one layer"
    assert seq % SEL_CHUNK == 0 and TRIP_CHUNKS[-1] == 2 and seq >= WIN_SPAN and seq & (seq - 1) == 0 and n_slc <= LANES

    rope_consts = _rope_constants()
    vec = lambda p: p.reshape(1, -1)

    wi = w_in[0]
    o_u, o_v, o_q, o_kv, o_gn, o_ga, o_gb = (0, 1024, 2048, 3072, 3072 + 6 * KV_DIM, 3120 + 6 * KV_DIM, 4144 + 6 * KV_DIM)
    w_gn = jnp.pad(wi[:, o_gn:o_ga], ((0, 0), (0, LANES - 3 * NSA_HEADS)))
    w_parts = [wi[:, o_u:o_v], wi[:, o_v:o_q], wi[:, o_ga:o_gb], wi[:, o_q:o_kv], wi[:, o_gb:], wi[:, o_kv:o_gn], w_gn]
    w_parts = [w.astype(_BF16) for w in w_parts]

    h, a, gb, qt, kvc, ks, vst, kw, vwt, gnt = _project(
        x.reshape(n, D_MODEL), positions.reshape(n, 1), vec(ln_in_g), vec(ln_in_b), w_parts,
        vec(gmlp_ln_g[0]), vec(gmlp_ln_b[0]), gmlp_ws[0], gmlp_bs[0].T, rope_consts, seq)

    rows = seq // CMP_STRIDE
    flat = CMP_STRIDE * HEAD_DIM
    pos_end = jnp.pad(positions[:, CMP_BLOCK - 1::CMP_STRIDE], ((0, 0), (0, 1)))[:, :, None]
    pad_lanes = lambda w: jnp.pad(w, ((0, 0), (0, LANES - HEAD_DIM)))
    kvcmp, kvcmp_t = _compress(
        kvc.reshape(2 * KV_HEADS, batch, rows, flat), pos_end,
        jnp.stack([cmp_k_pe[0].reshape(2, flat), cmp_v_pe[0].reshape(2, flat)]),
        jnp.stack([cmp_k_w1[0], cmp_v_w1[0]]).astype(_BF16),
        jnp.stack([vec(cmp_k_b1[0]), vec(cmp_v_b1[0])]),
        jnp.stack([pad_lanes(cmp_k_w2[0]), pad_lanes(cmp_v_w2[0])]).astype(_BF16),
        jnp.stack([pad_lanes(vec(cmp_k_b2[0])), pad_lanes(vec(cmp_v_b2[0]))]),
        rope_consts, batch)

    yb = _nsa(qt, gnt, kvcmp, kvcmp_t, ks, vst, kw, vwt, batch, seq)

    kvm = _memkv(mem.reshape(batch * mem_len, D_MODEL), w_xkv[0].astype(_BF16))
    weights = [w.astype(_BF16) for w in (w_out[0], w_xq[0], w_xo[0], w_ff1[0], w_ff2[0])]
    lns = [vec(p[0]) for p in (ln1_g, ln1_b, ln2_g, ln2_b, ln3_g, ln3_b)]
    out = _trunk(a, gb, yb, h, kvm, weights, lns, seq, mem_len)
    return out.reshape(batch, seq, D_MODEL)
```

```python
import functools
import math

import jax
import jax.numpy as jnp
import numpy as np
from jax import lax
from jax.experimental import pallas as pl
from jax.experimental.pallas import tpu as pltpu

D_MODEL = 1024
LN_EPS = 1e-5
ALPHA = 2.0 ** 0.25
ROPE_THETA = 500000.0
GMLP_GROUPS = 8
GMLP_CHUNK = 128
NSA_HEADS = 16
HEAD_DIM = 64
KV_HEADS = 4
GROUP = NSA_HEADS // KV_HEADS
KV_DIM = KV_HEADS * HEAD_DIM
ROT_DIM = HEAD_DIM // 4
CMP_BLOCK = 32
CMP_STRIDE = 16
CMP_HIDDEN = 4 * HEAD_DIM
SLC_BLOCK = 64
N_SELECT = 16
WINDOW = 512
Q_BLOCK = 128
KV_CHUNK = 128
V_ROWS = 80
XATTN_HEADS = 4
XATTN_HEAD_DIM = D_MODEL // XATTN_HEADS
D_FF = 4 * D_MODEL

LANES = 128
PROJ_TILE = 512
PROJ_SPLIT = 2
TRUNK_TILE = 512
TRUNK_SPLIT = 2
SEL_CHUNK = 512
TRIP_CHUNKS = (8, 4, 2)
LANE_SPLIT = 2
HEADS_PER_STEP = 1
WIN_SPAN = WINDOW + Q_BLOCK
SELECT_BLOCKS = (32, 64, 96, 128)
WIN_PIECES = 5
MASK_VALUE = -1e30
LOG2_E = math.log2(math.e)
VMEM_LIMIT_BYTES = 56 * 1024 * 1024

_F32 = jnp.float32
_BF16 = jnp.bfloat16


def _layer_norm(x, g, b):
    mu = jnp.mean(x, axis=-1, keepdims=True)
    xc = x - mu
    var = jnp.mean(xc * xc, axis=-1, keepdims=True)
    return xc * lax.rsqrt(var + LN_EPS) * g + b


def _dot(a, b):
    return jnp.dot(a, b, preferred_element_type=_F32)


def _dot_nt(a, b):
    return lax.dot_general(a, b, (((1,), (1,)), ((), ())), preferred_element_type=_F32)


def _rope_tables(pos_f32, inv_lane, neg_first, pos_second):
    ang = pos_f32 * inv_lane
    c = jnp.cos(ang)
    s = jnp.sin(ang)
    return c, s * neg_first, s * pos_second


def _rope_block(x, tables):
    c, s_first, s_second = tables
    half = ROT_DIM // 2
    up = pltpu.roll(x, LANES - half, axis=1)
    down = pltpu.roll(x, half, axis=1)
    return x * c + up * s_first + down * s_second


def _const_spec(shape):
    nd = len(shape)
    return pl.BlockSpec(shape, lambda *_: (0,) * nd, pipeline_mode=pl.Buffered(1))


def _proj_kernel(x_ref, pos_ref, lng_ref, lnb_ref, wu_ref, wv_ref, wga_ref, wq_ref, wgb_ref, wkv_ref, wgn_ref,
                 glng_ref, glnb_ref, ws_ref, bst_ref, inv_ref, m1_ref, m2_ref,
                 h_ref, a_ref, gb_ref, qt_ref, kvc_ref, ks_ref, vst_ref, kw_ref, vwt_ref, gnt_ref, cmp_ref, *, seq):
    tm = x_ref.shape[0]
    part = tm // PROJ_SPLIT
    ti = lax.broadcasted_iota(jnp.int32, (GMLP_CHUNK, GMLP_CHUNK), 0)
    si = lax.broadcasted_iota(jnp.int32, (GMLP_CHUNK, GMLP_CHUNK), 1)
    bst = bst_ref[...]
    q_scale = HEAD_DIM ** -0.5 * LOG2_E
    heads_per_slab = LANES // HEAD_DIM
    blk_lane = lax.broadcasted_iota(jnp.int32, (part, LANES), 1)
    ones_col = (lax.broadcasted_iota(jnp.int32, (part, HEAD_DIM), 1) == 0).astype(_F32)

    def stages(r0):
        rs = slice(r0, r0 + part)
        h = _layer_norm(x_ref[rs, :], lng_ref[...], lnb_ref[...])
        h_ref[rs, :] = h
        hb = h.astype(_BF16)
        zu = _dot(hb, wu_ref[...])
        yield
        u_act = jax.nn.gelu(zu)
        zv = _dot(hb, wv_ref[...])
        yield
        vn = _layer_norm(jax.nn.gelu(zv), glng_ref[...], glnb_ref[...]).astype(_BF16)
        zga = _dot(hb, wga_ref[...])
        yield
        ua = u_act * jax.nn.sigmoid(zga)
        for g in range(GMLP_GROUPS):
            cs = slice(g * LANES, (g + 1) * LANES)
            w_tril = jnp.where(si <= ti, ws_ref[g], 0.0).astype(_BF16)
            for c in range(part // GMLP_CHUNK):
                cr = slice(c * GMLP_CHUNK, (c + 1) * GMLP_CHUNK)
                mixed = _dot(w_tril, vn[cr, cs]) + bst[:, g:g + 1]
                a_ref[r0 + c * GMLP_CHUNK:r0 + (c + 1) * GMLP_CHUNK, cs] = ua[cr, cs] * mixed
        zgb = _dot(hb, wgb_ref[...])
        yield
        gb_ref[rs, :] = jax.nn.sigmoid(zgb)
        zgn = _dot(hb, wgn_ref[...])
        yield
        gnt_ref[:, rs] = jax.nn.sigmoid(zgn).T
        zq = _dot(hb, wq_ref[...])
        yield
        tables = _rope_tables(pos_ref[rs, :].astype(_F32), inv_ref[...], m1_ref[...], m2_ref[...])
        for i in range(D_MODEL // LANES):
            cs = slice(i * LANES, (i + 1) * LANES)
            qt_ref[cs, rs] = (_rope_block(zq[:, cs], tables) * q_scale).T.astype(_BF16)
        zkv = _dot(hb, wkv_ref[...])
        yield
        for sl in range(2 * KV_DIM // LANES):
            cmp_ref[sl, rs, :] = zkv[:, sl * LANES:(sl + 1) * LANES]
        out_rows = slice(r0 // CMP_STRIDE, (r0 + part) // CMP_STRIDE)
        for sl in range(2 * KV_DIM // LANES):
            for l in range(CMP_STRIDE):
                pair = cmp_ref[sl, pl.ds(r0 + l, part // CMP_STRIDE, stride=CMP_STRIDE), :]
                for hh in range(heads_per_slab):
                    kvc_ref[sl * heads_per_slab + hh, out_rows, l * HEAD_DIM:(l + 1) * HEAD_DIM] = (
                        pair[:, hh * HEAD_DIM:(hh + 1) * HEAD_DIM])
        seq_pos = (pl.program_id(0) * tm + r0 + lax.broadcasted_iota(jnp.int32, (part, LANES), 0)) & (seq - 1)
        blk_onehot = jnp.where(lax.shift_right_logical(seq_pos, SLC_BLOCK.bit_length() - 1) == blk_lane, 1.0, 0.0)
        k_tail = jnp.concatenate([jnp.zeros((part, HEAD_DIM), _BF16), blk_onehot.astype(_BF16)], axis=1)
        for base, k_out, vt_out in ((2 * KV_DIM, ks_ref, vst_ref), (4 * KV_DIM, kw_ref, vwt_ref)):
            for i in range(KV_DIM // LANES):
                kr = _rope_block(zkv[:, base + i * LANES: base + (i + 1) * LANES], tables).astype(_BF16)
                for hh in range(heads_per_slab):
                    k_head = kr[:, hh * HEAD_DIM:(hh + 1) * HEAD_DIM]
                    if k_out is ks_ref:
                        k_out[2 * i + hh, rs, :] = jnp.concatenate([k_head, k_tail], axis=1)
                    else:
                        k_out[2 * i + hh, rs, :] = k_head
            for hh in range(KV_HEADS):
                v = zkv[:, base + KV_DIM + hh * HEAD_DIM: base + KV_DIM + (hh + 1) * HEAD_DIM]
                v_ext = jnp.concatenate([v, ones_col], axis=1)
                for c in range(part // KV_CHUNK):
                    vt_out[hh, r0 // KV_CHUNK + c] = v_ext[c * KV_CHUNK:(c + 1) * KV_CHUNK, :].T[:V_ROWS].astype(_BF16)
        yield

    for _ in zip(*[stages(i * part) for i in range(PROJ_SPLIT)]):
        pass


def _project(x2, pos2, ln_g, ln_b, w_parts, gln_g, gln_b, ws, bst, rope_consts, seq):
    n = x2.shape[0]
    tm = PROJ_TILE
    row = lambda shape: pl.BlockSpec(shape, lambda i: (i, 0))
    col = lambda shape: pl.BlockSpec(shape, lambda i: (0, i))
    head_rows = lambda nh, r, w: pl.BlockSpec((nh, r, w), lambda i: (0, i, 0))
    vt_spec = pl.BlockSpec((KV_HEADS, tm // KV_CHUNK, V_ROWS, KV_CHUNK), lambda i: (0, i, 0, 0))
    w_specs = [_const_spec(w.shape) for w in w_parts]
    in_specs = ([row((tm, D_MODEL)), row((tm, 1)), _const_spec(ln_g.shape), _const_spec(ln_b.shape)] + w_specs
                + [_const_spec(gln_g.shape), _const_spec(gln_b.shape), _const_spec(ws.shape), _const_spec(bst.shape)]
                + [_const_spec(c.shape) for c in rope_consts])
    out_shape = (
        jax.ShapeDtypeStruct((n, D_MODEL), _F32),
        jax.ShapeDtypeStruct((n, D_MODEL), _F32),
        jax.ShapeDtypeStruct((n, D_MODEL), _F32),
        jax.ShapeDtypeStruct((D_MODEL, n), _BF16),
        jax.ShapeDtypeStruct((2 * KV_HEADS, n // CMP_STRIDE, CMP_STRIDE * HEAD_DIM), _F32),
        jax.ShapeDtypeStruct((KV_HEADS, n, 2 * LANES), _BF16),
        jax.ShapeDtypeStruct((KV_HEADS, n // KV_CHUNK, V_ROWS, KV_CHUNK), _BF16),
        jax.ShapeDtypeStruct((KV_HEADS, n, HEAD_DIM), _BF16),
        jax.ShapeDtypeStruct((KV_HEADS, n // KV_CHUNK, V_ROWS, KV_CHUNK), _BF16),
        jax.ShapeDtypeStruct((LANES, n), _F32),
    )
    out_specs = (row((tm, D_MODEL)), row((tm, D_MODEL)), row((tm, D_MODEL)), col((D_MODEL, tm)),
                 head_rows(2 * KV_HEADS, tm // CMP_STRIDE, CMP_STRIDE * HEAD_DIM),
                 head_rows(KV_HEADS, tm, 2 * LANES), vt_spec, head_rows(KV_HEADS, tm, HEAD_DIM), vt_spec,
                 col((LANES, tm)))
    return pl.pallas_call(
        functools.partial(_proj_kernel, seq=seq), grid=(n // tm,), in_specs=in_specs,
        out_specs=out_specs, out_shape=out_shape,
        scratch_shapes=[pltpu.VMEM((2 * KV_DIM // LANES, tm, LANES), _F32)],
        compiler_params=pltpu.CompilerParams(dimension_semantics=("arbitrary",), vmem_limit_bytes=VMEM_LIMIT_BYTES),
        name="nsa_proj",
    )(x2, pos2, ln_g, ln_b, *w_parts, gln_g, gln_b, ws, bst, *rope_consts)


def _compress_kernel(x_ref, pos_ref, pe_ref, w1_ref, b1_ref, w2_ref, b2_ref, inv_ref, m1_ref, m2_ref,
                     o_ref, ot_ref, perm_ref):
    rows = x_ref.shape[0]
    is_key = (pl.program_id(1) == 0).astype(_F32)
    x = x_ref[...]
    half = CMP_STRIDE * HEAD_DIM
    ya = _dot((x + pe_ref[0:1, :]).astype(_BF16), w1_ref[0:half, :])
    yb = _dot((x + pe_ref[1:2, :]).astype(_BF16), w1_ref[half:2 * half, :])
    hid = jax.nn.gelu(ya + pltpu.roll(yb, rows - 1, axis=0) + b1_ref[...])
    out = _dot(hid.astype(_BF16), w2_ref[...]) + b2_ref[...]
    tables = _rope_tables(pos_ref[...].astype(_F32) * is_key, inv_ref[...], m1_ref[...], m2_ref[...])
    perm_ref[...] = _rope_block(out, tables)
    groups = rows // 4
    for r in range(4):
        part = perm_ref[pl.ds(r, groups, stride=4), :]
        o_ref[r * groups:(r + 1) * groups, :] = part[:, :HEAD_DIM].astype(_BF16)
        ot_ref[:, r * groups:(r + 1) * groups] = part.T[:HEAD_DIM, :].astype(_BF16)


def _compress(kvc, pos_end, pe, w1, b1, w2, b2, rope_consts, batch):
    rows = kvc.shape[2]
    flat = CMP_STRIDE * HEAD_DIM
    in_specs = [
        pl.BlockSpec((None, None, rows, flat), lambda b, kv, hh: (kv * KV_HEADS + hh, b, 0, 0)),
        pl.BlockSpec((None, rows, 1), lambda b, kv, hh: (b, 0, 0)),
        pl.BlockSpec((None, 2, flat), lambda b, kv, hh: (kv, 0, 0)),
        pl.BlockSpec((None, 2 * flat, CMP_HIDDEN), lambda b, kv, hh: (kv, 0, 0)),
        pl.BlockSpec((None, 1, CMP_HIDDEN), lambda b, kv, hh: (kv, 0, 0)),
        pl.BlockSpec((None, CMP_HIDDEN, LANES), lambda b, kv, hh: (kv, 0, 0)),
        pl.BlockSpec((None, 1, LANES), lambda b, kv, hh: (kv, 0, 0)),
    ] + [pl.BlockSpec(c.shape, lambda b, kv, hh: (0, 0)) for c in rope_consts]
    return pl.pallas_call(
        _compress_kernel, grid=(batch, 2, KV_HEADS), in_specs=in_specs,
        out_specs=(pl.BlockSpec((None, None, None, rows, HEAD_DIM), lambda b, kv, hh: (kv, b, hh, 0, 0)),
                   pl.BlockSpec((None, None, None, HEAD_DIM, rows), lambda b, kv, hh: (kv, b, hh, 0, 0))),
        out_shape=(jax.ShapeDtypeStruct((2, batch, KV_HEADS, rows, HEAD_DIM), _BF16),
                   jax.ShapeDtypeStruct((2, batch, KV_HEADS, HEAD_DIM, rows), _BF16)),
        scratch_shapes=[pltpu.VMEM((rows, LANES), _F32)],
        compiler_params=pltpu.CompilerParams(dimension_semantics=("arbitrary",) * 3, vmem_limit_bytes=VMEM_LIMIT_BYTES),
        name="nsa_compress",
    )(kvc, pos_end, pe, w1, b1, w2, b2, *rope_consts)


def _per_head(fn, s, *shared):
    return jnp.concatenate([fn(s[:, g * Q_BLOCK:(g + 1) * Q_BLOCK], *shared) for g in range(GROUP)], axis=1)


def _nsa_step(qt_ref, gnt_ref, kc_ref, vct_ref, ks_ref, vst_ref, kw_ref, vwt_ref, cend_ref, wbias_ref,
              o_ref, sa_ref, sb_ref, qx_ref, *, n_sel, n_blocks):
    rows = kc_ref.shape[1]
    n_slc = rows // 4
    seq = ks_ref.shape[1]
    heads = range(HEADS_PER_STEP)
    qb = pl.program_id(2)
    q0 = qb * Q_BLOCK
    t = q0 + lax.broadcasted_iota(jnp.int32, (1, Q_BLOCK), 1)
    w0 = pl.multiple_of(jnp.maximum(q0 - WINDOW, 0), KV_CHUNK)
    wc = lax.shift_right_logical(w0, KV_CHUNK.bit_length() - 1)
    n_seq_chunks = seq // SEL_CHUNK
    sub = SEL_CHUNK // KV_CHUNK
    piece = WIN_SPAN // WIN_PIECES

    valid_c = cend_ref[...] <= t
    has_c = t >= CMP_BLOCK - 1
    blk = lax.broadcasted_iota(jnp.int32, (n_slc, 1), 0)
    cur = lax.shift_right_logical(t, SLC_BLOCK.bit_length() - 1)
    forced = (blk == 0) | (blk == cur) | (blk == cur - 1)
    bi = lax.broadcasted_iota(jnp.int32, (n_slc, Q_BLOCK), 0)
    bf = bi.astype(_F32)
    past = bi < lax.shift_right_logical(q0, SLC_BLOCK.bit_length() - 1)
    bias_w = wbias_ref[jnp.minimum(qb, WINDOW // Q_BLOCK)]
    causal = jnp.where(lax.broadcasted_iota(jnp.int32, (Q_BLOCK, Q_BLOCK), 0)
                       <= lax.broadcasted_iota(jnp.int32, (Q_BLOCK, Q_BLOCK), 1), 0.0, MASK_VALUE)
    out = [None] * HEADS_PER_STEP

    def before_loop(h):
        qt = qt_ref[h * GROUP * HEAD_DIM:(h + 1) * GROUP * HEAD_DIM, :]
        q_t = jnp.concatenate([qt[g * HEAD_DIM:(g + 1) * HEAD_DIM, :] for g in range(GROUP)], axis=1)
        raw_c = _dot(kc_ref[h], q_t)
        raw_w = _dot(kw_ref[h, pl.ds(w0, WIN_SPAN), :], q_t)
        raw_d = _dot(ks_ref[h, pl.ds(q0, Q_BLOCK), :HEAD_DIM], q_t)
        raw_0 = _dot(ks_ref[h, 0:SEL_CHUNK, :HEAD_DIM], q_t)
        yield

        s_w = _per_head(lambda s, b: s + b, raw_w, bias_w)
        m_w = jnp.max(s_w, axis=0, keepdims=True)
        s_d = _per_head(lambda s, b: s + b, raw_d, causal)
        m_d = jnp.max(s_d, axis=0, keepdims=True)

        def select(nb):
            take = lambda x: x if nb == n_slc else jnp.concatenate(
                [x[r * n_slc:r * n_slc + nb] for r in range(4)], axis=0)

            s_c = _per_head(lambda s, v: jnp.where(v, s, MASK_VALUE), take(raw_c), take(valid_c))
            e_c = jnp.exp2(s_c - jnp.max(s_c, axis=0, keepdims=True))
            inv_c = _per_head(lambda l, ok: jnp.where(ok, 1.0 / l, 0.0), jnp.sum(e_c, axis=0, keepdims=True), has_c)
            p_c = e_c * inv_c
            p_mm = p_c.astype(_BF16)
            if nb < n_slc:
                gap = jnp.zeros((n_slc - nb, GROUP * Q_BLOCK), _BF16)
                p_mm = jnp.concatenate([x for r in range(4) for x in (p_mm[r * nb:(r + 1) * nb], gap)], axis=0)
            o_c = _dot(vct_ref[h], p_mm)

            p_sum = p_c[:, :Q_BLOCK]
            for g in range(1, GROUP):
                p_sum = p_sum + p_c[:, g * Q_BLOCK:(g + 1) * Q_BLOCK]
            parts = [p_sum[r * nb:(r + 1) * nb, :] for r in range(4)]
            prev3 = jnp.where(blk[:nb] == 0, 0.0, pltpu.roll(parts[3], 1, axis=0))
            p_slc = parts[0] + parts[1] + parts[2] + parts[3] + prev3

            st = jnp.where(forced[:nb], -jnp.inf, jnp.where(blk[:nb] <= cur, p_slc, -1.0))
            p_w_parts = []
            for r in range(n_sel - 3):
                mx = jnp.max(st, axis=0, keepdims=True)
                first = jnp.min(jnp.where(st == mx, bf[:nb], float(n_slc)), axis=0, keepdims=True)
                if r < WIN_PIECES:
                    part = jnp.exp2(s_w[r * piece:(r + 1) * piece, :] - m_w)
                    p_w_parts.append(part.astype(_BF16))
                    col_max = jnp.max(part, axis=0, keepdims=True)
                    tie = col_max[:, :Q_BLOCK]
                    for g in range(1, GROUP):
                        tie = jnp.maximum(tie, col_max[:, g * Q_BLOCK:(g + 1) * Q_BLOCK])
                    first = first + tie * 0.0
                st = jnp.where(bf[:nb] == first, -jnp.inf, st)
            sel = st == -jnp.inf
            pen = jnp.where(sel & past[:nb], 0.0, MASK_VALUE)
            if nb < n_slc:
                pen = jnp.concatenate([pen, jnp.full((n_slc - nb, Q_BLOCK), MASK_VALUE, _F32)], axis=0)
            return (o_c, pen) + tuple(p_w_parts)

        o_c, pen, *p_w_parts = select(n_blocks)
        yield
        p_d = jnp.exp2(s_d - m_d).astype(_BF16)
        if n_slc < LANES:
            pen = jnp.concatenate([pen, jnp.zeros((LANES - n_slc, Q_BLOCK), _F32)], axis=0)
        pen_heads = jnp.concatenate([pen] * GROUP, axis=1)
        for b in range(SEL_CHUNK // SLC_BLOCK):
            ks_rows = slice(b * SLC_BLOCK, (b + 1) * SLC_BLOCK)
            sa_ref[h, ks_rows, :] = raw_0[ks_rows, :] + pen_heads[b:b + 1, :]
        qx_ref[h, 0:HEAD_DIM, :] = q_t
        qx_ref[h, HEAD_DIM:LANES, :] = jnp.zeros((LANES - HEAD_DIM, GROUP * Q_BLOCK), _BF16)
        qx_ref[h, LANES:2 * LANES, :] = jnp.concatenate([pen.astype(_BF16)] * GROUP, axis=1)
        yield

        v_w = jnp.concatenate([vwt_ref[h, wc + i] for i in range(WIN_SPAN // KV_CHUNK)], axis=1)
        acc_w = _dot(v_w, jnp.concatenate(p_w_parts, axis=0))
        acc_d = _dot(vst_ref[h, qb], p_d)
        yield
        o_w = acc_w[:HEAD_DIM] * (1.0 / acc_w[HEAD_DIM:HEAD_DIM + 1])
        out[h] = (o_c, o_w, m_d, acc_d)
        yield

    for _ in zip(*[before_loop(h) for h in heads]):
        pass

    def sel_scores(h, c, buf):
        k0 = pl.multiple_of(jnp.minimum(c, n_seq_chunks - 1) * SEL_CHUNK, SEL_CHUNK)
        buf[h] = _dot(ks_ref[h, pl.ds(k0, SEL_CHUNK), :], qx_ref[h])

    def sel_consume(h, c, buf, m, acc):
        c0 = jnp.minimum(c, n_seq_chunks - 1) * sub
        v_t = jnp.concatenate([vst_ref[h, c0 + i] for i in range(sub)], axis=1)
        width = GROUP * Q_BLOCK // LANE_SPLIT
        ms, accs = [], []
        for part in range(LANE_SPLIT):
            ls = slice(part * width, (part + 1) * width)
            s = buf[h, :, ls]
            m_new = jnp.maximum(m[:, ls], jnp.max(s, axis=0, keepdims=True))
            p = jnp.exp2(s - m_new)
            accs.append(jnp.exp2(m[:, ls] - m_new) * acc[:, ls] + _dot(v_t, p.astype(_BF16)))
            ms.append(m_new)
        return jnp.concatenate(ms, axis=1), jnp.concatenate(accs, axis=1)

    bufs = (sa_ref, sb_ref)

    def sel_trip(per_trip, base):
        def body(i, carry):
            carry = list(carry)
            for j in range(per_trip):
                c = base + per_trip * i + j
                for h in heads:
                    sel_scores(h, c + 1, bufs[(j + 1) % 2])
                for h in heads:
                    carry[h] = sel_consume(h, c, bufs[j % 2], *carry[h])
            return tuple(carry)
        return body

    n_chunks = (q0 + SEL_CHUNK - 1) // SEL_CHUNK
    carry, done = tuple((out[h][2], out[h][3]) for h in heads), 0
    for per_trip in TRIP_CHUNKS:
        left = n_chunks - done
        trips = (left + 1) // 2 if per_trip == TRIP_CHUNKS[-1] else left // per_trip
        carry = lax.fori_loop(0, trips, sel_trip(per_trip, done), carry)
        done = done + trips * per_trip

    for h in heads:
        o_c, o_w = out[h][0], out[h][1]
        acc_s = carry[h][1]
        o_s = acc_s[:HEAD_DIM] * (1.0 / acc_s[HEAD_DIM:HEAD_DIM + 1])
        first_gate = 3 * GROUP * (pl.program_id(1) * HEADS_PER_STEP + h)
        gate = lambda g, k: gnt_ref[pl.ds(first_gate + 3 * g + k, 1), :]
        outs = []
        for g in range(GROUP):
            ls = slice(g * Q_BLOCK, (g + 1) * Q_BLOCK)
            outs.append(gate(g, 0) * o_c[:, ls] + gate(g, 1) * o_s[:, ls] + gate(g, 2) * o_w[:, ls])
        o_ref[:, h * GROUP * HEAD_DIM:(h + 1) * GROUP * HEAD_DIM] = jnp.concatenate(outs, axis=0).T


def _nsa_kernel(*refs, n_sel, ranges):
    qb = pl.program_id(2)
    for first, stop, n_blocks in ranges:
        @pl.when((qb >= first) & (qb < stop))
        def _():
            _nsa_step(*refs, n_sel=n_sel, n_blocks=n_blocks)


def _nsa(qt, gnt, kc, vct, ks, vst, kw, vwt, batch, seq):
    n = qt.shape[1]
    n_qb = seq // Q_BLOCK
    rows = kc.shape[3]
    n_slc = seq // SLC_BLOCK
    hps = HEADS_PER_STEP
    seq_rows = lambda w: pl.BlockSpec((hps, seq, w), lambda b, hp, i: (hp, b, 0))
    vt_spec = pl.BlockSpec((hps, seq // KV_CHUNK, V_ROWS, KV_CHUNK), lambda b, hp, i: (hp, b, 0, 0))
    slot = np.arange(rows)
    cmp_idx = 4 * (slot % n_slc) + slot // n_slc
    cmp_end = np.where(cmp_idx < rows - 1, CMP_STRIDE * cmp_idx + CMP_BLOCK - 1, np.iinfo(np.int32).max)
    cmp_end = np.broadcast_to(cmp_end.astype(np.int32)[:, None], (rows, Q_BLOCK))
    off = Q_BLOCK * np.arange(WINDOW // Q_BLOCK + 1)[:, None, None]
    diff = off + np.arange(Q_BLOCK)[None, None, :] - np.arange(WIN_SPAN)[None, :, None]
    win_bias = np.where((diff >= 0) & (diff < WINDOW), 0.0, MASK_VALUE).astype(np.float32)
    counts = sorted({min(nb, n_slc) for nb in SELECT_BLOCKS})
    starts = [0] + [nb // 2 for nb in counts[:-1]]
    ranges = tuple(zip(starts, starts[1:] + [n_qb], counts))
    qcol = lambda r: pl.BlockSpec((hps * r, Q_BLOCK), lambda b, hp, i: (hp, b * n_qb + i))
    in_specs = [
        qcol(GROUP * HEAD_DIM), pl.BlockSpec((LANES, Q_BLOCK), lambda b, hp, i: (0, b * n_qb + i)),
        pl.BlockSpec((None, None, hps, rows, HEAD_DIM), lambda b, hp, i: (0, b, hp, 0, 0)),
        pl.BlockSpec((None, None, hps, HEAD_DIM, rows), lambda b, hp, i: (1, b, hp, 0, 0)),
        seq_rows(2 * LANES), vt_spec, seq_rows(HEAD_DIM), vt_spec,
        _const_spec(cmp_end.shape), _const_spec(win_bias.shape),
    ]
    return pl.pallas_call(
        functools.partial(_nsa_kernel, n_sel=min(N_SELECT, n_slc), ranges=ranges),
        grid=(batch, KV_HEADS // hps, n_qb), in_specs=in_specs,
        out_specs=pl.BlockSpec((Q_BLOCK, hps * GROUP * HEAD_DIM), lambda b, hp, i: (b * n_qb + i, hp)),
        out_shape=jax.ShapeDtypeStruct((n, D_MODEL), _F32),
        scratch_shapes=[pltpu.VMEM((hps, SEL_CHUNK, GROUP * Q_BLOCK), _F32)] * 2
        + [pltpu.VMEM((hps, 2 * LANES, GROUP * Q_BLOCK), _BF16)],
        compiler_params=pltpu.CompilerParams(dimension_semantics=("arbitrary",) * 3, vmem_limit_bytes=VMEM_LIMIT_BYTES),
        name="nsa_attention",
    )(qt, gnt, kc, vct, ks, vst, kw, vwt, cmp_end, win_bias)


def _memkv_kernel(mem_ref, w_ref, o_ref):
    o_ref[...] = _dot(mem_ref[...].astype(_BF16), w_ref[...]).astype(_BF16)


def _memkv(mem2, w_xkv):
    m = mem2.shape[0]
    return pl.pallas_call(
        _memkv_kernel, grid=(1,),
        in_specs=[pl.BlockSpec(mem2.shape, lambda i: (0, 0)), pl.BlockSpec(w_xkv.shape, lambda i: (0, 0))],
        out_specs=pl.BlockSpec((m, 2 * D_MODEL), lambda i: (0, 0)),
        out_shape=jax.ShapeDtypeStruct((m, 2 * D_MODEL), _BF16),
        compiler_params=pltpu.CompilerParams(vmem_limit_bytes=VMEM_LIMIT_BYTES),
        name="mem_kv",
    )(mem2, w_xkv)


def _trunk_kernel(a_ref, gb_ref, yb_ref, h_ref, kvm_ref, wo_ref, wxq_ref, wxo_ref, wf1_ref, wf2_ref,
                  g1_ref, b1_ref, g2_ref, b2_ref, g3_ref, b3_ref, o_ref):
    tm = a_ref.shape[0]
    halves = [slice(i * (tm // TRUNK_SPLIT), (i + 1) * (tm // TRUNK_SPLIT)) for i in range(TRUNK_SPLIT)]
    both = lambda fn, *xs: [fn(*(x[i] for x in xs)) for i in range(TRUNK_SPLIT)]

    mix = [(a_ref[r, :] + gb_ref[r, :] * yb_ref[r, :]).astype(_BF16) for r in halves]
    y1 = both(lambda m: _dot(m, wo_ref[...]), mix)
    h1 = [_layer_norm(ALPHA * h_ref[r, :] + y, g1_ref[...], b1_ref[...]) for r, y in zip(halves, y1)]

    qx = both(lambda x: _dot(x.astype(_BF16), wxq_ref[...]).astype(_BF16), h1)
    heads = [[] for _ in range(TRUNK_SPLIT)]
    for hh in range(XATTN_HEADS):
        cs = slice(hh * XATTN_HEAD_DIM, (hh + 1) * XATTN_HEAD_DIM)
        vcs = slice(D_MODEL + hh * XATTN_HEAD_DIM, D_MODEL + (hh + 1) * XATTN_HEAD_DIM)
        s = both(lambda q: _dot_nt(q[:, cs], kvm_ref[:, cs]) * (XATTN_HEAD_DIM ** -0.5), qx)
        e = both(lambda x: jnp.exp(x - jnp.max(x, axis=-1, keepdims=True)), s)
        p = both(lambda x: (x * (1.0 / jnp.sum(x, axis=-1, keepdims=True))).astype(_BF16), e)
        for i, o in enumerate(both(lambda x: _dot(x, kvm_ref[:, vcs]), p)):
            heads[i].append(o)
    xo = both(lambda hs: jnp.concatenate(hs, axis=1).astype(_BF16), heads)
    y2 = both(lambda x: _dot(x, wxo_ref[...]), xo)
    h2 = both(lambda x, y: _layer_norm(ALPHA * x + y, g2_ref[...], b2_ref[...]), h1, y2)

    h2b = both(lambda x: x.astype(_BF16), h2)
    ff = [None] * TRUNK_SPLIT
    for c in range(D_FF // D_MODEL):
        cs = slice(c * D_MODEL, (c + 1) * D_MODEL)
        act = both(lambda x: jnp.square(jnp.maximum(_dot(x, wf1_ref[:, cs]), 0.0)).astype(_BF16), h2b)
        part = both(lambda x: _dot(x, wf2_ref[cs, :]), act)
        ff = part if c == 0 else both(lambda x, y: x + y, ff, part)
    for r, x, y in zip(halves, h2, ff):
        o_ref[r, :] = _layer_norm(ALPHA * x + y, g3_ref[...], b3_ref[...])


def _trunk(a, gb, yb, h, kvm, weights, lns, seq, mem_len):
    n = a.shape[0]
    tm = TRUNK_TILE
    tiles_per_batch = seq // tm
    row = pl.BlockSpec((tm, D_MODEL), lambda i: (i, 0))
    in_specs = ([row, row, row, row, pl.BlockSpec((mem_len, 2 * D_MODEL), lambda i: (i // tiles_per_batch, 0))]
                + [_const_spec(w.shape) for w in weights] + [_const_spec(p.shape) for p in lns])
    return pl.pallas_call(
        _trunk_kernel, grid=(n // tm,), in_specs=in_specs, out_specs=row,
        out_shape=jax.ShapeDtypeStruct((n, D_MODEL), _F32),
        compiler_params=pltpu.CompilerParams(dimension_semantics=("arbitrary",), vmem_limit_bytes=VMEM_LIMIT_BYTES),
        name="trunk",
    )(a, gb, yb, h, kvm, *weights, *lns)


def _rope_constants():
    half = ROT_DIM // 2
    inv = ROPE_THETA ** (-jnp.arange(half, dtype=_F32) / half)
    d = jnp.arange(LANES) % HEAD_DIM
    inv_lane = jnp.where(d < ROT_DIM, inv[d % half], 0.0).astype(_F32)[None, :]
    neg_first = jnp.where(d < half, -1.0, 0.0).astype(_F32)[None, :]
    pos_second = jnp.where((d >= half) & (d < ROT_DIM), 1.0, 0.0).astype(_F32)[None, :]
    return inv_lane, neg_first, pos_second


def kernel(x, mem, positions, ln_in_g, ln_in_b, w_in, gmlp_ln_g, gmlp_ln_b, gmlp_ws, gmlp_bs, cmp_k_pe, cmp_k_w1, cmp_k_b1, cmp_k_w2, cmp_k_b2, cmp_v_pe, cmp_v_w1, cmp_v_b1, cmp_v_w2, cmp_v_b2, w_out, ln1_g, ln1_b, w_xq, w_xkv, w_xo, ln2_g, ln2_b, w_ff1, w_ff2, ln3_g, ln3_b):
    batch, seq, _ = x.shape
    mem_len = mem.shape[1]
    n = batch * seq
    n_slc = seq // SLC_BLOCK
    assert w_in.shape[0] == 1, "one layer"
    assert seq % SEL_CHUNK == 0 and TRIP_CHUNKS[-1] == 2 and seq >= WIN_SPAN and seq & (seq - 1) == 0 and n_slc <= LANES

    rope_consts = _rope_constants()
    vec = lambda p: p.reshape(1, -1)

    wi = w_in[0]
    o_u, o_v, o_q, o_kv, o_gn, o_ga, o_gb = (0, 1024, 2048, 3072, 3072 + 6 * KV_DIM, 3120 + 6 * KV_DIM, 4144 + 6 * KV_DIM)
    w_gn = jnp.pad(wi[:, o_gn:o_ga], ((0, 0), (0, LANES - 3 * NSA_HEADS)))
    w_parts = [wi[:, o_u:o_v], wi[:, o_v:o_q], wi[:, o_ga:o_gb], wi[:, o_q:o_kv], wi[:, o_gb:], wi[:, o_kv:o_gn], w_gn]
    w_parts = [w.astype(_BF16) for w in w_parts]

    h, a, gb, qt, kvc, ks, vst, kw, vwt, gnt = _project(
        x.reshape(n, D_MODEL), positions.reshape(n, 1), vec(ln_in_g), vec(ln_in_b), w_parts,
        vec(gmlp_ln_g[0]), vec(gmlp_ln_b[0]), gmlp_ws[0], gmlp_bs[0].T, rope_consts, seq)

    rows = seq // CMP_STRIDE
    flat = CMP_STRIDE * HEAD_DIM
    pos_end = jnp.pad(positions[:, CMP_BLOCK - 1::CMP_STRIDE], ((0, 0), (0, 1)))[:, :, None]
    pad_lanes = lambda w: jnp.pad(w, ((0, 0), (0, LANES - HEAD_DIM)))
    kvcmp, kvcmp_t = _compress(
        kvc.reshape(2 * KV_HEADS, batch, rows, flat), pos_end,
        jnp.stack([cmp_k_pe[0].reshape(2, flat), cmp_v_pe[0].reshape(2, flat)]),
        jnp.stack([cmp_k_w1[0], cmp_v_w1[0]]).astype(_BF16),
        jnp.stack([vec(cmp_k_b1[0]), vec(cmp_v_b1[0])]),
        jnp.stack([pad_lanes(cmp_k_w2[0]), pad_lanes(cmp_v_w2[0])]).astype(_BF16),
        jnp.stack([pad_lanes(vec(cmp_k_b2[0])), pad_lanes(vec(cmp_v_b2[0]))]),
        rope_consts, batch)

    yb = _nsa(qt, gnt, kvcmp, kvcmp_t, ks, vst, kw, vwt, batch, seq)

    kvm = _memkv(mem.reshape(batch * mem_len, D_MODEL), w_xkv[0].astype(_BF16))
    weights = [w.astype(_BF16) for w in (w_out[0], w_xq[0], w_xo[0], w_ff1[0], w_ff2[0])]
    lns = [vec(p[0]) for p in (ln1_g, ln1_b, ln2_g, ln2_b, ln3_g, ln3_b)]
    out = _trunk(a, gb, yb, h, kvm, weights, lns, seq, mem_len)
    return out.reshape(batch, seq, D_MODEL)
```
